```python
import math
import jax, jax.numpy as jnp
from jax import lax
import numpy as np

D_MODEL = 2048
BATCH = 2
SEQ = 4096
DEPTH = 2

N_Q_HEADS = 16
N_KV_HEADS = 4
HEAD_DIM = 64
Q_PER_KV = N_Q_HEADS // N_KV_HEADS
WINDOW = 128
ATTN_WIDTH = N_Q_HEADS * HEAD_DIM
KV_WIDTH = N_KV_HEADS * HEAD_DIM
SSM_GROUP_CH = 16
SSM_GROUPS = 32
SSM_WIDTH = SSM_GROUPS * SSM_GROUP_CH
SSM_STATE = 64
IN_WIDTH = ATTN_WIDTH + 2 * KV_WIDTH + SSM_WIDTH + 2 * D_MODEL
N_EXPERTS = 64
TOP_K = 8
N_EXPERT_GROUPS = 8
TOPK_GROUPS = 4
EXPERT_DIM = 512
SHARED_DIM = 512
ROUTED_SCALE = 2.5
MOE_BLOCK = 128
NORM_EPS = 1e-6

kernel_name = "hybrid_swa_s5_moe_adaln"


def rmsnorm(x, g):
    xf = x.astype(jnp.float32)
    xf = xf * lax.rsqrt(jnp.mean(xf * xf, axis=-1, keepdims=True) + NORM_EPS)
    return (xf * g.astype(jnp.float32)).astype(x.dtype)


def modulate(h, shift, scale):
    return h * (1.0 + scale[:, None, :]) + shift[:, None, :]


def swiglu(x, wg, wu, wd):
    return (jax.nn.silu(x @ wg) * (x @ wu)) @ wd


def sliding_window_gqa(q, k, v, sinks):
    B, L = q.shape[0], q.shape[1]
    nb = L // WINDOW
    qb = q.reshape(B, nb, WINDOW, N_KV_HEADS, Q_PER_KV, HEAD_DIM)
    kb = k.reshape(B, nb, WINDOW, N_KV_HEADS, HEAD_DIM)
    vb = v.reshape(B, nb, WINDOW, N_KV_HEADS, HEAD_DIM)
    pad = ((0, 0), (1, 0), (0, 0), (0, 0), (0, 0))
    k_cat = jnp.concatenate([jnp.pad(kb, pad)[:, :-1], kb], axis=2)
    v_cat = jnp.concatenate([jnp.pad(vb, pad)[:, :-1], vb], axis=2)
    s = jnp.einsum('bnqhgd,bnkhd->bnhgqk', qb, k_cat).astype(jnp.float32) / math.sqrt(HEAD_DIM)
    i = jnp.arange(WINDOW)[:, None]
    j = jnp.arange(2 * WINDOW)[None, :]
    band = (j > i) & (j <= i + WINDOW)
    not_first = (jnp.arange(nb) > 0)[:, None, None]
    mask = band[None] & (not_first | (j >= WINDOW)[None])
    s = jnp.where(mask[None, :, None, None], s, jnp.finfo(jnp.float32).min)
    sink = jnp.broadcast_to(sinks.astype(jnp.float32).reshape(1, 1, N_KV_HEADS, Q_PER_KV, 1, 1),
                            s.shape[:-1] + (1,))
    p = jax.nn.softmax(jnp.concatenate([s, sink], axis=-1), axis=-1)[..., :-1]
    o = jnp.einsum('bnhgqk,bnkhd->bnqhgd', p.astype(v.dtype), v_cat)
    return o.reshape(B, L, ATTN_WIDTH)


def _complex_affine_combine(earlier, later):
    a1r, a1i, b1r, b1i = earlier
    a2r, a2i, b2r, b2i = later
    return (a2r * a1r - a2i * a1i,
            a2r * a1i + a2i * a1r,
            a2r * b1r - a2i * b1i + b2r,
            a2r * b1i + a2i * b1r + b2i)


def s5_branch(u, a_re, a_im, log_dt, b_re, b_im, c_re, c_im, d_skip, w_glu):
    B, L = u.shape[0], u.shape[1]
    f32 = jnp.float32
    uf = u.astype(f32).reshape(B, L, SSM_GROUPS, SSM_GROUP_CH)
    are, aim = a_re.astype(f32), a_im.astype(f32)
    dt = jnp.exp(log_dt.astype(f32))[:, None]
    mag = jnp.exp(dt * are)
    ab_r, ab_i = mag * jnp.cos(dt * aim), mag * jnp.sin(dt * aim)
    den = are * are + aim * aim
    nr, ni = ab_r - 1.0, ab_i
    f_r = (nr * are + ni * aim) / den
    f_i = (ni * are - nr * aim) / den
    br, bi = b_re.astype(f32), b_im.astype(f32)
    bb_r = f_r[..., None] * br - f_i[..., None] * bi
    bb_i = f_r[..., None] * bi + f_i[..., None] * br
    bu_r = jnp.einsum('blgc,gpc->blgp', uf, bb_r)
    bu_i = jnp.einsum('blgc,gpc->blgp', uf, bb_i)
    a_r = jnp.broadcast_to(ab_r, bu_r.shape)
    a_i = jnp.broadcast_to(ab_i, bu_i.shape)
    _, _, x_r, x_i = lax.associative_scan(_complex_affine_combine, (a_r, a_i, bu_r, bu_i), axis=1)
    y = (jnp.einsum('blgp,gcp->blgc', x_r, c_re.astype(f32))
         - jnp.einsum('blgp,gcp->blgc', x_i, c_im.astype(f32))
         + d_skip.astype(f32).reshape(SSM_GROUPS, SSM_GROUP_CH) * uf)
    y = jax.nn.gelu(y).reshape(B, L, SSM_WIDTH).astype(u.dtype)
    za, zb = jnp.split(y @ w_glu, 2, axis=-1)
    return za * jax.nn.sigmoid(zb)


def hybrid_mixer(h, w_in, b_in, attn_sinks, w_attn_o, a_re, a_im, log_dt, b_re, b_im,
                 c_re, c_im, d_skip, w_glu, w_mix_out):
    B, L = h.shape[0], h.shape[1]
    proj = h @ w_in + b_in
    o1 = ATTN_WIDTH
    o2 = o1 + KV_WIDTH
    o3 = o2 + KV_WIDTH
    o4 = o3 + SSM_WIDTH
    o5 = o4 + D_MODEL
    q, k, v, u, g_attn, g_ssm = jnp.split(proj, [o1, o2, o3, o4, o5], axis=-1)
    attn = sliding_window_gqa(q.reshape(B, L, N_Q_HEADS, HEAD_DIM),
                              k.reshape(B, L, N_KV_HEADS, HEAD_DIM),
                              v.reshape(B, L, N_KV_HEADS, HEAD_DIM), attn_sinks) @ w_attn_o
    ssm = s5_branch(u, a_re, a_im, log_dt, b_re, b_im, c_re, c_im, d_skip, w_glu)
    mixed = jax.nn.sigmoid(g_attn) * attn + jax.nn.sigmoid(g_ssm) * ssm
    return mixed @ w_mix_out


def moe_ffn(h, w_router, b_router, w_exp_gate, w_exp_up, w_exp_down, w_sh_gate, w_sh_up, w_sh_down):
    B, L, D = h.shape
    T = B * L
    xs = h.reshape(T, D)
    scores = jax.nn.sigmoid((xs @ w_router).astype(jnp.float32))
    sel = scores + b_router.astype(jnp.float32)
    grp = sel.reshape(T, N_EXPERT_GROUPS, N_EXPERTS // N_EXPERT_GROUPS)
    gscore = lax.top_k(grp, 2)[0].sum(axis=-1)
    _, gidx = lax.top_k(gscore, TOPK_GROUPS)
    gmask = jnp.any(gidx[:, :, None] == jnp.arange(N_EXPERT_GROUPS)[None, None, :], axis=1)
    emask = jnp.repeat(gmask, N_EXPERTS // N_EXPERT_GROUPS, axis=1)
    _, eidx = lax.top_k(jnp.where(emask, sel, -jnp.inf), TOP_K)
    wts = jnp.take_along_axis(scores, eidx, axis=-1)
    wts = wts / jnp.sum(wts, axis=-1, keepdims=True) * ROUTED_SCALE
    n_assign = T * TOP_K
    e_flat = eidx.reshape(-1)
    w_flat = wts.reshape(-1)
    tok_flat = jnp.arange(n_assign, dtype=jnp.int32) // TOP_K
    order = jnp.argsort(e_flat, stable=True)
    e_sorted = e_flat[order]
    counts = jnp.bincount(e_flat, length=N_EXPERTS)
    padded = (counts + MOE_BLOCK - 1) // MOE_BLOCK * MOE_BLOCK
    start = jnp.cumsum(counts) - counts
    pstart = jnp.cumsum(padded) - padded
    pend = pstart + padded
    dest = pstart[e_sorted] + jnp.arange(n_assign, dtype=jnp.int32) - start[e_sorted]
    n_slots = n_assign + N_EXPERTS * MOE_BLOCK
    n_blocks = n_slots // MOE_BLOCK
    slot_tok = jnp.full((n_slots,), T, dtype=jnp.int32).at[dest].set(tok_flat[order])
    slot_w = jnp.zeros((n_slots,), jnp.float32).at[dest].set(w_flat[order])
    blk_start = jnp.arange(n_blocks, dtype=jnp.int32) * MOE_BLOCK
    blk_e = jnp.minimum(jnp.sum(pend[None, :] <= blk_start[:, None], axis=1), N_EXPERTS - 1)
    xs_pad = jnp.concatenate([xs, jnp.zeros((1, D), xs.dtype)], axis=0)
    xb = xs_pad[slot_tok].reshape(n_blocks, MOE_BLOCK, D)

    def expert_block(args):
        xblk, e = args
        return swiglu(xblk, w_exp_gate[e], w_exp_up[e], w_exp_down[e])

    yb = lax.map(expert_block, (xb, blk_e)).reshape(n_slots, D)
    routed = jnp.zeros((T + 1, D), xs.dtype).at[slot_tok].add(yb * slot_w[:, None].astype(xs.dtype))[:T]
    shared = swiglu(xs, w_sh_gate, w_sh_up, w_sh_down)
    return (routed + shared).reshape(B, L, D)


def setup_inputs(seed: int = 0) -> dict:
    key = jax.random.key(seed)
    ks = jax.random.split(key, 32)
    D = D_MODEL

    def nrm(k, shape, scale):
        return jax.random.normal(k, shape, jnp.float32) * scale

    n_idx = jnp.arange(SSM_STATE, dtype=jnp.float32)
    return {
        "x": nrm(ks[0], (BATCH, SEQ, D), 1.0),
        "c": nrm(ks[1], (BATCH, D), 1.0),
        "w_ada": nrm(ks[2], (DEPTH, D, 6 * D), 0.5 * D ** -0.5),
        "b_ada": nrm(ks[3], (DEPTH, 6 * D), 0.02),
        "norm_mix": 1.0 + nrm(ks[4], (DEPTH, D), 0.01),
        "w_in": nrm(ks[5], (DEPTH, D, IN_WIDTH), D ** -0.5),
        "b_in": nrm(ks[6], (DEPTH, IN_WIDTH), 0.02),
        "attn_sinks": nrm(ks[7], (DEPTH, N_Q_HEADS), 0.5),
        "w_attn_o": nrm(ks[8], (DEPTH, ATTN_WIDTH, D), ATTN_WIDTH ** -0.5),
        "ssm_a_re": -0.5 + nrm(ks[9], (DEPTH, SSM_GROUPS, SSM_STATE), 0.01),
        "ssm_a_im": math.pi * n_idx + nrm(ks[10], (DEPTH, SSM_GROUPS, SSM_STATE), 0.01),
        "ssm_log_dt": jax.random.uniform(ks[11], (DEPTH, SSM_GROUPS), jnp.float32,
                                         math.log(1e-3), math.log(1e-1)),
        "ssm_b_re": nrm(ks[12], (DEPTH, SSM_GROUPS, SSM_STATE, SSM_GROUP_CH), (2 * SSM_GROUP_CH) ** -0.5),
        "ssm_b_im": nrm(ks[13], (DEPTH, SSM_GROUPS, SSM_STATE, SSM_GROUP_CH), (2 * SSM_GROUP_CH) ** -0.5),
        "ssm_c_re": nrm(ks[14], (DEPTH, SSM_GROUPS, SSM_GROUP_CH, SSM_STATE), 1.0),
        "ssm_c_im": nrm(ks[15], (DEPTH, SSM_GROUPS, SSM_GROUP_CH, SSM_STATE), 1.0),
        "ssm_d": nrm(ks[16], (DEPTH, SSM_WIDTH), 0.5),
        "w_glu": nrm(ks[17], (DEPTH, SSM_WIDTH, 2 * D), SSM_WIDTH ** -0.5),
        "w_mix_out": nrm(ks[18], (DEPTH, D, D), D ** -0.5),
        "norm_ffn": 1.0 + nrm(ks[19], (DEPTH, D), 0.01),
        "w_router": nrm(ks[20], (DEPTH, D, N_EXPERTS), D ** -0.5),
        "b_router": nrm(ks[21], (DEPTH, N_EXPERTS), 0.01),
        "w_exp_gate": nrm(ks[22], (DEPTH, N_EXPERTS, D, EXPERT_DIM), D ** -0.5),
        "w_exp_up": nrm(ks[23], (DEPTH, N_EXPERTS, D, EXPERT_DIM), D ** -0.5),
        "w_exp_down": nrm(ks[24], (DEPTH, N_EXPERTS, EXPERT_DIM, D), EXPERT_DIM ** -0.5),
        "w_sh_gate": nrm(ks[25], (DEPTH, D, SHARED_DIM), D ** -0.5),
        "w_sh_up": nrm(ks[26], (DEPTH, D, SHARED_DIM), D ** -0.5),
        "w_sh_down": nrm(ks[27], (DEPTH, SHARED_DIM, D), SHARED_DIM ** -0.5),
        "norm_final": 1.0 + nrm(ks[28], (D,), 0.01),
    }


def reference(x, c, w_ada, b_ada, norm_mix, w_in, b_in, attn_sinks, w_attn_o, ssm_a_re, ssm_a_im,
              ssm_log_dt, ssm_b_re, ssm_b_im, ssm_c_re, ssm_c_im, ssm_d, w_glu, w_mix_out, norm_ffn,
              w_router, b_router, w_exp_gate, w_exp_up, w_exp_down, w_sh_gate, w_sh_up, w_sh_down,
              norm_final):
    c_act = jax.nn.silu(c)
    for l in range(DEPTH):
        mod = c_act @ w_ada[l] + b_ada[l]
        sh1, sc1, gt1, sh2, sc2, gt2 = jnp.split(mod, 6, axis=-1)
        h = modulate(rmsnorm(x, norm_mix[l]), sh1, sc1)
        x = x + gt1[:, None, :] * hybrid_mixer(
            h, w_in[l], b_in[l], attn_sinks[l], w_attn_o[l], ssm_a_re[l], ssm_a_im[l], ssm_log_dt[l],
            ssm_b_re[l], ssm_b_im[l], ssm_c_re[l], ssm_c_im[l], ssm_d[l], w_glu[l], w_mix_out[l])
        h = modulate(rmsnorm(x, norm_ffn[l]), sh2, sc2)
        x = x + gt2[:, None, :] * moe_ffn(
            h, w_router[l], b_router[l], w_exp_gate[l], w_exp_up[l], w_exp_down[l],
            w_sh_gate[l], w_sh_up[l], w_sh_down[l])
    return rmsnorm(x, norm_final)
```

```python
import functools
import math

import jax
import jax.numpy as jnp
from jax import lax
from jax.experimental import pallas as pl
from jax.experimental.pallas import tpu as pltpu

F32 = jnp.float32
BF16 = jnp.bfloat16
I32 = jnp.int32
HIGHEST = lax.Precision.HIGHEST

N_Q_HEADS = 16
N_KV_HEADS = 4
HEAD_DIM = 64
Q_PER_KV = N_Q_HEADS // N_KV_HEADS
WINDOW = 128
ATTN_WIDTH = N_Q_HEADS * HEAD_DIM
KV_WIDTH = N_KV_HEADS * HEAD_DIM
SSM_GROUP_CH = 16
SSM_STATE = 64
N_EXPERT_GROUPS = 8
TOPK_GROUPS = 4
TOP_K = 8
ROUTED_SCALE = 2.5
NORM_EPS = 1e-6

LANES = 128
SUBLANES = 8
VMEM_LIMIT_BYTES = 56 * 1024 * 1024

SSM_GROUPS_PER_BLOCK = LANES // SSM_GROUP_CH
SSM_SEGMENTS = SUBLANES
MOE_ROWS = 256
NEG_BIG = -1e30


def _sigmoid(x):
    return 1.0 / (1.0 + jnp.exp(-x))


def _silu(x):
    return x * _sigmoid(x)


def _params(sem):
    return pltpu.CompilerParams(dimension_semantics=sem, vmem_limit_bytes=VMEM_LIMIT_BYTES)


def _resident(shape):
    nd = len(shape)
    return pl.BlockSpec(shape, lambda *_: (0,) * nd, pipeline_mode=pl.Buffered(1))


def _ada_kernel(c_ref, w_ref, b_ref, o_ref):
    c_act = _silu(c_ref[...]).astype(BF16)
    o_ref[0] = jnp.dot(c_act, w_ref[0].astype(BF16), preferred_element_type=F32) + b_ref[0]


def _ada_mod(c_pad, w_ada, b_ada):
    depth, d, n = w_ada.shape
    tn = 1024
    return pl.pallas_call(
        _ada_kernel,
        grid=(depth, n // tn),
        in_specs=[
            pl.BlockSpec((SUBLANES, d), lambda l, j: (0, 0)),
            pl.BlockSpec((1, d, tn), lambda l, j: (l, 0, j)),
            pl.BlockSpec((1, 1, tn), lambda l, j: (l, 0, j)),
        ],
        out_specs=pl.BlockSpec((1, SUBLANES, tn), lambda l, j: (l, 0, j)),
        out_shape=jax.ShapeDtypeStruct((depth, SUBLANES, n), F32),
        compiler_params=_params(("arbitrary", "arbitrary")),
    )(c_pad, w_ada, b_ada.reshape(depth, 1, n))


def _norm_mod(x, g, shift, scale):
    ms = jnp.mean(x * x, axis=-1, keepdims=True)
    xn = x * lax.rsqrt(ms + NORM_EPS) * g
    return xn * (1.0 + scale) + shift


def _inproj_kernel(x_ref, g_ref, sh_ref, sc_ref, w_ref, b_ref, o_ref, h_scr):
    @pl.when(pl.program_id(1) == 0)
    def _():
        h_scr[...] = _norm_mod(x_ref[...], g_ref[...], sh_ref[0], sc_ref[0]).astype(BF16)

    acc = jnp.dot(h_scr[...], w_ref[...].astype(BF16), preferred_element_type=F32)
    o_ref[...] = (acc + b_ref[...]).astype(o_ref.dtype)


def _inproj(x2, g, shift, scale, w, b, seq):
    t, d = x2.shape
    n = w.shape[1]
    tm, tn = 1024, 512
    per_b = seq // tm
    return pl.pallas_call(
        _inproj_kernel,
        grid=(t // tm, n // tn),
        in_specs=[
            pl.BlockSpec((tm, d), lambda i, j: (i, 0)),
            pl.BlockSpec((1, d), lambda i, j: (0, 0)),
            pl.BlockSpec((1, 1, d), lambda i, j: (i // per_b, 0, 0)),
            pl.BlockSpec((1, 1, d), lambda i, j: (i // per_b, 0, 0)),
            pl.BlockSpec((d, tn), lambda i, j: (0, j)),
            pl.BlockSpec((1, tn), lambda i, j: (0, j)),
        ],
        out_specs=pl.BlockSpec((tm, tn), lambda i, j: (i, j)),
        out_shape=jax.ShapeDtypeStruct((t, n), BF16),
        scratch_shapes=[pltpu.VMEM((tm, d), BF16)],
        compiler_params=_params(("arbitrary", "arbitrary")),
    )(x2, g.reshape(1, d), shift, scale, w, b.reshape(1, n))


def _attn_kernel(sink_ref, q_ref, kc_ref, vc_ref, kp_ref, vp_ref, g_ref, wo_ref, o_ref):
    i = pl.program_id(1)
    q = q_ref[...]
    k_cat = jnp.concatenate([kp_ref[...], kc_ref[...]], axis=0)
    v_cat = jnp.concatenate([vp_ref[...], vc_ref[...]], axis=0)
    row = lax.broadcasted_iota(I32, (WINDOW, 2 * WINDOW), 0)
    col = lax.broadcasted_iota(I32, (WINDOW, 2 * WINDOW), 1)
    mask = (col > row) & (col <= row + WINDOW) & ((col >= WINDOW) | (i > 0))
    inv_sqrt = 1.0 / math.sqrt(HEAD_DIM)

    outs = []
    for h in range(N_Q_HEADS):
        j = h // Q_PER_KV
        qh = q[:, h * HEAD_DIM:(h + 1) * HEAD_DIM]
        kj = k_cat[:, j * HEAD_DIM:(j + 1) * HEAD_DIM]
        vj = v_cat[:, j * HEAD_DIM:(j + 1) * HEAD_DIM]
        s = lax.dot_general(qh, kj, (((1,), (1,)), ((), ())), preferred_element_type=F32) * inv_sqrt
        s = jnp.where(mask, s, NEG_BIG)
        sink = sink_ref[h]
        m = jnp.maximum(jnp.max(s, axis=-1, keepdims=True), sink)
        p = jnp.exp(s - m)
        denom = jnp.sum(p, axis=-1, keepdims=True) + jnp.exp(sink - m)
        o = jnp.dot(p.astype(BF16), vj, preferred_element_type=F32) / denom
        outs.append(o)
    attn = jnp.concatenate(outs, axis=-1).astype(BF16)
    proj = jnp.dot(attn, wo_ref[...], preferred_element_type=F32)
    o_ref[...] = (_sigmoid(g_ref[...].astype(F32)) * proj).astype(o_ref.dtype)


def _attention(proj, sinks, w_o, batch, seq, d, gcol):
    t = proj.shape[0]
    nb = seq // WINDOW
    kcol = ATTN_WIDTH // KV_WIDTH
    vcol = kcol + 1
    cur = lambda b, i: b * nb + i
    prev = lambda b, i: b * nb + jnp.maximum(i - 1, 0)
    return pl.pallas_call(
        _attn_kernel,
        grid=(batch, nb),
        in_specs=[
            pl.BlockSpec(memory_space=pltpu.SMEM),
            pl.BlockSpec((WINDOW, ATTN_WIDTH), lambda b, i: (cur(b, i), 0)),
            pl.BlockSpec((WINDOW, KV_WIDTH), lambda b, i: (cur(b, i), kcol)),
            pl.BlockSpec((WINDOW, KV_WIDTH), lambda b, i: (cur(b, i), vcol)),
            pl.BlockSpec((WINDOW, KV_WIDTH), lambda b, i: (prev(b, i), kcol)),
            pl.BlockSpec((WINDOW, KV_WIDTH), lambda b, i: (prev(b, i), vcol)),
            pl.BlockSpec((WINDOW, d), lambda b, i: (cur(b, i), gcol)),
            _resident(w_o.shape),
        ],
        out_specs=pl.BlockSpec((WINDOW, d), lambda b, i: (cur(b, i), 0)),
        out_shape=jax.ShapeDtypeStruct((t, d), BF16),
        compiler_params=_params(("arbitrary", "arbitrary")),
    )(sinks, proj, proj, proj, proj, proj, proj, w_o.astype(BF16))


def _ssm_kernel(u_ref, ar_ref, ai_ref, br_ref, bi_ref, cr_ref, cin_ref, d_ref, y_ref,
                uperm, bur, bui, yperm):
    seq = u_ref.shape[0]
    nseg = SSM_SEGMENTS
    n = seq // nseg
    ns = bur.shape[1]

    for r in range(nseg):
        uperm[pl.ds(r, n, stride=nseg), :] = u_ref[r * n:(r + 1) * n, :].astype(F32)

    chunk = 512

    def bu_body(c, _):
        r0 = pl.multiple_of(c * chunk, chunk)
        up = uperm[pl.ds(r0, chunk), :]
        bur[pl.ds(r0, chunk), :] = jnp.dot(up, br_ref[0], precision=HIGHEST, preferred_element_type=F32)
        bui[pl.ds(r0, chunk), :] = jnp.dot(up, bi_ref[0], precision=HIGHEST, preferred_element_type=F32)
        return 0

    lax.fori_loop(0, seq // chunk, bu_body, 0)

    a_r1 = ar_ref[0]
    a_i1 = ai_ref[0]
    a_r = jnp.broadcast_to(a_r1, (nseg, ns))
    a_i = jnp.broadcast_to(a_i1, (nseg, ns))

    def scan_body(s, carry):
        xr, xi = carry
        r0 = pl.multiple_of(s * nseg, nseg)
        nxr = a_r * xr - a_i * xi + bur[pl.ds(r0, nseg), :]
        nxi = a_r * xi + a_i * xr + bui[pl.ds(r0, nseg), :]
        bur[pl.ds(r0, nseg), :] = nxr
        bui[pl.ds(r0, nseg), :] = nxi
        return nxr, nxi

    zeros = jnp.zeros((nseg, ns), F32)
    er, ei = lax.fori_loop(0, n, scan_body, (zeros, zeros), unroll=4)

    pr, pi = a_r1, a_i1
    for _ in range(n.bit_length() - 1):
        pr, pi = pr * pr - pi * pi, 2.0 * pr * pi

    cr_rows = [jnp.zeros((1, ns), F32)]
    ci_rows = [jnp.zeros((1, ns), F32)]
    for r in range(1, nseg):
        pcr, pci = cr_rows[-1], ci_rows[-1]
        cr_rows.append(er[r - 1:r, :] + pr * pcr - pi * pci)
        ci_rows.append(ei[r - 1:r, :] + pr * pci + pi * pcr)
    c_r = jnp.concatenate(cr_rows, axis=0)
    c_i = jnp.concatenate(ci_rows, axis=0)

    def fix_body(s, carry):
        qr, qi = carry
        r0 = pl.multiple_of(s * nseg, nseg)
        bur[pl.ds(r0, nseg), :] = bur[pl.ds(r0, nseg), :] + (qr * c_r - qi * c_i)
        bui[pl.ds(r0, nseg), :] = bui[pl.ds(r0, nseg), :] + (qr * c_i + qi * c_r)
        return qr * a_r - qi * a_i, qr * a_i + qi * a_r

    lax.fori_loop(0, n, fix_body, (a_r, a_i), unroll=4)

    def out_body(c, _):
        r0 = pl.multiple_of(c * chunk, chunk)
        y = jnp.dot(bur[pl.ds(r0, chunk), :], cr_ref[0], precision=HIGHEST, preferred_element_type=F32)
        y = y + jnp.dot(bui[pl.ds(r0, chunk), :], cin_ref[0], precision=HIGHEST, preferred_element_type=F32)
        y = y + d_ref[0] * uperm[pl.ds(r0, chunk), :]
        yperm[pl.ds(r0, chunk), :] = jax.nn.gelu(y, approximate=True)
        return 0

    lax.fori_loop(0, seq // chunk, out_body, 0)

    for r in range(nseg):
        y_ref[r * n:(r + 1) * n, :] = yperm[pl.ds(r, n, stride=nseg), :].astype(y_ref.dtype)


def _ssm_params(a_re, a_im, log_dt, b_re, b_im, c_re, c_im, d_skip):
    g, p = a_re.shape
    c = b_re.shape[-1]
    gpb = SSM_GROUPS_PER_BLOCK
    nblk = g // gpb
    dt = jnp.exp(log_dt)[:, None]
    mag = jnp.exp(dt * a_re)
    ab_r, ab_i = mag * jnp.cos(dt * a_im), mag * jnp.sin(dt * a_im)
    den = a_re * a_re + a_im * a_im
    nr, ni = ab_r - 1.0, ab_i
    f_r = (nr * a_re + ni * a_im) / den
    f_i = (ni * a_re - nr * a_im) / den
    bb_r = f_r[..., None] * b_re - f_i[..., None] * b_im
    bb_i = f_r[..., None] * b_im + f_i[..., None] * b_re
    eye = jnp.eye(gpb, dtype=F32)

    def in_mat(bb):
        return jnp.einsum("bgpc,gh->bgchp", bb.reshape(nblk, gpb, p, c), eye).reshape(nblk, gpb * c, gpb * p)

    def out_mat(cc):
        return jnp.einsum("bgcp,gh->bgphc", cc.reshape(nblk, gpb, c, p), eye).reshape(nblk, gpb * p, gpb * c)

    return (ab_r.reshape(nblk, 1, gpb * p), ab_i.reshape(nblk, 1, gpb * p),
            in_mat(bb_r), in_mat(bb_i), out_mat(c_re), out_mat(-c_im),
            d_skip.reshape(nblk, 1, gpb * c))


def _ssm(proj, ssm_mats, batch, seq, ucol0):
    ar, ai, br, bi, cr, cin, dsk = ssm_mats
    nblk = ar.shape[0]
    nu, ns = br.shape[1], br.shape[2]
    ublk0 = ucol0 // nu
    blk3 = lambda shape: pl.BlockSpec((1,) + shape, lambda b, g: (g, 0, 0))
    return pl.pallas_call(
        _ssm_kernel,
        grid=(batch, nblk),
        in_specs=[
            pl.BlockSpec((seq, nu), lambda b, g: (b, ublk0 + g)),
            blk3((1, ns)), blk3((1, ns)), blk3((nu, ns)), blk3((nu, ns)),
            blk3((ns, nu)), blk3((ns, nu)), blk3((1, nu)),
        ],
        out_specs=pl.BlockSpec((seq, nu), lambda b, g: (b, g)),
        out_shape=jax.ShapeDtypeStruct((batch * seq, nblk * nu), BF16),
        scratch_shapes=[
            pltpu.VMEM((seq, nu), F32), pltpu.VMEM((seq, ns), F32),
            pltpu.VMEM((seq, ns), F32), pltpu.VMEM((seq, nu), F32),
        ],
        compiler_params=_params(("arbitrary", "arbitrary")),
    )(proj, ar, ai, br, bi, cr, cin, dsk)


def _mixout_kernel(a_ref, y_ref, gs_ref, x_ref, gt_ref, wglu_ref, wmix_ref, g2_ref, sh_ref, sc_ref,
                   wr_ref, x1_ref, h2_ref, lg_ref):
    d = x_ref.shape[1]
    z = jnp.dot(y_ref[...], wglu_ref[...], preferred_element_type=F32)
    ssm = z[:, :d] * _sigmoid(z[:, d:])
    mixed = a_ref[...].astype(F32) + _sigmoid(gs_ref[...].astype(F32)) * ssm
    r = jnp.dot(mixed.astype(BF16), wmix_ref[...], preferred_element_type=F32)
    x1 = x_ref[...] + gt_ref[0] * r
    x1_ref[...] = x1
    h2 = _norm_mod(x1, g2_ref[...], sh_ref[0], sc_ref[0])
    h2_ref[...] = h2
    lg_ref[...] = lax.dot_general(wr_ref[...], h2, (((1,), (1,)), ((), ())),
                                  precision=HIGHEST, preferred_element_type=F32)


def _mixout(a_gated, y, proj, x2, gt1, w_glu, w_mix, g2, sh2, sc2, w_router_t, seq, gscol):
    t, d = x2.shape
    tm = 256
    per_b = seq // tm
    ne = w_router_t.shape[0]
    row = lambda i: (i, 0)
    bvec = pl.BlockSpec((1, 1, d), lambda i: (i // per_b, 0, 0))
    return pl.pallas_call(
        _mixout_kernel,
        grid=(t // tm,),
        in_specs=[
            pl.BlockSpec((tm, d), row),
            pl.BlockSpec((tm, y.shape[1]), row),
            pl.BlockSpec((tm, d), lambda i: (i, gscol)),
            pl.BlockSpec((tm, d), row),
            bvec,
            _resident(w_glu.shape),
            _resident(w_mix.shape),
            pl.BlockSpec((1, d), lambda i: (0, 0)),
            bvec, bvec,
            _resident(w_router_t.shape),
        ],
        out_specs=[
            pl.BlockSpec((tm, d), row),
            pl.BlockSpec((tm, d), row),
            pl.BlockSpec((ne, tm), lambda i: (0, i)),
        ],
        out_shape=[
            jax.ShapeDtypeStruct((t, d), F32),
            jax.ShapeDtypeStruct((t, d), F32),
            jax.ShapeDtypeStruct((ne, t), F32),
        ],
        compiler_params=_params(("arbitrary",)),
    )(a_gated, y, proj, x2, gt1, w_glu.astype(BF16), w_mix.astype(BF16), g2.reshape(1, d), sh2, sc2,
      w_router_t)


def _router_kernel(lg_ref, b_ref, idx_ref, w_ref):
    ng, gs, tn = lg_ref.shape
    scores = _sigmoid(lg_ref[...])
    sel = scores + b_ref[...]
    mem = lax.broadcasted_iota(I32, (ng, gs, tn), 1)
    grp = lax.broadcasted_iota(I32, (ng, gs, tn), 0)
    eid = grp * gs + mem
    ninf = -jnp.inf

    m1 = jnp.max(sel, axis=1, keepdims=True)
    first = jnp.min(jnp.where(sel == m1, mem, gs), axis=1, keepdims=True)
    m2 = jnp.max(jnp.where(mem == first, ninf, sel), axis=1, keepdims=True)
    gscore = m1 + m2

    gid = lax.broadcasted_iota(I32, (ng, 1, tn), 0)
    gmask = jnp.zeros((ng, 1, tn), jnp.bool_)
    for _ in range(TOPK_GROUPS):
        m = jnp.max(gscore, axis=0, keepdims=True)
        pick = gid == jnp.min(jnp.where(gscore == m, gid, ng), axis=0, keepdims=True)
        gmask = gmask | pick
        gscore = jnp.where(pick, ninf, gscore)

    cand = jnp.where(gmask, sel, ninf)
    idxs, wts = [], []
    for _ in range(TOP_K):
        m = jnp.max(jnp.max(cand, axis=1, keepdims=True), axis=0, keepdims=True)
        e = jnp.where(cand == m, eid, ng * gs)
        e = jnp.min(jnp.min(e, axis=1, keepdims=True), axis=0, keepdims=True)
        pick = eid == e
        w = jnp.sum(jnp.sum(jnp.where(pick, scores, 0.0), axis=1, keepdims=True), axis=0, keepdims=True)
        idxs.append(e[0])
        wts.append(w[0])
        cand = jnp.where(pick, ninf, cand)
    idx = jnp.concatenate(idxs, axis=0)
    w = jnp.concatenate(wts, axis=0)
    idx_ref[...] = idx
    w_ref[...] = w / jnp.sum(w, axis=0, keepdims=True) * ROUTED_SCALE


def _router(logits_t, b_router):
    ne, t = logits_t.shape
    ng = N_EXPERT_GROUPS
    gs = ne // ng
    tn = 512
    return pl.pallas_call(
        _router_kernel,
        grid=(t // tn,),
        in_specs=[
            pl.BlockSpec((ng, gs, tn), lambda i: (0, 0, i)),
            pl.BlockSpec((ng, gs, 1), lambda i: (0, 0, 0)),
        ],
        out_specs=[pl.BlockSpec((TOP_K, tn), lambda i: (0, i))] * 2,
        out_shape=[jax.ShapeDtypeStruct((TOP_K, t), I32), jax.ShapeDtypeStruct((TOP_K, t), F32)],
        compiler_params=_params(("arbitrary",)),
    )(logits_t.reshape(ng, gs, t), b_router.reshape(ng, gs, 1))


def _dispatch_plan(eidx, n_experts):
    k, t = eidx.shape
    onehot = jnp.sum((eidx[:, None, :] == jnp.arange(n_experts, dtype=I32)[None, :, None]).astype(I32), axis=0)
    csum = jnp.cumsum(onehot, axis=1)
    rank = csum - onehot
    counts = csum[:, -1]
    padded = (counts + MOE_ROWS - 1) // MOE_ROWS * MOE_ROWS
    pend = jnp.cumsum(padded)
    pstart = pend - padded
    dest = pstart[eidx] + jnp.take_along_axis(rank, eidx, axis=0)
    n_slots = k * t + n_experts * MOE_ROWS
    n_blocks = n_slots // MOE_ROWS
    tok = jnp.broadcast_to(jnp.arange(t, dtype=I32)[None, :], (k, t))
    slot_tok = jnp.zeros((n_slots,), I32).at[dest.reshape(-1)].set(tok.reshape(-1))
    blk_start = jnp.arange(n_blocks, dtype=I32) * MOE_ROWS
    blk_e = jnp.minimum(jnp.sum((pend[None, :] <= blk_start[:, None]).astype(I32), axis=1), n_experts - 1)
    n_used = (pend[-1] // MOE_ROWS).astype(I32)
    return dest.astype(I32), slot_tok, blk_e.astype(I32), n_used


def _row_gather_wait(src_hbm, buf, sem):
    pltpu.make_async_copy(src_hbm.at[pl.ds(0, buf.shape[0])], buf, sem).wait()


def _gather_kernel(tok_ref, h_hbm, o_ref, buf, sem):
    rows = buf.shape[0]

    def issue(r, _):
        pltpu.make_async_copy(h_hbm.at[pl.ds(tok_ref[0, 0, r], 1)], buf.at[pl.ds(r, 1)], sem).start()
        return 0

    lax.fori_loop(0, rows, issue, 0, unroll=8)
    _row_gather_wait(h_hbm, buf, sem)
    o_ref[...] = buf[...].astype(o_ref.dtype)


def _gather_rows(slot_tok, h2):
    n_slots = slot_tok.shape[0]
    d = h2.shape[1]
    nblk = n_slots // MOE_ROWS
    return pl.pallas_call(
        _gather_kernel,
        grid=(nblk,),
        in_specs=[
            pl.BlockSpec((1, 1, MOE_ROWS), lambda i: (i, 0, 0), memory_space=pltpu.SMEM),
            pl.BlockSpec(memory_space=pl.ANY),
        ],
        out_specs=pl.BlockSpec((MOE_ROWS, d), lambda i: (i, 0)),
        out_shape=jax.ShapeDtypeStruct((n_slots, d), BF16),
        scratch_shapes=[pltpu.VMEM((MOE_ROWS, d), F32), pltpu.SemaphoreType.DMA],
        compiler_params=_params(("arbitrary",)),
    )(slot_tok.reshape(nblk, 1, MOE_ROWS), h2)


def _expert_kernel(blk_e_ref, n_used_ref, x_ref, wg_ref, wu_ref, wd_ref, o_ref):
    i = pl.program_id(0)

    @pl.when(i < n_used_ref[0])
    def _():
        x = x_ref[...]
        gate = jnp.dot(x, wg_ref[0].astype(BF16), preferred_element_type=F32)
        up = jnp.dot(x, wu_ref[0].astype(BF16), preferred_element_type=F32)
        hid = (_silu(gate) * up).astype(BF16)
        o_ref[...] = jnp.dot(hid, wd_ref[0].astype(BF16), preferred_element_type=F32)

    @pl.when(i >= n_used_ref[0])
    def _():
        o_ref[...] = jnp.zeros(o_ref.shape, o_ref.dtype)


def _experts(blk_e, n_used, x_sorted, w_gate, w_up, w_down):
    n_slots, d = x_sorted.shape
    f = w_gate.shape[2]
    nblk = n_slots // MOE_ROWS
    grid_spec = pltpu.PrefetchScalarGridSpec(
        num_scalar_prefetch=2,
        grid=(nblk,),
        in_specs=[
            pl.BlockSpec((MOE_ROWS, d), lambda i, be, nu: (i, 0)),
            pl.BlockSpec((1, d, f), lambda i, be, nu: (be[i], 0, 0)),
            pl.BlockSpec((1, d, f), lambda i, be, nu: (be[i], 0, 0)),
            pl.BlockSpec((1, f, d), lambda i, be, nu: (be[i], 0, 0)),
        ],
        out_specs=pl.BlockSpec((MOE_ROWS, d), lambda i, be, nu: (i, 0)),
    )
    return pl.pallas_call(
        _expert_kernel,
        grid_spec=grid_spec,
        out_shape=jax.ShapeDtypeStruct((n_slots, d), F32),
        compiler_params=_params(("arbitrary",)),
    )(blk_e, n_used.reshape(1), x_sorted, w_gate, w_up, w_down)


def _combine_kernel(dest_ref, h_ref, wt_ref, x_ref, gt_ref, wg_ref, wu_ref, wd_ref, gf_ref, y_hbm,
                    o_ref, buf, sem, *, final_norm):
    k, tm, _ = buf.shape

    def issue(r, _):
        for kk in range(k):
            pltpu.make_async_copy(y_hbm.at[pl.ds(dest_ref[0, 0, r * k + kk], 1)],
                                  buf.at[kk, pl.ds(r, 1)], sem).start()
        return 0

    lax.fori_loop(0, tm, issue, 0, unroll=2)

    h = h_ref[...].astype(BF16)
    gate = jnp.dot(h, wg_ref[...], preferred_element_type=F32)
    up = jnp.dot(h, wu_ref[...], preferred_element_type=F32)
    acc = jnp.dot((_silu(gate) * up).astype(BF16), wd_ref[...], preferred_element_type=F32)

    for kk in range(k):
        _row_gather_wait(y_hbm, buf.at[kk], sem)
    wt = wt_ref[...]
    for kk in range(k):
        acc = acc + wt[:, kk:kk + 1] * buf[kk]
    out = x_ref[...] + gt_ref[0] * acc
    if final_norm:
        ms = jnp.mean(out * out, axis=-1, keepdims=True)
        out = out * lax.rsqrt(ms + NORM_EPS) * gf_ref[...]
    o_ref[...] = out


def _combine(dest, wts, h2, x1, gt2, w_sg, w_su, w_sd, g_final, y_sorted, seq, final_norm):
    t, d = x1.shape
    k = dest.shape[0]
    tm = 128
    per_b = seq // tm
    nblk = t // tm
    dest_tm = dest.T.reshape(nblk, 1, tm * k)
    row = lambda i: (i, 0)
    return pl.pallas_call(
        functools.partial(_combine_kernel, final_norm=final_norm),
        grid=(nblk,),
        in_specs=[
            pl.BlockSpec((1, 1, tm * k), lambda i: (i, 0, 0), memory_space=pltpu.SMEM),
            pl.BlockSpec((tm, d), row),
            pl.BlockSpec((tm, k), row),
            pl.BlockSpec((tm, d), row),
            pl.BlockSpec((1, 1, d), lambda i: (i // per_b, 0, 0)),
            _resident(w_sg.shape), _resident(w_su.shape), _resident(w_sd.shape),
            pl.BlockSpec((1, d), lambda i: (0, 0)),
            pl.BlockSpec(memory_space=pl.ANY),
        ],
        out_specs=pl.BlockSpec((tm, d), row),
        out_shape=jax.ShapeDtypeStruct((t, d), F32),
        scratch_shapes=[pltpu.VMEM((k, tm, d), F32), pltpu.SemaphoreType.DMA],
        compiler_params=_params(("arbitrary",)),
    )(dest_tm, h2, wts.T, x1, gt2, w_sg.astype(BF16), w_su.astype(BF16), w_sd.astype(BF16),
      g_final.reshape(1, d), y_sorted)


def kernel(x, c, w_ada, b_ada, norm_mix, w_in, b_in, attn_sinks, w_attn_o, ssm_a_re, ssm_a_im, ssm_log_dt,
           ssm_b_re, ssm_b_im, ssm_c_re, ssm_c_im, ssm_d, w_glu, w_mix_out, norm_ffn, w_router, b_router,
           w_exp_gate, w_exp_up, w_exp_down, w_sh_gate, w_sh_up, w_sh_down, norm_final):
    batch, seq, d = x.shape
    depth = w_ada.shape[0]
    n_experts = w_router.shape[2]
    ssm_width = ssm_d.shape[1]
    ucol0 = ATTN_WIDTH + 2 * KV_WIDTH
    gacol = (ucol0 + ssm_width) // d
    gscol = gacol + 1

    c_pad = jnp.zeros((SUBLANES, d), F32).at[:batch].set(c)
    mod = _ada_mod(c_pad, w_ada, b_ada)[:, :batch]
    mod = mod.reshape(depth, batch, 6, 1, d)

    x2 = x.reshape(batch * seq, d)
    for l in range(depth):
        sh1, sc1, gt1, sh2, sc2, gt2 = (mod[l, :, m] for m in range(6))
        proj = _inproj(x2, norm_mix[l], sh1, sc1, w_in[l], b_in[l], seq)
        a_gated = _attention(proj, attn_sinks[l], w_attn_o[l], batch, seq, d, gacol)
        ssm_mats = _ssm_params(ssm_a_re[l], ssm_a_im[l], ssm_log_dt[l], ssm_b_re[l], ssm_b_im[l],
                               ssm_c_re[l], ssm_c_im[l], ssm_d[l])
        y = _ssm(proj, ssm_mats, batch, seq, ucol0)
        x1, h2, logits_t = _mixout(a_gated, y, proj, x2, gt1, w_glu[l], w_mix_out[l], norm_ffn[l],
                                   sh2, sc2, w_router[l].T, seq, gscol)
        eidx, wts = _router(logits_t, b_router[l])
        dest, slot_tok, blk_e, n_used = _dispatch_plan(eidx, n_experts)
        x_sorted = _gather_rows(slot_tok, h2)
        y_sorted = _experts(blk_e, n_used, x_sorted, w_exp_gate[l], w_exp_up[l], w_exp_down[l])
        x2 = _combine(dest, wts, h2, x1, gt2, w_sh_gate[l], w_sh_up[l], w_sh_down[l], norm_final,
                      y_sorted, seq, final_norm=(l == depth - 1))
    return x2.reshape(batch, seq, d)
```

```python
import functools
import math

import jax
import jax.numpy as jnp
from jax import lax
from jax.experimental import pallas as pl
from jax.experimental.pallas import tpu as pltpu

F32 = jnp.float32
BF16 = jnp.bfloat16
I32 = jnp.int32
HIGHEST = lax.Precision.HIGHEST

N_Q_HEADS = 16
N_KV_HEADS = 4
HEAD_DIM = 64
Q_PER_KV = N_Q_HEADS // N_KV_HEADS
WINDOW = 128
ATTN_WIDTH = N_Q_HEADS * HEAD_DIM
KV_WIDTH = N_KV_HEADS * HEAD_DIM
SSM_GROUP_CH = 16
SSM_STATE = 64
N_EXPERT_GROUPS = 8
TOPK_GROUPS = 4
TOP_K = 8
ROUTED_SCALE = 2.5
NORM_EPS = 1e-6

LANES = 128
SUBLANES = 8
VMEM_LIMIT_BYTES = 56 * 1024 * 1024

SSM_GROUPS_PER_BLOCK = LANES // SSM_GROUP_CH
SSM_SEGMENTS = SUBLANES
MOE_SUB = 128
NEG_BIG = -1e30


def _sigmoid(x):
    return 1.0 / (1.0 + jnp.exp(-x))


def _silu(x):
    return x * _sigmoid(x)


def _params(sem):
    return pltpu.CompilerParams(dimension_semantics=sem, vmem_limit_bytes=VMEM_LIMIT_BYTES)


def _resident(shape):
    nd = len(shape)
    return pl.BlockSpec(shape, lambda *_: (0,) * nd, pipeline_mode=pl.Buffered(1))


def _ada_kernel(c_ref, w_ref, b_ref, o_ref):
    c_act = _silu(c_ref[...]).astype(BF16)
    o_ref[0] = jnp.dot(c_act, w_ref[0].astype(BF16), preferred_element_type=F32) + b_ref[0]


def _ada_mod(c_pad, w_ada, b_ada):
    depth, d, n = w_ada.shape
    tn = 1024
    return pl.pallas_call(
        _ada_kernel,
        grid=(depth, n // tn),
        in_specs=[
            pl.BlockSpec((SUBLANES, d), lambda l, j: (0, 0)),
            pl.BlockSpec((1, d, tn), lambda l, j: (l, 0, j)),
            pl.BlockSpec((1, 1, tn), lambda l, j: (l, 0, j)),
        ],
        out_specs=pl.BlockSpec((1, SUBLANES, tn), lambda l, j: (l, 0, j)),
        out_shape=jax.ShapeDtypeStruct((depth, SUBLANES, n), F32),
        compiler_params=_params(("arbitrary", "arbitrary")),
    )(c_pad, w_ada, b_ada.reshape(depth, 1, n))


def _norm_mod(x, g, shift, scale):
    ms = jnp.mean(x * x, axis=-1, keepdims=True)
    xn = x * lax.rsqrt(ms + NORM_EPS) * g
    return xn * (1.0 + scale) + shift


def _inproj_kernel(x_ref, g_ref, sh_ref, sc_ref, w_ref, b_ref, o_ref, h_scr):
    @pl.when(pl.program_id(1) == 0)
    def _():
        h_scr[...] = _norm_mod(x_ref[...], g_ref[...], sh_ref[0], sc_ref[0]).astype(BF16)

    acc = jnp.dot(h_scr[...], w_ref[0].astype(BF16), preferred_element_type=F32)
    o_ref[...] = (acc + b_ref[0]).astype(o_ref.dtype)


def _inproj(x2, g, shift, scale, w_all, b_all, layer, seq):
    t, d = x2.shape
    depth, _, n = w_all.shape
    tm, tn = 1024, 512
    per_b = seq // tm
    return pl.pallas_call(
        _inproj_kernel,
        grid=(t // tm, n // tn),
        in_specs=[
            pl.BlockSpec((tm, d), lambda i, j: (i, 0)),
            pl.BlockSpec((1, d), lambda i, j: (0, 0)),
            pl.BlockSpec((1, 1, d), lambda i, j: (i // per_b, 0, 0)),
            pl.BlockSpec((1, 1, d), lambda i, j: (i // per_b, 0, 0)),
            pl.BlockSpec((1, d, tn), lambda i, j: (layer, 0, j)),
            pl.BlockSpec((1, 1, tn), lambda i, j: (layer, 0, j)),
        ],
        out_specs=pl.BlockSpec((tm, tn), lambda i, j: (i, j)),
        out_shape=jax.ShapeDtypeStruct((t, n), BF16),
        scratch_shapes=[pltpu.VMEM((tm, d), BF16)],
        compiler_params=_params(("arbitrary", "arbitrary")),
    )(x2, g.reshape(1, d), shift, scale, w_all, b_all.reshape(depth, 1, n))


def _attn_kernel(sink_ref, q_ref, kc_ref, vc_ref, kp_ref, vp_ref, g_ref, wo_ref, o_ref):
    i = pl.program_id(1)
    q = q_ref[...]
    k_cat = jnp.concatenate([kp_ref[...], kc_ref[...]], axis=0)
    v_cat = jnp.concatenate([vp_ref[...], vc_ref[...]], axis=0)
    row = lax.broadcasted_iota(I32, (WINDOW, 2 * WINDOW), 0)
    col = lax.broadcasted_iota(I32, (WINDOW, 2 * WINDOW), 1)
    mask = (col > row) & (col <= row + WINDOW) & ((col >= WINDOW) | (i > 0))
    inv_sqrt = 1.0 / math.sqrt(HEAD_DIM)

    outs = []
    for h in range(N_Q_HEADS):
        j = h // Q_PER_KV
        qh = q[:, h * HEAD_DIM:(h + 1) * HEAD_DIM]
        kj = k_cat[:, j * HEAD_DIM:(j + 1) * HEAD_DIM]
        vj = v_cat[:, j * HEAD_DIM:(j + 1) * HEAD_DIM]
        s = lax.dot_general(qh, kj, (((1,), (1,)), ((), ())), preferred_element_type=F32) * inv_sqrt
        s = jnp.where(mask, s, NEG_BIG)
        sink = sink_ref[h]
        m = jnp.maximum(jnp.max(s, axis=-1, keepdims=True), sink)
        p = jnp.exp(s - m)
        denom = jnp.sum(p, axis=-1, keepdims=True) + jnp.exp(sink - m)
        o = jnp.dot(p.astype(BF16), vj, preferred_element_type=F32) / denom
        outs.append(o)
    attn = jnp.concatenate(outs, axis=-1).astype(BF16)
    proj = jnp.dot(attn, wo_ref[...], preferred_element_type=F32)
    o_ref[...] = (_sigmoid(g_ref[...].astype(F32)) * proj).astype(o_ref.dtype)


def _attention(proj, sinks, w_o, batch, seq, d, gcol):
    t = proj.shape[0]
    nb = seq // WINDOW
    kcol = ATTN_WIDTH // KV_WIDTH
    vcol = kcol + 1
    cur = lambda b, i: b * nb + i
    prev = lambda b, i: b * nb + jnp.maximum(i - 1, 0)
    return pl.pallas_call(
        _attn_kernel,
        grid=(batch, nb),
        in_specs=[
            pl.BlockSpec(memory_space=pltpu.SMEM),
            pl.BlockSpec((WINDOW, ATTN_WIDTH), lambda b, i: (cur(b, i), 0)),
            pl.BlockSpec((WINDOW, KV_WIDTH), lambda b, i: (cur(b, i), kcol)),
            pl.BlockSpec((WINDOW, KV_WIDTH), lambda b, i: (cur(b, i), vcol)),
            pl.BlockSpec((WINDOW, KV_WIDTH), lambda b, i: (prev(b, i), kcol)),
            pl.BlockSpec((WINDOW, KV_WIDTH), lambda b, i: (prev(b, i), vcol)),
            pl.BlockSpec((WINDOW, d), lambda b, i: (cur(b, i), gcol)),
            _resident(w_o.shape),
        ],
        out_specs=pl.BlockSpec((WINDOW, d), lambda b, i: (cur(b, i), 0)),
        out_shape=jax.ShapeDtypeStruct((t, d), BF16),
        compiler_params=_params(("arbitrary", "arbitrary")),
    )(sinks, proj, proj, proj, proj, proj, proj, w_o.astype(BF16))


def _ssm_kernel(u_ref, ar_ref, ai_ref, br_ref, bi_ref, cr_ref, cin_ref, d_ref, y_ref,
                uperm, bur, bui, yperm):
    seq = u_ref.shape[0]
    nseg = SSM_SEGMENTS
    n = seq // nseg
    ns = bur.shape[1]

    for r in range(nseg):
        uperm[pl.ds(r, n, stride=nseg), :] = u_ref[r * n:(r + 1) * n, :].astype(F32)

    chunk = 512

    def bu_body(c, _):
        r0 = pl.multiple_of(c * chunk, chunk)
        up = uperm[pl.ds(r0, chunk), :]
        bur[pl.ds(r0, chunk), :] = jnp.dot(up, br_ref[0], precision=HIGHEST, preferred_element_type=F32)
        bui[pl.ds(r0, chunk), :] = jnp.dot(up, bi_ref[0], precision=HIGHEST, preferred_element_type=F32)
        return 0

    lax.fori_loop(0, seq // chunk, bu_body, 0)

    a_r1 = ar_ref[0]
    a_i1 = ai_ref[0]
    a_r = jnp.broadcast_to(a_r1, (nseg, ns))
    a_i = jnp.broadcast_to(a_i1, (nseg, ns))

    def scan_body(s, carry):
        xr, xi = carry
        r0 = pl.multiple_of(s * nseg, nseg)
        nxr = a_r * xr - a_i * xi + bur[pl.ds(r0, nseg), :]
        nxi = a_r * xi + a_i * xr + bui[pl.ds(r0, nseg), :]
        bur[pl.ds(r0, nseg), :] = nxr
        bui[pl.ds(r0, nseg), :] = nxi
        return nxr, nxi

    zeros = jnp.zeros((nseg, ns), F32)
    er, ei = lax.fori_loop(0, n, scan_body, (zeros, zeros), unroll=4)

    pr, pi = a_r1, a_i1
    for _ in range(n.bit_length() - 1):
        pr, pi = pr * pr - pi * pi, 2.0 * pr * pi

    cr_rows = [jnp.zeros((1, ns), F32)]
    ci_rows = [jnp.zeros((1, ns), F32)]
    for r in range(1, nseg):
        pcr, pci = cr_rows[-1], ci_rows[-1]
        cr_rows.append(er[r - 1:r, :] + pr * pcr - pi * pci)
        ci_rows.append(ei[r - 1:r, :] + pr * pci + pi * pcr)
    c_r = jnp.concatenate(cr_rows, axis=0)
    c_i = jnp.concatenate(ci_rows, axis=0)

    def fix_body(s, carry):
        qr, qi = carry
        r0 = pl.multiple_of(s * nseg, nseg)
        bur[pl.ds(r0, nseg), :] = bur[pl.ds(r0, nseg), :] + (qr * c_r - qi * c_i)
        bui[pl.ds(r0, nseg), :] = bui[pl.ds(r0, nseg), :] + (qr * c_i + qi * c_r)
        return qr * a_r - qi * a_i, qr * a_i + qi * a_r

    lax.fori_loop(0, n, fix_body, (a_r, a_i), unroll=4)

    def out_body(c, _):
        r0 = pl.multiple_of(c * chunk, chunk)
        y = jnp.dot(bur[pl.ds(r0, chunk), :], cr_ref[0], precision=HIGHEST, preferred_element_type=F32)
        y = y + jnp.dot(bui[pl.ds(r0, chunk), :], cin_ref[0], precision=HIGHEST, preferred_element_type=F32)
        y = y + d_ref[0] * uperm[pl.ds(r0, chunk), :]
        yperm[pl.ds(r0, chunk), :] = jax.nn.gelu(y, approximate=True)
        return 0

    lax.fori_loop(0, seq // chunk, out_body, 0)

    for r in range(nseg):
        y_ref[r * n:(r + 1) * n, :] = yperm[pl.ds(r, n, stride=nseg), :].astype(y_ref.dtype)


def _ssm_params(a_re, a_im, log_dt, b_re, b_im, c_re, c_im, d_skip):
    g, p = a_re.shape
    c = b_re.shape[-1]
    gpb = SSM_GROUPS_PER_BLOCK
    nblk = g // gpb
    dt = jnp.exp(log_dt)[:, None]
    mag = jnp.exp(dt * a_re)
    ab_r, ab_i = mag * jnp.cos(dt * a_im), mag * jnp.sin(dt * a_im)
    den = a_re * a_re + a_im * a_im
    nr, ni = ab_r - 1.0, ab_i
    f_r = (nr * a_re + ni * a_im) / den
    f_i = (ni * a_re - nr * a_im) / den
    bb_r = f_r[..., None] * b_re - f_i[..., None] * b_im
    bb_i = f_r[..., None] * b_im + f_i[..., None] * b_re
    eye = jnp.eye(gpb, dtype=F32)

    def in_mat(bb):
        return jnp.einsum("bgpc,gh->bgchp", bb.reshape(nblk, gpb, p, c), eye).reshape(nblk, gpb * c, gpb * p)

    def out_mat(cc):
        return jnp.einsum("bgcp,gh->bgphc", cc.reshape(nblk, gpb, c, p), eye).reshape(nblk, gpb * p, gpb * c)

    return (ab_r.reshape(nblk, 1, gpb * p), ab_i.reshape(nblk, 1, gpb * p),
            in_mat(bb_r), in_mat(bb_i), out_mat(c_re), out_mat(-c_im),
            d_skip.reshape(nblk, 1, gpb * c))


def _ssm(proj, ssm_mats, batch, seq, ucol0):
    ar, ai, br, bi, cr, cin, dsk = ssm_mats
    nblk = ar.shape[0]
    nu, ns = br.shape[1], br.shape[2]
    ublk0 = ucol0 // nu
    blk3 = lambda shape: pl.BlockSpec((1,) + shape, lambda b, g: (g, 0, 0))
    return pl.pallas_call(
        _ssm_kernel,
        grid=(batch, nblk),
        in_specs=[
            pl.BlockSpec((seq, nu), lambda b, g: (b, ublk0 + g)),
            blk3((1, ns)), blk3((1, ns)), blk3((nu, ns)), blk3((nu, ns)),
            blk3((ns, nu)), blk3((ns, nu)), blk3((1, nu)),
        ],
        out_specs=pl.BlockSpec((seq, nu), lambda b, g: (b, g)),
        out_shape=jax.ShapeDtypeStruct((batch * seq, nblk * nu), BF16),
        scratch_shapes=[
            pltpu.VMEM((seq, nu), F32), pltpu.VMEM((seq, ns), F32),
            pltpu.VMEM((seq, ns), F32), pltpu.VMEM((seq, nu), F32),
        ],
        compiler_params=_params(("arbitrary", "arbitrary")),
    )(proj, ar, ai, br, bi, cr, cin, dsk)


def _mixout_kernel(a_ref, y_ref, gs_ref, x_ref, gt_ref, wglu_ref, wmix_ref, g2_ref, sh_ref, sc_ref,
                   wr_ref, x1_ref, h2_ref, lg_ref):
    d = x_ref.shape[1]
    z = jnp.dot(y_ref[...], wglu_ref[...], preferred_element_type=F32)
    ssm = z[:, :d] * _sigmoid(z[:, d:])
    mixed = a_ref[...].astype(F32) + _sigmoid(gs_ref[...].astype(F32)) * ssm
    r = jnp.dot(mixed.astype(BF16), wmix_ref[...], preferred_element_type=F32)
    x1 = x_ref[...] + gt_ref[0] * r
    x1_ref[...] = x1
    h2 = _norm_mod(x1, g2_ref[...], sh_ref[0], sc_ref[0])
    h2_ref[...] = h2
    lg_ref[...] = lax.dot_general(wr_ref[...], h2, (((1,), (1,)), ((), ())),
                                  precision=HIGHEST, preferred_element_type=F32)


def _mixout(a_gated, y, proj, x2, gt1, w_glu, w_mix, g2, sh2, sc2, w_router_t, seq, gscol):
    t, d = x2.shape
    tm = 256
    per_b = seq // tm
    ne = w_router_t.shape[0]
    row = lambda i: (i, 0)
    bvec = pl.BlockSpec((1, 1, d), lambda i: (i // per_b, 0, 0))
    return pl.pallas_call(
        _mixout_kernel,
        grid=(t // tm,),
        in_specs=[
            pl.BlockSpec((tm, d), row),
            pl.BlockSpec((tm, y.shape[1]), row),
            pl.BlockSpec((tm, d), lambda i: (i, gscol)),
            pl.BlockSpec((tm, d), row),
            bvec,
            _resident(w_glu.shape),
            _resident(w_mix.shape),
            pl.BlockSpec((1, d), lambda i: (0, 0)),
            bvec, bvec,
            _resident(w_router_t.shape),
        ],
        out_specs=[
            pl.BlockSpec((tm, d), row),
            pl.BlockSpec((tm, d), row),
            pl.BlockSpec((ne, tm), lambda i: (0, i)),
        ],
        out_shape=[
            jax.ShapeDtypeStruct((t, d), F32),
            jax.ShapeDtypeStruct((t, d), F32),
            jax.ShapeDtypeStruct((ne, t), F32),
        ],
        compiler_params=_params(("arbitrary",)),
    )(a_gated, y, proj, x2, gt1, w_glu.astype(BF16), w_mix.astype(BF16), g2.reshape(1, d), sh2, sc2,
      w_router_t)


def _router_kernel(lg_ref, b_ref, idx_ref, w_ref, rank_ref, cnt_ref, run_scr):
    ng, gs, tn = lg_ref.shape
    ne = ng * gs

    @pl.when(pl.program_id(0) == 0)
    def _():
        run_scr[...] = jnp.zeros(run_scr.shape, run_scr.dtype)

    scores = _sigmoid(lg_ref[...])
    sel = scores + b_ref[...]
    mem = lax.broadcasted_iota(I32, (ng, gs, tn), 1)
    grp = lax.broadcasted_iota(I32, (ng, gs, tn), 0)
    eid = grp * gs + mem
    ninf = -jnp.inf

    m1 = jnp.max(sel, axis=1, keepdims=True)
    first = jnp.min(jnp.where(sel == m1, mem, gs), axis=1, keepdims=True)
    m2 = jnp.max(jnp.where(mem == first, ninf, sel), axis=1, keepdims=True)
    gscore = m1 + m2

    gid = lax.broadcasted_iota(I32, (ng, 1, tn), 0)
    gmask = jnp.zeros((ng, 1, tn), jnp.bool_)
    for _ in range(TOPK_GROUPS):
        m = jnp.max(gscore, axis=0, keepdims=True)
        pick = gid == jnp.min(jnp.where(gscore == m, gid, ng), axis=0, keepdims=True)
        gmask = gmask | pick
        gscore = jnp.where(pick, ninf, gscore)

    def reduce_experts(v):
        return jnp.sum(jnp.sum(v, axis=1, keepdims=True), axis=0, keepdims=True)[0]

    cand = jnp.where(gmask, sel, ninf)
    idxs, wts, picks = [], [], []
    for _ in range(TOP_K):
        m = jnp.max(jnp.max(cand, axis=1, keepdims=True), axis=0, keepdims=True)
        e = jnp.where(cand == m, eid, ne)
        e = jnp.min(jnp.min(e, axis=1, keepdims=True), axis=0, keepdims=True)
        pick = eid == e
        idxs.append(e[0])
        wts.append(reduce_experts(jnp.where(pick, scores, 0.0)))
        picks.append(pick)
        cand = jnp.where(pick, ninf, cand)
    idx_ref[...] = jnp.concatenate(idxs, axis=0)
    w = jnp.concatenate(wts, axis=0)
    w_ref[...] = w / jnp.sum(w, axis=0, keepdims=True) * ROUTED_SCALE

    chosen = picks[0]
    for pick in picks[1:]:
        chosen = chosen | pick
    onehot = chosen.astype(F32).reshape(ne, tn)
    before = (lax.broadcasted_iota(I32, (tn, tn), 0) < lax.broadcasted_iota(I32, (tn, tn), 1)).astype(BF16)
    prefix = jnp.dot(onehot.astype(BF16), before, preferred_element_type=F32)
    rank = (run_scr[...] + prefix).reshape(ng, gs, tn)
    ranks = [reduce_experts(jnp.where(pick, rank, 0.0)) for pick in picks]
    rank_ref[...] = jnp.concatenate(ranks, axis=0).astype(I32)
    run_scr[...] = run_scr[...] + jnp.sum(onehot, axis=1, keepdims=True)
    cnt_ref[...] = run_scr[...].astype(I32)


def _router(logits_t, b_router):
    ne, t = logits_t.shape
    ng = N_EXPERT_GROUPS
    gs = ne // ng
    tn = 512
    tok = pl.BlockSpec((TOP_K, tn), lambda i: (0, i))
    return pl.pallas_call(
        _router_kernel,
        grid=(t // tn,),
        in_specs=[
            pl.BlockSpec((ng, gs, tn), lambda i: (0, 0, i)),
            pl.BlockSpec((ng, gs, 1), lambda i: (0, 0, 0)),
        ],
        out_specs=[tok, tok, tok, pl.BlockSpec((ne, 1), lambda i: (0, 0))],
        out_shape=[jax.ShapeDtypeStruct((TOP_K, t), I32), jax.ShapeDtypeStruct((TOP_K, t), F32),
                   jax.ShapeDtypeStruct((TOP_K, t), I32), jax.ShapeDtypeStruct((ne, 1), I32)],
        scratch_shapes=[pltpu.VMEM((ne, 1), F32)],
        compiler_params=_params(("arbitrary",)),
    )(logits_t.reshape(ng, gs, t), b_router.reshape(ng, gs, 1))


def _dest_kernel(pstart_ref, idx_ref, rank_ref, dest_ref):
    idx = idx_ref[...]
    base = jnp.zeros(idx.shape, I32)
    for e in range(pstart_ref.shape[0]):
        base = jnp.where(idx == e, pstart_ref[e], base)
    dest_ref[...] = base + rank_ref[...]


def _dest(pstart, eidx, rank):
    k, t = eidx.shape
    tn = 2048
    tok = pl.BlockSpec((k, tn), lambda i, ps: (0, i))
    return pl.pallas_call(
        _dest_kernel,
        grid_spec=pltpu.PrefetchScalarGridSpec(
            num_scalar_prefetch=1, grid=(t // tn,), in_specs=[tok, tok], out_specs=tok),
        out_shape=jax.ShapeDtypeStruct((k, t), I32),
        compiler_params=_params(("arbitrary",)),
    )(pstart, eidx, rank)


def _slot_plan(counts):
    padded = (counts + MOE_SUB - 1) // MOE_SUB * MOE_SUB
    pend = jnp.cumsum(padded)
    pstart = pend - padded
    npad = (padded - counts)[:, None]
    ntail = MOE_SUB - npad
    tail0 = pend[-1] + jnp.cumsum(ntail, axis=0) - ntail
    j = jnp.arange(MOE_SUB, dtype=I32)[None, :]
    free_rows = jnp.where(j < npad, (pstart + counts)[:, None] + j, tail0 + j - npad)
    return jnp.stack([pstart, padded]).astype(I32), pstart.astype(I32), free_rows.astype(I32)


def _dispatch_kernel(dest_ref, pad_ref, h_ref, xs_hbm, zero_scr, sem):
    tm = h_ref.shape[0]
    k = dest_ref.shape[2] // tm
    npad = pad_ref.shape[2]
    assert npad % tm == 0

    zero_scr[...] = jnp.zeros(zero_scr.shape, zero_scr.dtype)

    def issue(r, _):
        for kk in range(k):
            pltpu.make_async_copy(h_ref.at[pl.ds(r, 1)], xs_hbm.at[pl.ds(dest_ref[0, 0, r * k + kk], 1)],
                                  sem).start(priority=kk % 2)
        return 0

    lax.fori_loop(0, tm, issue, 0, unroll=2)

    def issue_pad(j, _):
        pltpu.make_async_copy(zero_scr.at[pl.ds(0, 1)], xs_hbm.at[pl.ds(pad_ref[0, 0, j], 1)], sem).start()
        return 0

    lax.fori_loop(0, npad, issue_pad, 0, unroll=8)
    for _ in range(k + npad // tm):
        pltpu.make_async_copy(h_ref, xs_hbm.at[pl.ds(0, tm)], sem).wait()


def _dispatch(dest, pad_dest, h2, n_rows):
    t, d = h2.shape
    k = dest.shape[0]
    tm = 256
    nblk = t // tm
    npad = pad_dest.size // nblk
    dest_tm = dest.T.reshape(nblk, 1, tm * k)
    return pl.pallas_call(
        _dispatch_kernel,
        grid=(nblk,),
        in_specs=[
            pl.BlockSpec((1, 1, tm * k), lambda i: (i, 0, 0), memory_space=pltpu.SMEM),
            pl.BlockSpec((1, 1, npad), lambda i: (i, 0, 0), memory_space=pltpu.SMEM),
            pl.BlockSpec((tm, d), lambda i: (i, 0)),
        ],
        out_specs=pl.BlockSpec(memory_space=pl.ANY),
        out_shape=jax.ShapeDtypeStruct((n_rows, d), F32),
        scratch_shapes=[pltpu.VMEM((SUBLANES, d), F32), pltpu.SemaphoreType.DMA],
        compiler_params=_params(("arbitrary",)),
    )(dest_tm, pad_dest.reshape(nblk, 1, npad), h2)


def _expert_kernel(meta_ref, x_hbm, wg_ref, wu_ref, wd_ref, y_hbm, wg_s, wu_s, wd_s, xbuf, ybuf,
                   xsem, ysem, ypend):
    e = pl.program_id(0)
    ne = pl.num_programs(0)
    chunk = xbuf.shape[1]
    nsub = chunk // MOE_SUB
    start = meta_ref[0, e]
    nrows = meta_ref[1, e]

    def sub(r):
        return pl.ds(pl.multiple_of(r, MOE_SUB), MOE_SUB)

    def x_piece(r0, slot, j):
        return pltpu.make_async_copy(x_hbm.at[sub(r0 + j * MOE_SUB)], xbuf.at[slot, sub(j * MOE_SUB)],
                                     xsem.at[slot])

    def y_piece(r0, slot, j):
        return pltpu.make_async_copy(ybuf.at[slot, sub(j * MOE_SUB)], y_hbm.at[sub(r0 + j * MOE_SUB)],
                                     ysem.at[slot])

    def pieces(rows_left):
        return jnp.clip(rows_left, 0, chunk) // MOE_SUB

    def x_start(r0, slot, n):
        def body(j, carry):
            x_piece(r0, slot, j).start()
            return carry

        lax.fori_loop(0, n, body, 0)

    def x_wait(slot, n):
        def body(j, carry):
            x_piece(0, slot, 0).wait()
            return carry

        lax.fori_loop(0, n, body, 0)

    def y_drain(slot):
        def body(j, carry):
            y_piece(0, slot, 0).wait()
            return carry

        lax.fori_loop(0, ypend[slot], body, 0)
        ypend[slot] = 0

    @pl.when(e == 0)
    def _():
        ypend[0] = 0
        ypend[1] = 0
        x_start(start, 0, pieces(nrows))

    wg_s[...] = wg_ref[0, 0].astype(BF16)
    wu_s[...] = wu_ref[0, 0].astype(BF16)
    wd_s[...] = wd_ref[0, 0].astype(BF16)

    def compute(slot, m):
        x = xbuf[slot, pl.ds(0, m), :].astype(BF16)
        gate = jnp.dot(x, wg_s[...], preferred_element_type=F32)
        up = jnp.dot(x, wu_s[...], preferred_element_type=F32)
        hid = (_silu(gate) * up).astype(BF16)
        ybuf[slot, pl.ds(0, m), :] = jnp.dot(hid, wd_s[...], preferred_element_type=F32)

    def run_chunk(c, m_static, n_here):
        slot = c & 1
        r0 = start + c * chunk
        x_wait(slot, n_here)
        x_start(r0 + chunk, 1 - slot, pieces(nrows - (c + 1) * chunk))
        y_drain(slot)
        compute(slot, m_static)
        for j in range(m_static // MOE_SUB):
            y_piece(r0, slot, j).start()
        ypend[slot] = m_static // MOE_SUB

    n_full = nrows // chunk

    def full_body(c, carry):
        run_chunk(c, chunk, nsub)
        return carry

    lax.fori_loop(0, n_full, full_body, 0)

    rem = nrows - n_full * chunk
    for m in range(MOE_SUB, chunk, MOE_SUB):
        @pl.when(rem == m)
        def _(m=m):
            run_chunk(n_full, m, m // MOE_SUB)

    @pl.when(e + 1 < ne)
    def _():
        x_start(meta_ref[0, e + 1], 0, pieces(meta_ref[1, e + 1]))

    @pl.when(e + 1 == ne)
    def _():
        y_drain(0)
        y_drain(1)
        tail0 = start + nrows
        ntail = (y_hbm.shape[0] - tail0) // MOE_SUB
        ybuf[0, pl.ds(0, MOE_SUB), :] = jnp.zeros((MOE_SUB, ybuf.shape[2]), ybuf.dtype)

        def fill(j, carry):
            y_piece(tail0 + j * MOE_SUB, 0, 0).start()
            return carry

        lax.fori_loop(0, ntail, fill, 0)
        ypend[0] = ntail
        y_drain(0)


def _experts(meta, x_sorted, w_gate, w_up, w_down, layer, n_slots):
    d = x_sorted.shape[1]
    _, ne, _, f = w_gate.shape
    chunk = 512
    wspec = lambda a, b: pl.BlockSpec((1, 1, a, b), lambda e, m: (layer, e, 0, 0))
    grid_spec = pltpu.PrefetchScalarGridSpec(
        num_scalar_prefetch=1,
        grid=(ne,),
        in_specs=[pl.BlockSpec(memory_space=pl.ANY), wspec(d, f), wspec(d, f), wspec(f, d)],
        out_specs=pl.BlockSpec(memory_space=pl.ANY),
        scratch_shapes=[
            pltpu.VMEM((d, f), BF16), pltpu.VMEM((d, f), BF16), pltpu.VMEM((f, d), BF16),
            pltpu.VMEM((2, chunk, d), F32), pltpu.VMEM((2, chunk, d), F32),
            pltpu.SemaphoreType.DMA((2,)), pltpu.SemaphoreType.DMA((2,)),
            pltpu.SMEM((2,), I32),
        ],
    )
    return pl.pallas_call(
        _expert_kernel,
        grid_spec=grid_spec,
        out_shape=jax.ShapeDtypeStruct((n_slots, d), F32),
        compiler_params=_params(("arbitrary",)),
    )(meta, x_sorted, w_gate, w_up, w_down)


def _combine_kernel(dest_ref, h_ref, wt_ref, x_ref, gt_ref, wg_ref, wu_ref, wd_ref, gf_ref, y_hbm,
                    o_ref, buf, sem, *, final_norm):
    k, tm, _ = buf.shape

    def issue(r, _):
        for kk in range(k):
            pltpu.make_async_copy(y_hbm.at[pl.ds(dest_ref[0, 0, r * k + kk], 1)],
                                  buf.at[kk, pl.ds(r, 1)], sem).start(priority=kk % 2)
        return 0

    lax.fori_loop(0, tm, issue, 0, unroll=2)

    h = h_ref[...].astype(BF16)
    gate = jnp.dot(h, wg_ref[...], preferred_element_type=F32)
    up = jnp.dot(h, wu_ref[...], preferred_element_type=F32)
    acc = jnp.dot((_silu(gate) * up).astype(BF16), wd_ref[...], preferred_element_type=F32)

    for kk in range(k):
        pltpu.make_async_copy(y_hbm.at[pl.ds(0, tm)], buf.at[kk], sem).wait()
    wt = wt_ref[...]
    for kk in range(k):
        acc = acc + wt[:, kk:kk + 1] * buf[kk]
    out = x_ref[...] + gt_ref[0] * acc
    if final_norm:
        ms = jnp.mean(out * out, axis=-1, keepdims=True)
        out = out * lax.rsqrt(ms + NORM_EPS) * gf_ref[...]
    o_ref[...] = out


def _combine(dest, wts, h2, x1, gt2, w_sg, w_su, w_sd, g_final, y_sorted, seq, final_norm):
    t, d = x1.shape
    k = dest.shape[0]
    tm = 128
    per_b = seq // tm
    nblk = t // tm
    dest_tm = dest.T.reshape(nblk, 1, tm * k)
    row = lambda i: (i, 0)
    return pl.pallas_call(
        functools.partial(_combine_kernel, final_norm=final_norm),
        grid=(nblk,),
        in_specs=[
            pl.BlockSpec((1, 1, tm * k), lambda i: (i, 0, 0), memory_space=pltpu.SMEM),
            pl.BlockSpec((tm, d), row),
            pl.BlockSpec((tm, k), row),
            pl.BlockSpec((tm, d), row),
            pl.BlockSpec((1, 1, d), lambda i: (i // per_b, 0, 0)),
            _resident(w_sg.shape), _resident(w_su.shape), _resident(w_sd.shape),
            pl.BlockSpec((1, d), lambda i: (0, 0)),
            pl.BlockSpec(memory_space=pl.ANY),
        ],
        out_specs=pl.BlockSpec((tm, d), row),
        out_shape=jax.ShapeDtypeStruct((t, d), F32),
        scratch_shapes=[pltpu.VMEM((k, tm, d), F32), pltpu.SemaphoreType.DMA],
        compiler_params=_params(("arbitrary",)),
    )(dest_tm, h2, wts.T, x1, gt2, w_sg.astype(BF16), w_su.astype(BF16), w_sd.astype(BF16),
      g_final.reshape(1, d), y_sorted)


def kernel(x, c, w_ada, b_ada, norm_mix, w_in, b_in, attn_sinks, w_attn_o, ssm_a_re, ssm_a_im, ssm_log_dt,
           ssm_b_re, ssm_b_im, ssm_c_re, ssm_c_im, ssm_d, w_glu, w_mix_out, norm_ffn, w_router, b_router,
           w_exp_gate, w_exp_up, w_exp_down, w_sh_gate, w_sh_up, w_sh_down, norm_final):
    batch, seq, d = x.shape
    depth = w_ada.shape[0]
    n_experts = w_router.shape[2]
    ssm_width = ssm_d.shape[1]
    ucol0 = ATTN_WIDTH + 2 * KV_WIDTH
    gacol = (ucol0 + ssm_width) // d
    gscol = gacol + 1
    n_slots = batch * seq * TOP_K + n_experts * MOE_SUB

    c_pad = jnp.zeros((SUBLANES, d), F32).at[:batch].set(c)
    mod = _ada_mod(c_pad, w_ada, b_ada)[:, :batch]
    mod = mod.reshape(depth, batch, 6, 1, d)

    x2 = x.reshape(batch * seq, d)
    for l in range(depth):
        sh1, sc1, gt1, sh2, sc2, gt2 = (mod[l, :, m] for m in range(6))
        proj = _inproj(x2, norm_mix[l], sh1, sc1, w_in, b_in, l, seq)
        a_gated = _attention(proj, attn_sinks[l], w_attn_o[l], batch, seq, d, gacol)
        ssm_mats = _ssm_params(ssm_a_re[l], ssm_a_im[l], ssm_log_dt[l], ssm_b_re[l], ssm_b_im[l],
                               ssm_c_re[l], ssm_c_im[l], ssm_d[l])
        y = _ssm(proj, ssm_mats, batch, seq, ucol0)
        x1, h2, logits_t = _mixout(a_gated, y, proj, x2, gt1, w_glu[l], w_mix_out[l], norm_ffn[l],
                                   sh2, sc2, w_router[l].T, seq, gscol)
        eidx, wts, rank, counts = _router(logits_t, b_router[l])
        meta, pstart, free_rows = _slot_plan(counts[:, 0])
        dest = _dest(pstart, eidx, rank)
        x_sorted = _dispatch(dest, free_rows, h2, n_slots)
        y_sorted = _experts(meta, x_sorted, w_exp_gate, w_exp_up, w_exp_down, l, n_slots)
        x2 = _combine(dest, wts, h2, x1, gt2, w_sh_gate[l], w_sh_up[l], w_sh_down[l], norm_final,
                      y_sorted, seq, final_norm=(l == depth - 1))
    return x2.reshape(batch, seq, d)
```

```python
import functools
import math

import jax
import jax.numpy as jnp
from jax import lax
from jax.experimental import pallas as pl
from jax.experimental.pallas import tpu as pltpu

F32 = jnp.float32
BF16 = jnp.bfloat16
I32 = jnp.int32
HIGHEST = lax.Precision.HIGHEST

N_Q_HEADS = 16
N_KV_HEADS = 4
HEAD_DIM = 64
Q_PER_KV = N_Q_HEADS // N_KV_HEADS
WINDOW = 128
ATTN_WIDTH = N_Q_HEADS * HEAD_DIM
KV_WIDTH = N_KV_HEADS * HEAD_DIM
SSM_GROUP_CH = 16
SSM_STATE = 64
N_EXPERT_GROUPS = 8
TOPK_GROUPS = 4
TOP_K = 8
ROUTED_SCALE = 2.5
NORM_EPS = 1e-6

LANES = 128
SUBLANES = 8
VMEM_LIMIT_BYTES = 56 * 1024 * 1024

SSM_GROUPS_PER_BLOCK = LANES // SSM_GROUP_CH
SSM_SEGMENTS = SUBLANES
MOE_SUB = 128
NEG_BIG = -1e30


def _sigmoid(x):
    return 1.0 / (1.0 + jnp.exp(-x))


def _silu(x):
    return x * _sigmoid(x)


def _params(sem):
    return pltpu.CompilerParams(dimension_semantics=sem, vmem_limit_bytes=VMEM_LIMIT_BYTES)


def _resident(shape):
    nd = len(shape)
    return pl.BlockSpec(shape, lambda *_: (0,) * nd, pipeline_mode=pl.Buffered(1))


def _ada_kernel(c_ref, w_ref, b_ref, o_ref):
    c_act = _silu(c_ref[...]).astype(BF16)
    o_ref[0] = jnp.dot(c_act, w_ref[0].astype(BF16), preferred_element_type=F32) + b_ref[0]


def _ada_mod(c_pad, w_ada, b_ada):
    depth, d, n = w_ada.shape
    tn = 1024
    return pl.pallas_call(
        _ada_kernel,
        grid=(depth, n // tn),
        in_specs=[
            pl.BlockSpec((SUBLANES, d), lambda l, j: (0, 0)),
            pl.BlockSpec((1, d, tn), lambda l, j: (l, 0, j)),
            pl.BlockSpec((1, 1, tn), lambda l, j: (l, 0, j)),
        ],
        out_specs=pl.BlockSpec((1, SUBLANES, tn), lambda l, j: (l, 0, j)),
        out_shape=jax.ShapeDtypeStruct((depth, SUBLANES, n), F32),
        compiler_params=_params(("arbitrary", "arbitrary")),
    )(c_pad, w_ada, b_ada.reshape(depth, 1, n))


def _norm_mod(x, g, shift, scale):
    ms = jnp.mean(x * x, axis=-1, keepdims=True)
    xn = x * lax.rsqrt(ms + NORM_EPS) * g
    return xn * (1.0 + scale) + shift


def _inproj_kernel(x_ref, g_ref, sh_ref, sc_ref, w_ref, b_ref, o_ref, h_scr):
    @pl.when(pl.program_id(1) == 0)
    def _():
        h_scr[...] = _norm_mod(x_ref[...], g_ref[...], sh_ref[0], sc_ref[0]).astype(BF16)

    acc = jnp.dot(h_scr[...], w_ref[0].astype(BF16), preferred_element_type=F32)
    o_ref[...] = (acc + b_ref[0]).astype(o_ref.dtype)


def _inproj(x2, g, shift, scale, w_all, b_all, layer, seq):
    t, d = x2.shape
    depth, _, n = w_all.shape
    tm, tn = 1024, 512
    per_b = seq // tm
    return pl.pallas_call(
        _inproj_kernel,
        grid=(t // tm, n // tn),
        in_specs=[
            pl.BlockSpec((tm, d), lambda i, j: (i, 0)),
            pl.BlockSpec((1, d), lambda i, j: (0, 0)),
            pl.BlockSpec((1, 1, d), lambda i, j: (i // per_b, 0, 0)),
            pl.BlockSpec((1, 1, d), lambda i, j: (i // per_b, 0, 0)),
            pl.BlockSpec((1, d, tn), lambda i, j: (layer, 0, j)),
            pl.BlockSpec((1, 1, tn), lambda i, j: (layer, 0, j)),
        ],
        out_specs=pl.BlockSpec((tm, tn), lambda i, j: (i, j)),
        out_shape=jax.ShapeDtypeStruct((t, n), BF16),
        scratch_shapes=[pltpu.VMEM((tm, d), BF16)],
        compiler_params=_params(("arbitrary", "arbitrary")),
    )(x2, g.reshape(1, d), shift, scale, w_all, b_all.reshape(depth, 1, n))


def _attn_kernel(sink_ref, q_ref, kc_ref, vc_ref, kp_ref, vp_ref, g_ref, wo_ref, o_ref):
    i = pl.program_id(1)
    q = q_ref[...]
    k_cat = jnp.concatenate([kp_ref[...], kc_ref[...]], axis=0)
    v_cat = jnp.concatenate([vp_ref[...], vc_ref[...]], axis=0)
    row = lax.broadcasted_iota(I32, (WINDOW, 2 * WINDOW), 0)
    col = lax.broadcasted_iota(I32, (WINDOW, 2 * WINDOW), 1)
    mask = (col > row) & (col <= row + WINDOW) & ((col >= WINDOW) | (i > 0))
    inv_sqrt = 1.0 / math.sqrt(HEAD_DIM)

    outs = []
    for h in range(N_Q_HEADS):
        j = h // Q_PER_KV
        qh = q[:, h * HEAD_DIM:(h + 1) * HEAD_DIM]
        kj = k_cat[:, j * HEAD_DIM:(j + 1) * HEAD_DIM]
        vj = v_cat[:, j * HEAD_DIM:(j + 1) * HEAD_DIM]
        s = lax.dot_general(qh, kj, (((1,), (1,)), ((), ())), preferred_element_type=F32) * inv_sqrt
        s = jnp.where(mask, s, NEG_BIG)
        sink = sink_ref[h]
        m = jnp.maximum(jnp.max(s, axis=-1, keepdims=True), sink)
        p = jnp.exp(s - m)
        denom = jnp.sum(p, axis=-1, keepdims=True) + jnp.exp(sink - m)
        o = jnp.dot(p.astype(BF16), vj, preferred_element_type=F32) / denom
        outs.append(o)
    attn = jnp.concatenate(outs, axis=-1).astype(BF16)
    proj = jnp.dot(attn, wo_ref[...], preferred_element_type=F32)
    o_ref[...] = (_sigmoid(g_ref[...].astype(F32)) * proj).astype(o_ref.dtype)


def _attention(proj, sinks, w_o, batch, seq, d, gcol):
    t = proj.shape[0]
    nb = seq // WINDOW
    kcol = ATTN_WIDTH // KV_WIDTH
    vcol = kcol + 1
    cur = lambda b, i: b * nb + i
    prev = lambda b, i: b * nb + jnp.maximum(i - 1, 0)
    return pl.pallas_call(
        _attn_kernel,
        grid=(batch, nb),
        in_specs=[
            pl.BlockSpec(memory_space=pltpu.SMEM),
            pl.BlockSpec((WINDOW, ATTN_WIDTH), lambda b, i: (cur(b, i), 0)),
            pl.BlockSpec((WINDOW, KV_WIDTH), lambda b, i: (cur(b, i), kcol)),
            pl.BlockSpec((WINDOW, KV_WIDTH), lambda b, i: (cur(b, i), vcol)),
            pl.BlockSpec((WINDOW, KV_WIDTH), lambda b, i: (prev(b, i), kcol)),
            pl.BlockSpec((WINDOW, KV_WIDTH), lambda b, i: (prev(b, i), vcol)),
            pl.BlockSpec((WINDOW, d), lambda b, i: (cur(b, i), gcol)),
            _resident(w_o.shape),
        ],
        out_specs=pl.BlockSpec((WINDOW, d), lambda b, i: (cur(b, i), 0)),
        out_shape=jax.ShapeDtypeStruct((t, d), BF16),
        compiler_params=_params(("arbitrary", "arbitrary")),
    )(sinks, proj, proj, proj, proj, proj, proj, w_o.astype(BF16))


def _ssm_kernel(u_ref, ar_ref, ai_ref, br_ref, bi_ref, cr_ref, cin_ref, d_ref, y_ref,
                uperm, bur, bui, yperm):
    seq = u_ref.shape[0]
    nseg = SSM_SEGMENTS
    n = seq // nseg
    ns = bur.shape[1]

    for r in range(nseg):
        uperm[pl.ds(r, n, stride=nseg), :] = u_ref[r * n:(r + 1) * n, :].astype(F32)

    chunk = 512

    def bu_body(c, _):
        r0 = pl.multiple_of(c * chunk, chunk)
        up = uperm[pl.ds(r0, chunk), :].astype(BF16)
        bur[pl.ds(r0, chunk), :] = (jnp.dot(up, br_ref[0, 0], preferred_element_type=F32)
                                    + jnp.dot(up, br_ref[0, 1], preferred_element_type=F32))
        bui[pl.ds(r0, chunk), :] = (jnp.dot(up, bi_ref[0, 0], preferred_element_type=F32)
                                    + jnp.dot(up, bi_ref[0, 1], preferred_element_type=F32))
        return 0

    lax.fori_loop(0, seq // chunk, bu_body, 0)

    a_r1 = ar_ref[0]
    a_i1 = ai_ref[0]
    a_r = jnp.broadcast_to(a_r1, (nseg, ns))
    a_i = jnp.broadcast_to(a_i1, (nseg, ns))

    def scan_body(s, carry):
        xr, xi = carry
        r0 = pl.multiple_of(s * nseg, nseg)
        nxr = a_r * xr - a_i * xi + bur[pl.ds(r0, nseg), :]
        nxi = a_r * xi + a_i * xr + bui[pl.ds(r0, nseg), :]
        bur[pl.ds(r0, nseg), :] = nxr
        bui[pl.ds(r0, nseg), :] = nxi
        return nxr, nxi

    zeros = jnp.zeros((nseg, ns), F32)
    er, ei = lax.fori_loop(0, n, scan_body, (zeros, zeros), unroll=4)

    pr, pi = a_r1, a_i1
    for _ in range(n.bit_length() - 1):
        pr, pi = pr * pr - pi * pi, 2.0 * pr * pi

    cr_rows = [jnp.zeros((1, ns), F32)]
    ci_rows = [jnp.zeros((1, ns), F32)]
    for r in range(1, nseg):
        pcr, pci = cr_rows[-1], ci_rows[-1]
        cr_rows.append(er[r - 1:r, :] + pr * pcr - pi * pci)
        ci_rows.append(ei[r - 1:r, :] + pr * pci + pi * pcr)
    c_r = jnp.concatenate(cr_rows, axis=0)
    c_i = jnp.concatenate(ci_rows, axis=0)

    def fix_body(s, carry):
        qr, qi = carry
        r0 = pl.multiple_of(s * nseg, nseg)
        bur[pl.ds(r0, nseg), :] = bur[pl.ds(r0, nseg), :] + (qr * c_r - qi * c_i)
        bui[pl.ds(r0, nseg), :] = bui[pl.ds(r0, nseg), :] + (qr * c_i + qi * c_r)
        return qr * a_r - qi * a_i, qr * a_i + qi * a_r

    lax.fori_loop(0, n, fix_body, (a_r, a_i), unroll=4)

    def out_body(c, _):
        r0 = pl.multiple_of(c * chunk, chunk)
        y = jnp.dot(bur[pl.ds(r0, chunk), :].astype(BF16), cr_ref[0], preferred_element_type=F32)
        y = y + jnp.dot(bui[pl.ds(r0, chunk), :].astype(BF16), cin_ref[0], preferred_element_type=F32)
        y = y + d_ref[0] * uperm[pl.ds(r0, chunk), :]
        yperm[pl.ds(r0, chunk), :] = jax.nn.gelu(y, approximate=True)
        return 0

    lax.fori_loop(0, seq // chunk, out_body, 0)

    for r in range(nseg):
        y_ref[r * n:(r + 1) * n, :] = yperm[pl.ds(r, n, stride=nseg), :].astype(y_ref.dtype)


def _ssm_params(a_re, a_im, log_dt, b_re, b_im, c_re, c_im, d_skip):
    g, p = a_re.shape
    c = b_re.shape[-1]
    gpb = SSM_GROUPS_PER_BLOCK
    nblk = g // gpb
    dt = jnp.exp(log_dt)[:, None]
    mag = jnp.exp(dt * a_re)
    ab_r, ab_i = mag * jnp.cos(dt * a_im), mag * jnp.sin(dt * a_im)
    den = a_re * a_re + a_im * a_im
    nr, ni = ab_r - 1.0, ab_i
    f_r = (nr * a_re + ni * a_im) / den
    f_i = (ni * a_re - nr * a_im) / den
    bb_r = f_r[..., None] * b_re - f_i[..., None] * b_im
    bb_i = f_r[..., None] * b_im + f_i[..., None] * b_re
    eye = jnp.eye(gpb, dtype=F32)

    def in_mat(bb):
        return jnp.einsum("bgpc,gh->bgchp", bb.reshape(nblk, gpb, p, c), eye).reshape(nblk, gpb * c, gpb * p)

    def out_mat(cc):
        return jnp.einsum("bgcp,gh->bgphc", cc.reshape(nblk, gpb, c, p), eye).reshape(nblk, gpb * p, gpb * c)

    def hi_lo(m):
        hi = m.astype(BF16)
        return jnp.stack([hi, (m - hi.astype(F32)).astype(BF16)], axis=1)

    return (ab_r.reshape(nblk, 1, gpb * p), ab_i.reshape(nblk, 1, gpb * p),
            hi_lo(in_mat(bb_r)), hi_lo(in_mat(bb_i)), out_mat(c_re).astype(BF16), out_mat(-c_im).astype(BF16),
            d_skip.reshape(nblk, 1, gpb * c))


def _ssm(proj, ssm_mats, batch, seq, ucol0):
    ar, ai, br, bi, cr, cin, dsk = ssm_mats
    nblk = ar.shape[0]
    nu, ns = br.shape[2], br.shape[3]
    ublk0 = ucol0 // nu
    blk3 = lambda shape: pl.BlockSpec((1,) + shape, lambda b, g: (g,) + (0,) * len(shape))
    return pl.pallas_call(
        _ssm_kernel,
        grid=(batch, nblk),
        in_specs=[
            pl.BlockSpec((seq, nu), lambda b, g: (b, ublk0 + g)),
            blk3((1, ns)), blk3((1, ns)), blk3((2, nu, ns)), blk3((2, nu, ns)),
            blk3((ns, nu)), blk3((ns, nu)), blk3((1, nu)),
        ],
        out_specs=pl.BlockSpec((seq, nu), lambda b, g: (b, g)),
        out_shape=jax.ShapeDtypeStruct((batch * seq, nblk * nu), BF16),
        scratch_shapes=[
            pltpu.VMEM((seq, nu), F32), pltpu.VMEM((seq, ns), F32),
            pltpu.VMEM((seq, ns), F32), pltpu.VMEM((seq, nu), F32),
        ],
        compiler_params=_params(("arbitrary", "arbitrary")),
    )(proj, ar, ai, br, bi, cr, cin, dsk)


def _mixout_kernel(a_ref, y_ref, gs_ref, x_ref, gt_ref, wglu_ref, wmix_ref, g2_ref, sh_ref, sc_ref,
                   wr_ref, x1_ref, lg_ref):
    d = x_ref.shape[1]
    y = y_ref[...]
    r = None
    nsplit = 2
    w = d // nsplit
    for c in range(nsplit):
        cols = slice(c * w, (c + 1) * w)
        za = jnp.dot(y, wglu_ref[:, c * w:(c + 1) * w], preferred_element_type=F32)
        zb = jnp.dot(y, wglu_ref[:, d + c * w:d + (c + 1) * w], preferred_element_type=F32)
        mixed = a_ref[:, cols].astype(F32) + _sigmoid(gs_ref[:, cols].astype(F32)) * (za * _sigmoid(zb))
        part = jnp.dot(mixed.astype(BF16), wmix_ref[c * w:(c + 1) * w, :], preferred_element_type=F32)
        r = part if r is None else r + part
    x1 = x_ref[...] + gt_ref[0] * r
    x1_ref[...] = x1
    h2 = _norm_mod(x1, g2_ref[...], sh_ref[0], sc_ref[0])
    lg_ref[...] = lax.dot_general(wr_ref[...], h2, (((1,), (1,)), ((), ())),
                                  precision=HIGHEST, preferred_element_type=F32)


def _mixout(a_gated, y, proj, x2, gt1, w_glu, w_mix, g2, sh2, sc2, w_router_t, seq, gscol):
    t, d = x2.shape
    tm = 512
    per_b = seq // tm
    ne = w_router_t.shape[0]
    row = lambda i: (i, 0)
    bvec = pl.BlockSpec((1, 1, d), lambda i: (i // per_b, 0, 0))
    return pl.pallas_call(
        _mixout_kernel,
        grid=(t // tm,),
        in_specs=[
            pl.BlockSpec((tm, d), row),
            pl.BlockSpec((tm, y.shape[1]), row),
            pl.BlockSpec((tm, d), lambda i: (i, gscol)),
            pl.BlockSpec((tm, d), row),
            bvec,
            _resident(w_glu.shape),
            _resident(w_mix.shape),
            pl.BlockSpec((1, d), lambda i: (0, 0)),
            bvec, bvec,
            _resident(w_router_t.shape),
        ],
        out_specs=[
            pl.BlockSpec((tm, d), row),
            pl.BlockSpec((ne, tm), lambda i: (0, i)),
        ],
        out_shape=[
            jax.ShapeDtypeStruct((t, d), F32),
            jax.ShapeDtypeStruct((ne, t), F32),
        ],
        compiler_params=_params(("arbitrary",)),
    )(a_gated, y, proj, x2, gt1, w_glu.astype(BF16), w_mix.astype(BF16), g2.reshape(1, d), sh2, sc2,
      w_router_t)


def _router_kernel(lg_ref, b_ref, idx_ref, w_ref, rank_ref, cnt_ref, run_scr):
    ng, gs, tn = lg_ref.shape
    ne = ng * gs

    @pl.when(pl.program_id(0) == 0)
    def _():
        run_scr[...] = jnp.zeros(run_scr.shape, run_scr.dtype)

    scores = _sigmoid(lg_ref[...])
    sel = scores + b_ref[...]
    mem = lax.broadcasted_iota(I32, (ng, gs, tn), 1)
    grp = lax.broadcasted_iota(I32, (ng, gs, tn), 0)
    eid = grp * gs + mem
    ninf = -jnp.inf

    m1 = jnp.max(sel, axis=1, keepdims=True)
    first = jnp.min(jnp.where(sel == m1, mem, gs), axis=1, keepdims=True)
    m2 = jnp.max(jnp.where(mem == first, ninf, sel), axis=1, keepdims=True)
    gscore = m1 + m2

    gid = lax.broadcasted_iota(I32, (ng, 1, tn), 0)
    gmask = jnp.zeros((ng, 1, tn), jnp.bool_)
    for _ in range(TOPK_GROUPS):
        m = jnp.max(gscore, axis=0, keepdims=True)
        pick = gid == jnp.min(jnp.where(gscore == m, gid, ng), axis=0, keepdims=True)
        gmask = gmask | pick
        gscore = jnp.where(pick, ninf, gscore)

    def reduce_experts(v):
        return jnp.sum(jnp.sum(v, axis=1, keepdims=True), axis=0, keepdims=True)[0]

    cand = jnp.where(gmask, sel, ninf)
    idxs, wts, picks = [], [], []
    for _ in range(TOP_K):
        m = jnp.max(jnp.max(cand, axis=1, keepdims=True), axis=0, keepdims=True)
        e = jnp.where(cand == m, eid, ne)
        e = jnp.min(jnp.min(e, axis=1, keepdims=True), axis=0, keepdims=True)
        pick = eid == e
        idxs.append(e[0])
        wts.append(reduce_experts(jnp.where(pick, scores, 0.0)))
        picks.append(pick)
        cand = jnp.where(pick, ninf, cand)
    idx_ref[...] = jnp.concatenate(idxs, axis=0)
    w = jnp.concatenate(wts, axis=0)
    w_ref[...] = w / jnp.sum(w, axis=0, keepdims=True) * ROUTED_SCALE

    chosen = picks[0]
    for pick in picks[1:]:
        chosen = chosen | pick
    onehot = chosen.astype(F32).reshape(ne, tn)
    before = (lax.broadcasted_iota(I32, (tn, tn), 0) < lax.broadcasted_iota(I32, (tn, tn), 1)).astype(BF16)
    prefix = jnp.dot(onehot.astype(BF16), before, preferred_element_type=F32)
    rank = (run_scr[...] + prefix).reshape(ng, gs, tn)
    ranks = [reduce_experts(jnp.where(pick, rank, 0.0)) for pick in picks]
    rank_ref[...] = jnp.concatenate(ranks, axis=0).astype(I32)
    run_scr[...] = run_scr[...] + jnp.sum(onehot, axis=1, keepdims=True)
    cnt_ref[...] = run_scr[...].astype(I32)


def _router(logits_t, b_router):
    ne, t = logits_t.shape
    ng = N_EXPERT_GROUPS
    gs = ne // ng
    tn = 512
    tok = pl.BlockSpec((TOP_K, tn), lambda i: (0, i))
    return pl.pallas_call(
        _router_kernel,
        grid=(t // tn,),
        in_specs=[
            pl.BlockSpec((ng, gs, tn), lambda i: (0, 0, i)),
            pl.BlockSpec((ng, gs, 1), lambda i: (0, 0, 0)),
        ],
        out_specs=[tok, tok, tok, pl.BlockSpec((ne, 1), lambda i: (0, 0))],
        out_shape=[jax.ShapeDtypeStruct((TOP_K, t), I32), jax.ShapeDtypeStruct((TOP_K, t), F32),
                   jax.ShapeDtypeStruct((TOP_K, t), I32), jax.ShapeDtypeStruct((ne, 1), I32)],
        scratch_shapes=[pltpu.VMEM((ne, 1), F32)],
        compiler_params=_params(("arbitrary",)),
    )(logits_t.reshape(ng, gs, t), b_router.reshape(ng, gs, 1))


def _dest_kernel(pstart_ref, idx_ref, rank_ref, dest_ref):
    idx = idx_ref[...]
    base = jnp.zeros(idx.shape, I32)
    for e in range(pstart_ref.shape[0]):
        base = jnp.where(idx == e, pstart_ref[e], base)
    dest_ref[...] = base + rank_ref[...]


def _dest(pstart, eidx, rank):
    k, t = eidx.shape
    tn = 2048
    tok = pl.BlockSpec((k, tn), lambda i, ps: (0, i))
    return pl.pallas_call(
        _dest_kernel,
        grid_spec=pltpu.PrefetchScalarGridSpec(
            num_scalar_prefetch=1, grid=(t // tn,), in_specs=[tok, tok], out_specs=tok),
        out_shape=jax.ShapeDtypeStruct((k, t), I32),
        compiler_params=_params(("arbitrary",)),
    )(pstart, eidx, rank)


def _slot_plan(counts):
    padded = (counts + MOE_SUB - 1) // MOE_SUB * MOE_SUB
    pend = jnp.cumsum(padded)
    pstart = pend - padded
    npad = (padded - counts)[:, None]
    ntail = MOE_SUB - npad
    tail0 = pend[-1] + jnp.cumsum(ntail, axis=0) - ntail
    j = jnp.arange(MOE_SUB, dtype=I32)[None, :]
    free_rows = jnp.where(j < npad, (pstart + counts)[:, None] + j, tail0 + j - npad)
    return jnp.stack([pstart, padded]).astype(I32), pstart.astype(I32), free_rows.astype(I32)


def _dispatch_kernel(dest_ref, pad_ref, x_ref, g_ref, sh_ref, sc_ref, xs_hbm, h_scr, zero_scr, sem):
    ntile, sub, d = h_scr.shape
    tm = ntile * sub
    k = dest_ref.shape[2] // tm
    npad = pad_ref.shape[2]
    assert npad % tm == 0

    h_scr[...] = _norm_mod(x_ref[...], g_ref[...], sh_ref[0], sc_ref[0]).reshape(ntile, sub, d)
    zero_scr[...] = jnp.zeros(zero_scr.shape, zero_scr.dtype)

    def issue(r8, _):
        for j in range(sub):
            for kk in range(k):
                slot = dest_ref[0, 0, (r8 * sub + j) * k + kk]
                pltpu.make_async_copy(h_scr.at[r8, pl.ds(j, 1)], xs_hbm.at[pl.ds(slot, 1)],
                                      sem).start(priority=kk % 2)
        return 0

    lax.fori_loop(0, ntile, issue, 0)

    def issue_pad(j, _):
        pltpu.make_async_copy(zero_scr.at[pl.ds(0, 1)], xs_hbm.at[pl.ds(pad_ref[0, 0, j], 1)], sem).start()
        return 0

    lax.fori_loop(0, npad, issue_pad, 0, unroll=8)
    for _ in range(k + npad // tm):
        pltpu.make_async_copy(x_ref, xs_hbm.at[pl.ds(0, tm)], sem).wait()


def _dispatch(dest, free_rows, x1, g2, sh2, sc2, n_rows, seq):
    t, d = x1.shape
    k = dest.shape[0]
    tm = 256
    per_b = seq // tm
    nblk = t // tm
    npad = free_rows.size // nblk
    dest_tm = dest.T.reshape(nblk, 1, tm * k)
    bvec = pl.BlockSpec((1, 1, d), lambda i: (i // per_b, 0, 0))
    return pl.pallas_call(
        _dispatch_kernel,
        grid=(nblk,),
        in_specs=[
            pl.BlockSpec((1, 1, tm * k), lambda i: (i, 0, 0), memory_space=pltpu.SMEM),
            pl.BlockSpec((1, 1, npad), lambda i: (i, 0, 0), memory_space=pltpu.SMEM),
            pl.BlockSpec((tm, d), lambda i: (i, 0)),
            pl.BlockSpec((1, d), lambda i: (0, 0)),
            bvec, bvec,
        ],
        out_specs=pl.BlockSpec(memory_space=pl.ANY),
        out_shape=jax.ShapeDtypeStruct((n_rows, d), F32),
        scratch_shapes=[pltpu.VMEM((tm // SUBLANES, SUBLANES, d), F32), pltpu.VMEM((SUBLANES, d), F32),
                        pltpu.SemaphoreType.DMA],
        compiler_params=_params(("arbitrary",)),
    )(dest_tm, free_rows.reshape(nblk, 1, npad), x1, g2.reshape(1, d), sh2, sc2)


def _expert_kernel(meta_ref, x_hbm, wg_ref, wu_ref, wd_ref, y_hbm, wg_s, wu_s, wd_s, xbuf, ybuf,
                   xsem, ysem, ypend):
    e = pl.program_id(0)
    ne = pl.num_programs(0)
    chunk = xbuf.shape[1]
    nsub = chunk // MOE_SUB
    start = meta_ref[0, e]
    nrows = meta_ref[1, e]

    def sub(r):
        return pl.ds(pl.multiple_of(r, MOE_SUB), MOE_SUB)

    def x_piece(r0, slot, j):
        return pltpu.make_async_copy(x_hbm.at[sub(r0 + j * MOE_SUB)], xbuf.at[slot, sub(j * MOE_SUB)],
                                     xsem.at[slot])

    def y_piece(r0, slot, j):
        return pltpu.make_async_copy(ybuf.at[slot, sub(j * MOE_SUB)], y_hbm.at[sub(r0 + j * MOE_SUB)],
                                     ysem.at[slot])

    def pieces(rows_left):
        return jnp.clip(rows_left, 0, chunk) // MOE_SUB

    def x_start(r0, slot, n):
        def body(j, carry):
            x_piece(r0, slot, j).start()
            return carry

        lax.fori_loop(0, n, body, 0)

    def x_wait(slot, n):
        def body(j, carry):
            x_piece(0, slot, 0).wait()
            return carry

        lax.fori_loop(0, n, body, 0)

    def y_drain(slot):
        def body(j, carry):
            y_piece(0, slot, 0).wait()
            return carry

        lax.fori_loop(0, ypend[slot], body, 0)
        ypend[slot] = 0

    @pl.when(e == 0)
    def _():
        ypend[0] = 0
        ypend[1] = 0
        x_start(start, 0, pieces(nrows))

    wg_s[...] = wg_ref[0, 0].astype(BF16)
    wu_s[...] = wu_ref[0, 0].astype(BF16)
    wd_s[...] = wd_ref[0, 0].astype(BF16)

    def compute(slot, m):
        x = xbuf[slot, pl.ds(0, m), :].astype(BF16)
        gate = jnp.dot(x, wg_s[...], preferred_element_type=F32)
        up = jnp.dot(x, wu_s[...], preferred_element_type=F32)
        hid = (_silu(gate) * up).astype(BF16)
        ybuf[slot, pl.ds(0, m), :] = jnp.dot(hid, wd_s[...], preferred_element_type=F32)

    def run_chunk(c, m_static, n_here):
        slot = c & 1
        r0 = start + c * chunk
        x_wait(slot, n_here)
        x_start(r0 + chunk, 1 - slot, pieces(nrows - (c + 1) * chunk))
        y_drain(slot)
        compute(slot, m_static)
        for j in range(m_static // MOE_SUB):
            y_piece(r0, slot, j).start()
        ypend[slot] = m_static // MOE_SUB

    n_full = nrows // chunk

    def full_body(c, carry):
        run_chunk(c, chunk, nsub)
        return carry

    lax.fori_loop(0, n_full, full_body, 0)

    rem = nrows - n_full * chunk
    for m in range(MOE_SUB, chunk, MOE_SUB):
        @pl.when(rem == m)
        def _(m=m):
            run_chunk(n_full, m, m // MOE_SUB)

    @pl.when(e + 1 < ne)
    def _():
        x_start(meta_ref[0, e + 1], 0, pieces(meta_ref[1, e + 1]))

    @pl.when(e + 1 == ne)
    def _():
        y_drain(0)
        y_drain(1)
        tail0 = start + nrows
        ntail = (y_hbm.shape[0] - tail0) // MOE_SUB
        ybuf[0, pl.ds(0, MOE_SUB), :] = jnp.zeros((MOE_SUB, ybuf.shape[2]), ybuf.dtype)

        def fill(j, carry):
            y_piece(tail0 + j * MOE_SUB, 0, 0).start()
            return carry

        lax.fori_loop(0, ntail, fill, 0)
        ypend[0] = ntail
        y_drain(0)


def _experts(meta, x_sorted, w_gate, w_up, w_down, layer, n_slots):
    d = x_sorted.shape[1]
    _, ne, _, f = w_gate.shape
    chunk = 512
    wspec = lambda a, b: pl.BlockSpec((1, 1, a, b), lambda e, m: (layer, e, 0, 0))
    grid_spec = pltpu.PrefetchScalarGridSpec(
        num_scalar_prefetch=1,
        grid=(ne,),
        in_specs=[pl.BlockSpec(memory_space=pl.ANY), wspec(d, f), wspec(d, f), wspec(f, d)],
        out_specs=pl.BlockSpec(memory_space=pl.ANY),
        scratch_shapes=[
            pltpu.VMEM((d, f), BF16), pltpu.VMEM((d, f), BF16), pltpu.VMEM((f, d), BF16),
            pltpu.VMEM((2, chunk, d), F32), pltpu.VMEM((2, chunk, d), F32),
            pltpu.SemaphoreType.DMA((2,)), pltpu.SemaphoreType.DMA((2,)),
            pltpu.SMEM((2,), I32),
        ],
    )
    return pl.pallas_call(
        _expert_kernel,
        grid_spec=grid_spec,
        out_shape=jax.ShapeDtypeStruct((n_slots, d), F32),
        compiler_params=_params(("arbitrary",)),
    )(meta, x_sorted, w_gate, w_up, w_down)


def _combine_kernel(dest_ref, wt_ref, x_ref, g2_ref, sh_ref, sc_ref, gt_ref, wg_ref, wu_ref, wd_ref, gf_ref,
                    y_hbm, o_ref, buf, sem, *, final_norm):
    k, ntile, sub, d = buf.shape
    tm = ntile * sub

    def issue(r8, _):
        for j in range(sub):
            for kk in range(k):
                slot = dest_ref[0, 0, (r8 * sub + j) * k + kk]
                pltpu.make_async_copy(y_hbm.at[pl.ds(slot, 1)], buf.at[kk, r8, pl.ds(j, 1)],
                                      sem).start(priority=kk % 2)
        return 0

    lax.fori_loop(0, ntile, issue, 0)

    x1 = x_ref[...]
    h = _norm_mod(x1, g2_ref[...], sh_ref[0], sc_ref[0]).astype(BF16)
    gate = jnp.dot(h, wg_ref[...], preferred_element_type=F32)
    up = jnp.dot(h, wu_ref[...], preferred_element_type=F32)
    acc = jnp.dot((_silu(gate) * up).astype(BF16), wd_ref[...], preferred_element_type=F32)

    for kk in range(k):
        pltpu.make_async_copy(buf.at[kk], buf.at[kk], sem).wait()
    wt = wt_ref[...]
    for kk in range(k):
        acc = acc + wt[:, kk:kk + 1] * buf[kk].reshape(tm, d)
    out = x1 + gt_ref[0] * acc
    if final_norm:
        ms = jnp.mean(out * out, axis=-1, keepdims=True)
        out = out * lax.rsqrt(ms + NORM_EPS) * gf_ref[...]
    o_ref[...] = out


def _combine(dest, wts, x1, g2, sh2, sc2, gt2, w_sg, w_su, w_sd, g_final, y_sorted, seq, final_norm):
    t, d = x1.shape
    k = dest.shape[0]
    tm = 128
    per_b = seq // tm
    nblk = t // tm
    dest_tm = dest.T.reshape(nblk, 1, tm * k)
    row = lambda i: (i, 0)
    vec = pl.BlockSpec((1, d), lambda i: (0, 0))
    bvec = pl.BlockSpec((1, 1, d), lambda i: (i // per_b, 0, 0))
    return pl.pallas_call(
        functools.partial(_combine_kernel, final_norm=final_norm),
        grid=(nblk,),
        in_specs=[
            pl.BlockSpec((1, 1, tm * k), lambda i: (i, 0, 0), memory_space=pltpu.SMEM),
            pl.BlockSpec((tm, k), row),
            pl.BlockSpec((tm, d), row),
            vec, bvec, bvec, bvec,
            _resident(w_sg.shape), _resident(w_su.shape), _resident(w_sd.shape),
            vec,
            pl.BlockSpec(memory_space=pl.ANY),
        ],
        out_specs=pl.BlockSpec((tm, d), row),
        out_shape=jax.ShapeDtypeStruct((t, d), F32),
        scratch_shapes=[pltpu.VMEM((k, tm // SUBLANES, SUBLANES, d), F32), pltpu.SemaphoreType.DMA],
        compiler_params=_params(("arbitrary",)),
    )(dest_tm, wts.T, x1, g2.reshape(1, d), sh2, sc2, gt2, w_sg.astype(BF16), w_su.astype(BF16),
      w_sd.astype(BF16), g_final.reshape(1, d), y_sorted)


def kernel(x, c, w_ada, b_ada, norm_mix, w_in, b_in, attn_sinks, w_attn_o, ssm_a_re, ssm_a_im, ssm_log_dt,
           ssm_b_re, ssm_b_im, ssm_c_re, ssm_c_im, ssm_d, w_glu, w_mix_out, norm_ffn, w_router, b_router,
           w_exp_gate, w_exp_up, w_exp_down, w_sh_gate, w_sh_up, w_sh_down, norm_final):
    batch, seq, d = x.shape
    depth = w_ada.shape[0]
    n_experts = w_router.shape[2]
    ssm_width = ssm_d.shape[1]
    ucol0 = ATTN_WIDTH + 2 * KV_WIDTH
    gacol = (ucol0 + ssm_width) // d
    gscol = gacol + 1
    n_slots = batch * seq * TOP_K + n_experts * MOE_SUB

    c_pad = jnp.zeros((SUBLANES, d), F32).at[:batch].set(c)
    mod = _ada_mod(c_pad, w_ada, b_ada)[:, :batch]
    mod = mod.reshape(depth, batch, 6, 1, d)

    x2 = x.reshape(batch * seq, d)
    for l in range(depth):
        sh1, sc1, gt1, sh2, sc2, gt2 = (mod[l, :, m] for m in range(6))
        proj = _inproj(x2, norm_mix[l], sh1, sc1, w_in, b_in, l, seq)
        a_gated = _attention(proj, attn_sinks[l], w_attn_o[l], batch, seq, d, gacol)
        ssm_mats = _ssm_params(ssm_a_re[l], ssm_a_im[l], ssm_log_dt[l], ssm_b_re[l], ssm_b_im[l],
                               ssm_c_re[l], ssm_c_im[l], ssm_d[l])
        y = _ssm(proj, ssm_mats, batch, seq, ucol0)
        x1, logits_t = _mixout(a_gated, y, proj, x2, gt1, w_glu[l], w_mix_out[l], norm_ffn[l],
                               sh2, sc2, w_router[l].T, seq, gscol)
        eidx, wts, rank, counts = _router(logits_t, b_router[l])
        meta, pstart, free_rows = _slot_plan(counts[:, 0])
        dest = _dest(pstart, eidx, rank)
        x_sorted = _dispatch(dest, free_rows, x1, norm_ffn[l], sh2, sc2, n_slots, seq)
        y_sorted = _experts(meta, x_sorted, w_exp_gate, w_exp_up, w_exp_down, l, n_slots)
        x2 = _combine(dest, wts, x1, norm_ffn[l], sh2, sc2, gt2, w_sh_gate[l], w_sh_up[l], w_sh_down[l],
                      norm_final, y_sorted, seq, final_norm=(l == depth - 1))
    return x2.reshape(batch, seq, d)
```

```python
import functools
import math

import jax
import jax.numpy as jnp
from jax import lax
from jax.experimental import pallas as pl
from jax.experimental.pallas import tpu as pltpu

F32 = jnp.float32
BF16 = jnp.bfloat16
I32 = jnp.int32
HIGHEST = lax.Precision.HIGHEST

N_Q_HEADS = 16
N_KV_HEADS = 4
HEAD_DIM = 64
Q_PER_KV = N_Q_HEADS // N_KV_HEADS
WINDOW = 128
ATTN_WIDTH = N_Q_HEADS * HEAD_DIM
KV_WIDTH = N_KV_HEADS * HEAD_DIM
SSM_GROUP_CH = 16
SSM_STATE = 64
N_EXPERT_GROUPS = 8
TOPK_GROUPS = 4
TOP_K = 8
ROUTED_SCALE = 2.5
NORM_EPS = 1e-6

LANES = 128
SUBLANES = 8
VMEM_LIMIT_BYTES = 56 * 1024 * 1024

SSM_GROUPS_PER_BLOCK = LANES // SSM_GROUP_CH
SSM_SEGMENTS = SUBLANES
MOE_SUB = 128
NEG_BIG = -1e30


def _sigmoid(x):
    return 1.0 / (1.0 + jnp.exp(-x))


def _silu(x):
    return x * _sigmoid(x)


def _params(sem):
    return pltpu.CompilerParams(dimension_semantics=sem, vmem_limit_bytes=VMEM_LIMIT_BYTES)


def _resident(shape):
    nd = len(shape)
    return pl.BlockSpec(shape, lambda *_: (0,) * nd, pipeline_mode=pl.Buffered(1))


def _ada_kernel(c_ref, w_ref, b_ref, o_ref):
    c_act = _silu(c_ref[...]).astype(BF16)
    o_ref[0] = jnp.dot(c_act, w_ref[0].astype(BF16), preferred_element_type=F32) + b_ref[0]


def _ada_mod(c_pad, w_ada, b_ada):
    depth, d, n = w_ada.shape
    tn = 1024
    return pl.pallas_call(
        _ada_kernel,
        grid=(depth, n // tn),
        in_specs=[
            pl.BlockSpec((SUBLANES, d), lambda l, j: (0, 0)),
            pl.BlockSpec((1, d, tn), lambda l, j: (l, 0, j)),
            pl.BlockSpec((1, 1, tn), lambda l, j: (l, 0, j)),
        ],
        out_specs=pl.BlockSpec((1, SUBLANES, tn), lambda l, j: (l, 0, j)),
        out_shape=jax.ShapeDtypeStruct((depth, SUBLANES, n), F32),
        compiler_params=_params(("arbitrary", "arbitrary")),
    )(c_pad, w_ada, b_ada.reshape(depth, 1, n))


def _norm_mod(x, g, shift, scale):
    ms = jnp.mean(x * x, axis=-1, keepdims=True)
    xn = x * lax.rsqrt(ms + NORM_EPS) * g
    return xn * (1.0 + scale) + shift


def _inproj_kernel(x_ref, g_ref, sh_ref, sc_ref, w_ref, b_ref, o_ref, h_scr):
    @pl.when(pl.program_id(1) == 0)
    def _():
        h_scr[...] = _norm_mod(x_ref[...], g_ref[...], sh_ref[0], sc_ref[0]).astype(BF16)

    acc = jnp.dot(h_scr[...], w_ref[0], preferred_element_type=F32)
    o_ref[...] = (acc + b_ref[0]).astype(o_ref.dtype)


def _inproj(x2, g, shift, scale, w_all, b_all, layer, seq):
    t, d = x2.shape
    depth, _, n = w_all.shape
    tm, tn = 1024, 1024
    per_b = seq // tm
    return pl.pallas_call(
        _inproj_kernel,
        grid=(t // tm, n // tn),
        in_specs=[
            pl.BlockSpec((tm, d), lambda i, j: (i, 0)),
            pl.BlockSpec((1, d), lambda i, j: (0, 0)),
            pl.BlockSpec((1, 1, d), lambda i, j: (i // per_b, 0, 0)),
            pl.BlockSpec((1, 1, d), lambda i, j: (i // per_b, 0, 0)),
            pl.BlockSpec((1, d, tn), lambda i, j: (layer, 0, j)),
            pl.BlockSpec((1, 1, tn), lambda i, j: (layer, 0, j)),
        ],
        out_specs=pl.BlockSpec((tm, tn), lambda i, j: (i, j)),
        out_shape=jax.ShapeDtypeStruct((t, n), BF16),
        scratch_shapes=[pltpu.VMEM((tm, d), BF16)],
        compiler_params=_params(("arbitrary", "arbitrary")),
    )(x2, g.reshape(1, d), shift, scale, w_all, b_all.reshape(depth, 1, n))


def _attn_kernel(sink_ref, q_ref, kc_ref, vc_ref, kp_ref, vp_ref, g_ref, wo_ref, o_ref):
    i = pl.program_id(1)
    q = q_ref[...]
    k_cat = jnp.concatenate([kp_ref[...], kc_ref[...]], axis=0)
    v_cat = jnp.concatenate([vp_ref[...], vc_ref[...]], axis=0)
    row = lax.broadcasted_iota(I32, (WINDOW, 2 * WINDOW), 0)
    col = lax.broadcasted_iota(I32, (WINDOW, 2 * WINDOW), 1)
    mask = (col > row) & (col <= row + WINDOW) & ((col >= WINDOW) | (i > 0))
    inv_sqrt = 1.0 / math.sqrt(HEAD_DIM)

    nk = 2 * WINDOW
    pair_w = 2 * HEAD_DIM
    zeros = jnp.zeros((nk, HEAD_DIM), k_cat.dtype)
    first_head = lax.broadcasted_iota(I32, (WINDOW, pair_w), 1) < HEAD_DIM

    def block_diag(x):
        return jnp.concatenate([jnp.concatenate([x, zeros], axis=1),
                                jnp.concatenate([zeros, x], axis=1)], axis=0)

    def softmax_terms(s, sink):
        s = jnp.where(mask, s * inv_sqrt, NEG_BIG)
        m = jnp.maximum(jnp.max(s, axis=-1, keepdims=True), sink)
        p = jnp.exp(s - m)
        return p.astype(BF16), jnp.sum(p, axis=-1, keepdims=True) + jnp.exp(sink - m)

    outs = []
    for j in range(N_KV_HEADS):
        k_bd = block_diag(k_cat[:, j * HEAD_DIM:(j + 1) * HEAD_DIM])
        v_bd = block_diag(v_cat[:, j * HEAD_DIM:(j + 1) * HEAD_DIM])
        for pair in range(Q_PER_KV // 2):
            h0 = j * Q_PER_KV + 2 * pair
            q2 = q[:, h0 * HEAD_DIM:(h0 + 2) * HEAD_DIM]
            s2 = lax.dot_general(q2, k_bd, (((1,), (1,)), ((), ())), preferred_element_type=F32)
            p_a, den_a = softmax_terms(s2[:, :nk], sink_ref[h0])
            p_b, den_b = softmax_terms(s2[:, nk:], sink_ref[h0 + 1])
            o2 = jnp.dot(jnp.concatenate([p_a, p_b], axis=1), v_bd, preferred_element_type=F32)
            outs.append(o2 / jnp.where(first_head, den_a, den_b))
    attn = jnp.concatenate(outs, axis=-1).astype(BF16)
    proj = jnp.dot(attn, wo_ref[...], preferred_element_type=F32)
    o_ref[...] = (_sigmoid(g_ref[...].astype(F32)) * proj).astype(o_ref.dtype)


def _attention(proj, sinks, w_o, batch, seq, d, gcol):
    t = proj.shape[0]
    nb = seq // WINDOW
    kcol = ATTN_WIDTH // KV_WIDTH
    vcol = kcol + 1
    cur = lambda b, i: b * nb + i
    prev = lambda b, i: b * nb + jnp.maximum(i - 1, 0)
    return pl.pallas_call(
        _attn_kernel,
        grid=(batch, nb),
        in_specs=[
            pl.BlockSpec(memory_space=pltpu.SMEM),
            pl.BlockSpec((WINDOW, ATTN_WIDTH), lambda b, i: (cur(b, i), 0)),
            pl.BlockSpec((WINDOW, KV_WIDTH), lambda b, i: (cur(b, i), kcol)),
            pl.BlockSpec((WINDOW, KV_WIDTH), lambda b, i: (cur(b, i), vcol)),
            pl.BlockSpec((WINDOW, KV_WIDTH), lambda b, i: (prev(b, i), kcol)),
            pl.BlockSpec((WINDOW, KV_WIDTH), lambda b, i: (prev(b, i), vcol)),
            pl.BlockSpec((WINDOW, d), lambda b, i: (cur(b, i), gcol)),
            _resident(w_o.shape),
        ],
        out_specs=pl.BlockSpec((WINDOW, d), lambda b, i: (cur(b, i), 0)),
        out_shape=jax.ShapeDtypeStruct((t, d), BF16),
        compiler_params=_params(("arbitrary", "arbitrary")),
    )(sinks, proj, proj, proj, proj, proj, proj, w_o.astype(BF16))


def _ssm_kernel(u_ref, ar_ref, ai_ref, br_ref, bi_ref, cr_ref, cin_ref, d_ref, y_ref,
                uperm, bur, bui, yperm):
    seq = u_ref.shape[0]
    nseg = SSM_SEGMENTS
    n = seq // nseg
    ns = bur.shape[1]

    for r in range(nseg):
        uperm[pl.ds(r, n, stride=nseg), :] = u_ref[r * n:(r + 1) * n, :].astype(F32)

    chunk = 512

    def bu_body(c, _):
        r0 = pl.multiple_of(c * chunk, chunk)
        up = uperm[pl.ds(r0, chunk), :].astype(BF16)
        bur[pl.ds(r0, chunk), :] = (jnp.dot(up, br_ref[0, 0], preferred_element_type=F32)
                                    + jnp.dot(up, br_ref[0, 1], preferred_element_type=F32))
        bui[pl.ds(r0, chunk), :] = (jnp.dot(up, bi_ref[0, 0], preferred_element_type=F32)
                                    + jnp.dot(up, bi_ref[0, 1], preferred_element_type=F32))
        return 0

    lax.fori_loop(0, seq // chunk, bu_body, 0)

    a_r1 = ar_ref[0]
    a_i1 = ai_ref[0]
    a_r = jnp.broadcast_to(a_r1, (nseg, ns))
    a_i = jnp.broadcast_to(a_i1, (nseg, ns))

    def scan_body(s, carry):
        xr, xi = carry
        r0 = pl.multiple_of(s * nseg, nseg)
        nxr = a_r * xr - a_i * xi + bur[pl.ds(r0, nseg), :]
        nxi = a_r * xi + a_i * xr + bui[pl.ds(r0, nseg), :]
        bur[pl.ds(r0, nseg), :] = nxr
        bui[pl.ds(r0, nseg), :] = nxi
        return nxr, nxi

    zeros = jnp.zeros((nseg, ns), F32)
    er, ei = lax.fori_loop(0, n, scan_body, (zeros, zeros), unroll=4)

    pr, pi = a_r1, a_i1
    for _ in range(n.bit_length() - 1):
        pr, pi = pr * pr - pi * pi, 2.0 * pr * pi

    cr_rows = [jnp.zeros((1, ns), F32)]
    ci_rows = [jnp.zeros((1, ns), F32)]
    for r in range(1, nseg):
        pcr, pci = cr_rows[-1], ci_rows[-1]
        cr_rows.append(er[r - 1:r, :] + pr * pcr - pi * pci)
        ci_rows.append(ei[r - 1:r, :] + pr * pci + pi * pcr)
    c_r = jnp.concatenate(cr_rows, axis=0)
    c_i = jnp.concatenate(ci_rows, axis=0)

    def fix_body(s, carry):
        qr, qi = carry
        r0 = pl.multiple_of(s * nseg, nseg)
        bur[pl.ds(r0, nseg), :] = bur[pl.ds(r0, nseg), :] + (qr * c_r - qi * c_i)
        bui[pl.ds(r0, nseg), :] = bui[pl.ds(r0, nseg), :] + (qr * c_i + qi * c_r)
        return qr * a_r - qi * a_i, qr * a_i + qi * a_r

    lax.fori_loop(0, n, fix_body, (a_r, a_i), unroll=4)

    def out_body(c, _):
        r0 = pl.multiple_of(c * chunk, chunk)
        y = jnp.dot(bur[pl.ds(r0, chunk), :].astype(BF16), cr_ref[0], preferred_element_type=F32)
        y = y + jnp.dot(bui[pl.ds(r0, chunk), :].astype(BF16), cin_ref[0], preferred_element_type=F32)
        y = y + d_ref[0] * uperm[pl.ds(r0, chunk), :]
        yperm[pl.ds(r0, chunk), :] = jax.nn.gelu(y, approximate=True)
        return 0

    lax.fori_loop(0, seq // chunk, out_body, 0)

    for r in range(nseg):
        y_ref[r * n:(r + 1) * n, :] = yperm[pl.ds(r, n, stride=nseg), :].astype(y_ref.dtype)


def _ssm_params(a_re, a_im, log_dt, b_re, b_im, c_re, c_im, d_skip):
    g, p = a_re.shape
    c = b_re.shape[-1]
    gpb = SSM_GROUPS_PER_BLOCK
    nblk = g // gpb
    dt = jnp.exp(log_dt)[:, None]
    mag = jnp.exp(dt * a_re)
    ab_r, ab_i = mag * jnp.cos(dt * a_im), mag * jnp.sin(dt * a_im)
    den = a_re * a_re + a_im * a_im
    nr, ni = ab_r - 1.0, ab_i
    f_r = (nr * a_re + ni * a_im) / den
    f_i = (ni * a_re - nr * a_im) / den
    bb_r = f_r[..., None] * b_re - f_i[..., None] * b_im
    bb_i = f_r[..., None] * b_im + f_i[..., None] * b_re
    eye = jnp.eye(gpb, dtype=F32)

    def in_mat(bb):
        return jnp.einsum("bgpc,gh->bgchp", bb.reshape(nblk, gpb, p, c), eye).reshape(nblk, gpb * c, gpb * p)

    def out_mat(cc):
        return jnp.einsum("bgcp,gh->bgphc", cc.reshape(nblk, gpb, c, p), eye).reshape(nblk, gpb * p, gpb * c)

    def hi_lo(m):
        hi = m.astype(BF16)
        return jnp.stack([hi, (m - hi.astype(F32)).astype(BF16)], axis=1)

    return (ab_r.reshape(nblk, 1, gpb * p), ab_i.reshape(nblk, 1, gpb * p),
            hi_lo(in_mat(bb_r)), hi_lo(in_mat(bb_i)), out_mat(c_re).astype(BF16), out_mat(-c_im).astype(BF16),
            d_skip.reshape(nblk, 1, gpb * c))


def _ssm(proj, ssm_mats, batch, seq, ucol0):
    ar, ai, br, bi, cr, cin, dsk = ssm_mats
    nblk = ar.shape[0]
    nu, ns = br.shape[2], br.shape[3]
    ublk0 = ucol0 // nu
    blk3 = lambda shape: pl.BlockSpec((1,) + shape, lambda b, g: (g,) + (0,) * len(shape))
    return pl.pallas_call(
        _ssm_kernel,
        grid=(batch, nblk),
        in_specs=[
            pl.BlockSpec((seq, nu), lambda b, g: (b, ublk0 + g)),
            blk3((1, ns)), blk3((1, ns)), blk3((2, nu, ns)), blk3((2, nu, ns)),
            blk3((ns, nu)), blk3((ns, nu)), blk3((1, nu)),
        ],
        out_specs=pl.BlockSpec((seq, nu), lambda b, g: (b, g)),
        out_shape=jax.ShapeDtypeStruct((batch * seq, nblk * nu), BF16),
        scratch_shapes=[
            pltpu.VMEM((seq, nu), F32), pltpu.VMEM((seq, ns), F32),
            pltpu.VMEM((seq, ns), F32), pltpu.VMEM((seq, nu), F32),
        ],
        compiler_params=_params(("arbitrary", "arbitrary")),
    )(proj, ar, ai, br, bi, cr, cin, dsk)


def _mixout_kernel(a_ref, y_ref, gs_ref, x_ref, gt_ref, wglu_ref, wmix_ref, g2_ref, sh_ref, sc_ref,
                   wr_ref, x1_ref, lg_ref):
    d = x_ref.shape[1]
    y = y_ref[...]
    r = None
    nsplit = 2
    w = d // nsplit
    for c in range(nsplit):
        cols = slice(c * w, (c + 1) * w)
        za = jnp.dot(y, wglu_ref[:, c * w:(c + 1) * w], preferred_element_type=F32)
        zb = jnp.dot(y, wglu_ref[:, d + c * w:d + (c + 1) * w], preferred_element_type=F32)
        mixed = a_ref[:, cols].astype(F32) + _sigmoid(gs_ref[:, cols].astype(F32)) * (za * _sigmoid(zb))
        part = jnp.dot(mixed.astype(BF16), wmix_ref[c * w:(c + 1) * w, :], preferred_element_type=F32)
        r = part if r is None else r + part
    x1 = x_ref[...] + gt_ref[0] * r
    x1_ref[...] = x1
    h2 = _norm_mod(x1, g2_ref[...], sh_ref[0], sc_ref[0])
    h_hi = h2.astype(BF16)
    h_lo = (h2 - h_hi.astype(F32)).astype(BF16)
    nt = lambda a, b: lax.dot_general(a, b, (((1,), (1,)), ((), ())), preferred_element_type=F32)
    lg_ref[...] = nt(wr_ref[0], h_hi) + nt(wr_ref[0], h_lo) + nt(wr_ref[1], h_hi)


def _mixout(a_gated, y, proj, x2, gt1, w_glu, w_mix, g2, sh2, sc2, w_router_t, seq, gscol):
    t, d = x2.shape
    tm = 512
    per_b = seq // tm
    ne = w_router_t.shape[0]
    wr_hi = w_router_t.astype(BF16)
    wr_split = jnp.stack([wr_hi, (w_router_t - wr_hi.astype(F32)).astype(BF16)])
    row = lambda i: (i, 0)
    bvec = pl.BlockSpec((1, 1, d), lambda i: (i // per_b, 0, 0))
    return pl.pallas_call(
        _mixout_kernel,
        grid=(t // tm,),
        in_specs=[
            pl.BlockSpec((tm, d), row),
            pl.BlockSpec((tm, y.shape[1]), row),
            pl.BlockSpec((tm, d), lambda i: (i, gscol)),
            pl.BlockSpec((tm, d), row),
            bvec,
            _resident(w_glu.shape),
            _resident(w_mix.shape),
            pl.BlockSpec((1, d), lambda i: (0, 0)),
            bvec, bvec,
            _resident(wr_split.shape),
        ],
        out_specs=[
            pl.BlockSpec((tm, d), row),
            pl.BlockSpec((ne, tm), lambda i: (0, i)),
        ],
        out_shape=[
            jax.ShapeDtypeStruct((t, d), F32),
            jax.ShapeDtypeStruct((ne, t), F32),
        ],
        compiler_params=_params(("arbitrary",)),
    )(a_gated, y, proj, x2, gt1, w_glu.astype(BF16), w_mix.astype(BF16), g2.reshape(1, d), sh2, sc2,
      wr_split)


def _router_kernel(lg_ref, b_ref, idx_ref, w_ref, rank_ref, cnt_ref, run_scr):
    ng, gs, tn = lg_ref.shape
    ne = ng * gs

    @pl.when(pl.program_id(0) == 0)
    def _():
        run_scr[...] = jnp.zeros(run_scr.shape, run_scr.dtype)

    scores = _sigmoid(lg_ref[...])
    sel = scores + b_ref[...]
    mem = lax.broadcasted_iota(I32, (ng, gs, tn), 1)
    grp = lax.broadcasted_iota(I32, (ng, gs, tn), 0)
    eid = grp * gs + mem
    ninf = -jnp.inf

    m1 = jnp.max(sel, axis=1, keepdims=True)
    first = jnp.min(jnp.where(sel == m1, mem, gs), axis=1, keepdims=True)
    m2 = jnp.max(jnp.where(mem == first, ninf, sel), axis=1, keepdims=True)
    gscore = m1 + m2

    gid = lax.broadcasted_iota(I32, (ng, 1, tn), 0)
    gmask = jnp.zeros((ng, 1, tn), jnp.bool_)
    for _ in range(TOPK_GROUPS):
        m = jnp.max(gscore, axis=0, keepdims=True)
        pick = gid == jnp.min(jnp.where(gscore == m, gid, ng), axis=0, keepdims=True)
        gmask = gmask | pick
        gscore = jnp.where(pick, ninf, gscore)

    def reduce_experts(v):
        return jnp.sum(jnp.sum(v, axis=1, keepdims=True), axis=0, keepdims=True)[0]

    cand = jnp.where(gmask, sel, ninf)
    idxs, wts, picks = [], [], []
    for _ in range(TOP_K):
        m = jnp.max(jnp.max(cand, axis=1, keepdims=True), axis=0, keepdims=True)
        e = jnp.where(cand == m, eid, ne)
        e = jnp.min(jnp.min(e, axis=1, keepdims=True), axis=0, keepdims=True)
        pick = eid == e
        idxs.append(e[0])
        wts.append(reduce_experts(jnp.where(pick, scores, 0.0)))
        picks.append(pick)
        cand = jnp.where(pick, ninf, cand)
    idx_ref[...] = jnp.concatenate(idxs, axis=0)
    w = jnp.concatenate(wts, axis=0)
    w_ref[...] = w / jnp.sum(w, axis=0, keepdims=True) * ROUTED_SCALE

    chosen = picks[0]
    for pick in picks[1:]:
        chosen = chosen | pick
    onehot = chosen.astype(F32).reshape(ne, tn)
    before = (lax.broadcasted_iota(I32, (tn, tn), 0) < lax.broadcasted_iota(I32, (tn, tn), 1)).astype(BF16)
    prefix = jnp.dot(onehot.astype(BF16), before, preferred_element_type=F32)
    rank = (run_scr[...] + prefix).reshape(ng, gs, tn)
    ranks = [reduce_experts(jnp.where(pick, rank, 0.0)) for pick in picks]
    rank_ref[...] = jnp.concatenate(ranks, axis=0).astype(I32)
    run_scr[...] = run_scr[...] + jnp.sum(onehot, axis=1, keepdims=True)
    cnt_ref[...] = run_scr[...].astype(I32)


def _router(logits_t, b_router):
    ne, t = logits_t.shape
    ng = N_EXPERT_GROUPS
    gs = ne // ng
    tn = 512
    tok = pl.BlockSpec((TOP_K, tn), lambda i: (0, i))
    return pl.pallas_call(
        _router_kernel,
        grid=(t // tn,),
        in_specs=[
            pl.BlockSpec((ng, gs, tn), lambda i: (0, 0, i)),
            pl.BlockSpec((ng, gs, 1), lambda i: (0, 0, 0)),
        ],
        out_specs=[tok, tok, tok, pl.BlockSpec((ne, 1), lambda i: (0, 0))],
        out_shape=[jax.ShapeDtypeStruct((TOP_K, t), I32), jax.ShapeDtypeStruct((TOP_K, t), F32),
                   jax.ShapeDtypeStruct((TOP_K, t), I32), jax.ShapeDtypeStruct((ne, 1), I32)],
        scratch_shapes=[pltpu.VMEM((ne, 1), F32)],
        compiler_params=_params(("arbitrary",)),
    )(logits_t.reshape(ng, gs, t), b_router.reshape(ng, gs, 1))


def _dest_kernel(pstart_ref, idx_ref, rank_ref, dest_ref):
    idx = idx_ref[...]
    base = jnp.zeros(idx.shape, I32)
    for e in range(pstart_ref.shape[0]):
        base = jnp.where(idx == e, pstart_ref[e], base)
    dest_ref[...] = base + rank_ref[...]


def _dest(pstart, eidx, rank):
    k, t = eidx.shape
    tn = 2048
    tok = pl.BlockSpec((k, tn), lambda i, ps: (0, i))
    return pl.pallas_call(
        _dest_kernel,
        grid_spec=pltpu.PrefetchScalarGridSpec(
            num_scalar_prefetch=1, grid=(t // tn,), in_specs=[tok, tok], out_specs=tok),
        out_shape=jax.ShapeDtypeStruct((k, t), I32),
        compiler_params=_params(("arbitrary",)),
    )(pstart, eidx, rank)


def _slot_plan(counts):
    padded = (counts + MOE_SUB - 1) // MOE_SUB * MOE_SUB
    pend = jnp.cumsum(padded)
    pstart = pend - padded
    npad = (padded - counts)[:, None]
    ntail = MOE_SUB - npad
    tail0 = pend[-1] + jnp.cumsum(ntail, axis=0) - ntail
    j = jnp.arange(MOE_SUB, dtype=I32)[None, :]
    free_rows = jnp.where(j < npad, (pstart + counts)[:, None] + j, tail0 + j - npad)
    return jnp.stack([pstart, padded]).astype(I32), pstart.astype(I32), free_rows.astype(I32)


def _dispatch_kernel(dest_ref, pad_ref, x_ref, g_ref, sh_ref, sc_ref, xs_hbm, h_scr, zero_scr, sem):
    ntile, sub, d = h_scr.shape
    tm = ntile * sub
    k = dest_ref.shape[2] // tm
    npad = pad_ref.shape[2]
    assert npad % tm == 0

    h_scr[...] = _norm_mod(x_ref[...], g_ref[...], sh_ref[0], sc_ref[0]).reshape(ntile, sub, d)
    zero_scr[...] = jnp.zeros(zero_scr.shape, zero_scr.dtype)

    def issue(r8, _):
        for j in range(sub):
            for kk in range(k):
                slot = dest_ref[0, 0, (r8 * sub + j) * k + kk]
                pltpu.make_async_copy(h_scr.at[r8, pl.ds(j, 1)], xs_hbm.at[pl.ds(slot, 1)],
                                      sem).start(priority=kk % 2)
        return 0

    lax.fori_loop(0, ntile, issue, 0)

    def issue_pad(j, _):
        pltpu.make_async_copy(zero_scr.at[pl.ds(0, 1)], xs_hbm.at[pl.ds(pad_ref[0, 0, j], 1)], sem).start()
        return 0

    lax.fori_loop(0, npad, issue_pad, 0, unroll=8)
    for _ in range(k + npad // tm):
        pltpu.make_async_copy(x_ref, xs_hbm.at[pl.ds(0, tm)], sem).wait()


def _dispatch(dest, free_rows, x1, g2, sh2, sc2, n_rows, seq):
    t, d = x1.shape
    k = dest.shape[0]
    tm = 256
    per_b = seq // tm
    nblk = t // tm
    npad = free_rows.size // nblk
    dest_tm = dest.T.reshape(nblk, 1, tm * k)
    bvec = pl.BlockSpec((1, 1, d), lambda i: (i // per_b, 0, 0))
    return pl.pallas_call(
        _dispatch_kernel,
        grid=(nblk,),
        in_specs=[
            pl.BlockSpec((1, 1, tm * k), lambda i: (i, 0, 0), memory_space=pltpu.SMEM),
            pl.BlockSpec((1, 1, npad), lambda i: (i, 0, 0), memory_space=pltpu.SMEM),
            pl.BlockSpec((tm, d), lambda i: (i, 0)),
            pl.BlockSpec((1, d), lambda i: (0, 0)),
            bvec, bvec,
        ],
        out_specs=pl.BlockSpec(memory_space=pl.ANY),
        out_shape=jax.ShapeDtypeStruct((n_rows, d), F32),
        scratch_shapes=[pltpu.VMEM((tm // SUBLANES, SUBLANES, d), F32), pltpu.VMEM((SUBLANES, d), F32),
                        pltpu.SemaphoreType.DMA],
        compiler_params=_params(("arbitrary",)),
    )(dest_tm, free_rows.reshape(nblk, 1, npad), x1, g2.reshape(1, d), sh2, sc2)


def _expert_kernel(meta_ref, x_hbm, wg_ref, wu_ref, wd_ref, y_hbm, wg_s, wu_s, wd_s, xbuf, ybuf,
                   xsem, ysem, ypend):
    e = pl.program_id(0)
    ne = pl.num_programs(0)
    chunk = xbuf.shape[1]
    nsub = chunk // MOE_SUB
    start = meta_ref[0, e]
    nrows = meta_ref[1, e]

    def sub(r):
        return pl.ds(pl.multiple_of(r, MOE_SUB), MOE_SUB)

    def x_piece(r0, slot, j):
        return pltpu.make_async_copy(x_hbm.at[sub(r0 + j * MOE_SUB)], xbuf.at[slot, sub(j * MOE_SUB)],
                                     xsem.at[slot])

    def y_piece(r0, slot, j):
        return pltpu.make_async_copy(ybuf.at[slot, sub(j * MOE_SUB)], y_hbm.at[sub(r0 + j * MOE_SUB)],
                                     ysem.at[slot])

    def pieces(rows_left):
        return jnp.clip(rows_left, 0, chunk) // MOE_SUB

    def x_start(r0, slot, n):
        def body(j, carry):
            x_piece(r0, slot, j).start()
            return carry

        lax.fori_loop(0, n, body, 0)

    def x_wait(slot, n):
        def body(j, carry):
            x_piece(0, slot, 0).wait()
            return carry

        lax.fori_loop(0, n, body, 0)

    def y_drain(slot):
        def body(j, carry):
            y_piece(0, slot, 0).wait()
            return carry

        lax.fori_loop(0, ypend[slot], body, 0)
        ypend[slot] = 0

    nx = xbuf.shape[0]
    ahead = nx - 1

    def prime(first_row, rows):
        for a in range(ahead):
            x_start(first_row + a * chunk, a, pieces(rows - a * chunk))

    @pl.when(e == 0)
    def _():
        ypend[0] = 0
        ypend[1] = 0
        prime(start, nrows)

    wg_s[...] = wg_ref[0, 0].astype(BF16)
    wu_s[...] = wu_ref[0, 0].astype(BF16)
    wd_s[...] = wd_ref[0, 0].astype(BF16)

    def compute(xslot, yslot, m):
        x = xbuf[xslot, pl.ds(0, m), :].astype(BF16)
        gate = jnp.dot(x, wg_s[...], preferred_element_type=F32)
        up = jnp.dot(x, wu_s[...], preferred_element_type=F32)
        hid = (_silu(gate) * up).astype(BF16)
        ybuf[yslot, pl.ds(0, m), :] = jnp.dot(hid, wd_s[...], preferred_element_type=F32)

    def run_chunk(c, m_static, n_here):
        xslot = lax.rem(c, nx)
        yslot = c & 1
        r0 = start + c * chunk
        x_wait(xslot, n_here)
        x_start(r0 + ahead * chunk, lax.rem(c + ahead, nx), pieces(nrows - (c + ahead) * chunk))
        y_drain(yslot)
        compute(xslot, yslot, m_static)
        for j in range(m_static // MOE_SUB):
            y_piece(r0, yslot, j).start()
        ypend[yslot] = m_static // MOE_SUB

    n_full = nrows // chunk

    def full_body(c, carry):
        run_chunk(c, chunk, nsub)
        return carry

    lax.fori_loop(0, n_full, full_body, 0)

    rem = nrows - n_full * chunk
    for m in range(MOE_SUB, chunk, MOE_SUB):
        @pl.when(rem == m)
        def _(m=m):
            run_chunk(n_full, m, m // MOE_SUB)

    @pl.when(e + 1 < ne)
    def _():
        prime(meta_ref[0, e + 1], meta_ref[1, e + 1])

    @pl.when(e + 1 == ne)
    def _():
        y_drain(0)
        y_drain(1)
        tail0 = start + nrows
        ntail = (y_hbm.shape[0] - tail0) // MOE_SUB
        ybuf[0, pl.ds(0, MOE_SUB), :] = jnp.zeros((MOE_SUB, ybuf.shape[2]), ybuf.dtype)

        def fill(j, carry):
            y_piece(tail0 + j * MOE_SUB, 0, 0).start()
            return carry

        lax.fori_loop(0, ntail, fill, 0)
        ypend[0] = ntail
        y_drain(0)


def _experts(meta, x_sorted, w_gate, w_up, w_down, layer, n_slots):
    d = x_sorted.shape[1]
    _, ne, _, f = w_gate.shape
    chunk = 512
    wspec = lambda a, b: pl.BlockSpec((1, 1, a, b), lambda e, m: (layer, e, 0, 0))
    grid_spec = pltpu.PrefetchScalarGridSpec(
        num_scalar_prefetch=1,
        grid=(ne,),
        in_specs=[pl.BlockSpec(memory_space=pl.ANY), wspec(d, f), wspec(d, f), wspec(f, d)],
        out_specs=pl.BlockSpec(memory_space=pl.ANY),
        scratch_shapes=[
            pltpu.VMEM((d, f), BF16), pltpu.VMEM((d, f), BF16), pltpu.VMEM((f, d), BF16),
            pltpu.VMEM((3, chunk, d), F32), pltpu.VMEM((2, chunk, d), F32),
            pltpu.SemaphoreType.DMA((3,)), pltpu.SemaphoreType.DMA((2,)),
            pltpu.SMEM((2,), I32),
        ],
    )
    return pl.pallas_call(
        _expert_kernel,
        grid_spec=grid_spec,
        out_shape=jax.ShapeDtypeStruct((n_slots, d), F32),
        compiler_params=_params(("arbitrary",)),
    )(meta, x_sorted, w_gate, w_up, w_down)


def _combine_kernel(dest_ref, wt_ref, x_ref, g2_ref, sh_ref, sc_ref, gt_ref, wg_ref, wu_ref, wd_ref, gf_ref,
                    y_hbm, o_ref, buf, sem, *, final_norm):
    k, ntile, sub, d = buf.shape
    tm = ntile * sub

    def issue(r8):
        for j in range(sub):
            for kk in range(k):
                slot = dest_ref[0, 0, (r8 * sub + j) * k + kk]
                pltpu.make_async_copy(y_hbm.at[pl.ds(slot, 1)], buf.at[kk, r8, pl.ds(j, 1)],
                                      sem).start(priority=kk % 2)

    groups = iter(range(ntile))
    x1 = x_ref[...]
    issue(next(groups))
    h = _norm_mod(x1, g2_ref[...], sh_ref[0], sc_ref[0]).astype(BF16)
    f = wg_ref.shape[1]
    hid = []
    for c in range(f // LANES):
        issue(next(groups))
        cols = slice(c * LANES, (c + 1) * LANES)
        gate = jnp.dot(h, wg_ref[:, cols], preferred_element_type=F32)
        up = jnp.dot(h, wu_ref[:, cols], preferred_element_type=F32)
        hid.append((_silu(gate) * up).astype(BF16))
    hid = jnp.concatenate(hid, axis=-1)
    acc = []
    wn = 2 * LANES
    for c in range(d // wn):
        issue(next(groups))
        acc.append(jnp.dot(hid, wd_ref[:, c * wn:(c + 1) * wn], preferred_element_type=F32))
    for r8 in groups:
        issue(r8)
    acc = jnp.concatenate(acc, axis=-1)

    for kk in range(k):
        pltpu.make_async_copy(buf.at[kk], buf.at[kk], sem).wait()
    wt = wt_ref[...]
    for kk in range(k):
        acc = acc + wt[:, kk:kk + 1] * buf[kk].reshape(tm, d)
    out = x1 + gt_ref[0] * acc
    if final_norm:
        ms = jnp.mean(out * out, axis=-1, keepdims=True)
        out = out * lax.rsqrt(ms + NORM_EPS) * gf_ref[...]
    o_ref[...] = out


def _combine(dest, wts, x1, g2, sh2, sc2, gt2, w_sg, w_su, w_sd, g_final, y_sorted, seq, final_norm):
    t, d = x1.shape
    k = dest.shape[0]
    tm = 128
    per_b = seq // tm
    nblk = t // tm
    dest_tm = dest.T.reshape(nblk, 1, tm * k)
    row = lambda i: (i, 0)
    vec = pl.BlockSpec((1, d), lambda i: (0, 0))
    bvec = pl.BlockSpec((1, 1, d), lambda i: (i // per_b, 0, 0))
    return pl.pallas_call(
        functools.partial(_combine_kernel, final_norm=final_norm),
        grid=(nblk,),
        in_specs=[
            pl.BlockSpec((1, 1, tm * k), lambda i: (i, 0, 0), memory_space=pltpu.SMEM),
            pl.BlockSpec((tm, k), row),
            pl.BlockSpec((tm, d), row),
            vec, bvec, bvec, bvec,
            _resident(w_sg.shape), _resident(w_su.shape), _resident(w_sd.shape),
            vec,
            pl.BlockSpec(memory_space=pl.ANY),
        ],
        out_specs=pl.BlockSpec((tm, d), row),
        out_shape=jax.ShapeDtypeStruct((t, d), F32),
        scratch_shapes=[pltpu.VMEM((k, tm // SUBLANES, SUBLANES, d), F32), pltpu.SemaphoreType.DMA],
        compiler_params=_params(("arbitrary",)),
    )(dest_tm, wts.T, x1, g2.reshape(1, d), sh2, sc2, gt2, w_sg.astype(BF16), w_su.astype(BF16),
      w_sd.astype(BF16), g_final.reshape(1, d), y_sorted)


def kernel(x, c, w_ada, b_ada, norm_mix, w_in, b_in, attn_sinks, w_attn_o, ssm_a_re, ssm_a_im, ssm_log_dt,
           ssm_b_re, ssm_b_im, ssm_c_re, ssm_c_im, ssm_d, w_glu, w_mix_out, norm_ffn, w_router, b_router,
           w_exp_gate, w_exp_up, w_exp_down, w_sh_gate, w_sh_up, w_sh_down, norm_final):
    batch, seq, d = x.shape
    depth = w_ada.shape[0]
    n_experts = w_router.shape[2]
    ssm_width = ssm_d.shape[1]
    ucol0 = ATTN_WIDTH + 2 * KV_WIDTH
    gacol = (ucol0 + ssm_width) // d
    gscol = gacol + 1
    n_slots = batch * seq * TOP_K + n_experts * MOE_SUB

    c_pad = jnp.zeros((SUBLANES, d), F32).at[:batch].set(c)
    mod = _ada_mod(c_pad, w_ada, b_ada)[:, :batch]
    mod = mod.reshape(depth, batch, 6, 1, d)

    w_in_bf16 = w_in.astype(BF16)
    x2 = x.reshape(batch * seq, d)
    for l in range(depth):
        sh1, sc1, gt1, sh2, sc2, gt2 = (mod[l, :, m] for m in range(6))
        proj = _inproj(x2, norm_mix[l], sh1, sc1, w_in_bf16, b_in, l, seq)
        a_gated = _attention(proj, attn_sinks[l], w_attn_o[l], batch, seq, d, gacol)
        ssm_mats = _ssm_params(ssm_a_re[l], ssm_a_im[l], ssm_log_dt[l], ssm_b_re[l], ssm_b_im[l],
                               ssm_c_re[l], ssm_c_im[l], ssm_d[l])
        y = _ssm(proj, ssm_mats, batch, seq, ucol0)
        x1, logits_t = _mixout(a_gated, y, proj, x2, gt1, w_glu[l], w_mix_out[l], norm_ffn[l],
                               sh2, sc2, w_router[l].T, seq, gscol)
        eidx, wts, rank, counts = _router(logits_t, b_router[l])
        meta, pstart, free_rows = _slot_plan(counts[:, 0])
        dest = _dest(pstart, eidx, rank)
        x_sorted = _dispatch(dest, free_rows, x1, norm_ffn[l], sh2, sc2, n_slots, seq)
        y_sorted = _experts(meta, x_sorted, w_exp_gate, w_exp_up, w_exp_down, l, n_slots)
        x2 = _combine(dest, wts, x1, norm_ffn[l], sh2, sc2, gt2, w_sh_gate[l], w_sh_up[l], w_sh_down[l],
                      norm_final, y_sorted, seq, final_norm=(l == depth - 1))
    return x2.reshape(batch, seq, d)
```

```python
import functools
import itertools
import math

import jax
import jax.numpy as jnp
from jax import lax
from jax.experimental import pallas as pl
from jax.experimental.pallas import tpu as pltpu

F32 = jnp.float32
BF16 = jnp.bfloat16
I32 = jnp.int32
U32 = jnp.uint32

N_Q_HEADS = 16
N_KV_HEADS = 4
HEAD_DIM = 64
Q_PER_KV = N_Q_HEADS // N_KV_HEADS
WINDOW = 128
ATTN_WIDTH = N_Q_HEADS * HEAD_DIM
KV_WIDTH = N_KV_HEADS * HEAD_DIM
SSM_GROUP_CH = 16
SSM_STATE = 64
N_EXPERT_GROUPS = 8
TOPK_GROUPS = 4
TOP_K = 8
ROUTED_SCALE = 2.5
NORM_EPS = 1e-6

LANES = 128
SUBLANES = 8
VMEM_LIMIT_BYTES = 56 * 1024 * 1024

SSM_GROUPS_PER_BLOCK = LANES // SSM_GROUP_CH
SSM_SEGMENTS = SUBLANES
MOE_SUB = 128
NEG_BIG = -1e30


def _sigmoid(x):
    return 1.0 / (1.0 + jnp.exp(-x))


def _silu(x):
    return x * _sigmoid(x)


def _pack_halves(x):
    n = x.shape[1] // 2
    return pltpu.pack_elementwise([x[:, :n], x[:, n:]], packed_dtype=BF16)


def _unpack_halves(w):
    return tuple(pltpu.unpack_elementwise(w, index=i, packed_dtype=BF16, unpacked_dtype=F32) for i in range(2))


def _params(sem):
    return pltpu.CompilerParams(dimension_semantics=sem, vmem_limit_bytes=VMEM_LIMIT_BYTES)


def _resident(shape):
    nd = len(shape)
    return pl.BlockSpec(shape, lambda *_: (0,) * nd, pipeline_mode=pl.Buffered(1))


def _ada_kernel(c_ref, w_ref, b_ref, o_ref):
    c_act = _silu(c_ref[...]).astype(BF16)
    o_ref[0] = jnp.dot(c_act, w_ref[0].astype(BF16), preferred_element_type=F32) + b_ref[0]


def _ada_mod(c_pad, w_ada, b_ada):
    depth, d, n = w_ada.shape
    tn = 1024
    return pl.pallas_call(
        _ada_kernel,
        grid=(depth, n // tn),
        in_specs=[
            pl.BlockSpec((SUBLANES, d), lambda l, j: (0, 0)),
            pl.BlockSpec((1, d, tn), lambda l, j: (l, 0, j)),
            pl.BlockSpec((1, 1, tn), lambda l, j: (l, 0, j)),
        ],
        out_specs=pl.BlockSpec((1, SUBLANES, tn), lambda l, j: (l, 0, j)),
        out_shape=jax.ShapeDtypeStruct((depth, SUBLANES, n), F32),
        compiler_params=_params(("arbitrary", "arbitrary")),
    )(c_pad, w_ada, b_ada.reshape(depth, 1, n))


def _norm_mod(x, g, shift, scale):
    ms = jnp.mean(x * x, axis=-1, keepdims=True)
    xn = x * lax.rsqrt(ms + NORM_EPS) * g
    return xn * (1.0 + scale) + shift


def _inproj_kernel(x_ref, g_ref, sh_ref, sc_ref, w_ref, b_ref, o_ref, h_scr):
    @pl.when(pl.program_id(1) == 0)
    def _():
        h_scr[...] = _norm_mod(x_ref[...], g_ref[...], sh_ref[0], sc_ref[0]).astype(BF16)

    acc = jnp.dot(h_scr[...], w_ref[0], preferred_element_type=F32)
    o_ref[...] = (acc + b_ref[0]).astype(o_ref.dtype)


def _inproj(x2, g, shift, scale, w_all, b_all, layer, seq):
    t, d = x2.shape
    depth, _, n = w_all.shape
    tm, tn = 1024, 1024
    per_b = seq // tm
    return pl.pallas_call(
        _inproj_kernel,
        grid=(t // tm, n // tn),
        in_specs=[
            pl.BlockSpec((tm, d), lambda i, j: (i, 0)),
            pl.BlockSpec((1, d), lambda i, j: (0, 0)),
            pl.BlockSpec((1, 1, d), lambda i, j: (i // per_b, 0, 0)),
            pl.BlockSpec((1, 1, d), lambda i, j: (i // per_b, 0, 0)),
            pl.BlockSpec((1, d, tn), lambda i, j: (layer, 0, j)),
            pl.BlockSpec((1, 1, tn), lambda i, j: (layer, 0, j)),
        ],
        out_specs=pl.BlockSpec((tm, tn), lambda i, j: (i, j)),
        out_shape=jax.ShapeDtypeStruct((t, n), BF16),
        scratch_shapes=[pltpu.VMEM((tm, d), BF16)],
        compiler_params=_params(("arbitrary", "arbitrary")),
    )(x2, g.reshape(1, d), shift, scale, w_all, b_all.reshape(depth, 1, n))


def _attn_kernel(sink_ref, q_ref, kc_ref, vc_ref, kp_ref, vp_ref, g_ref, wo_ref, o_ref):
    i = pl.program_id(1)
    q = q_ref[...]
    k_cat = jnp.concatenate([kp_ref[...], kc_ref[...]], axis=0)
    v_cat = jnp.concatenate([vp_ref[...], vc_ref[...]], axis=0)
    row = lax.broadcasted_iota(I32, (WINDOW, 2 * WINDOW), 0)
    col = lax.broadcasted_iota(I32, (WINDOW, 2 * WINDOW), 1)
    mask = (col > row) & (col <= row + WINDOW) & ((col >= WINDOW) | (i > 0))
    inv_sqrt = 1.0 / math.sqrt(HEAD_DIM)

    nk = 2 * WINDOW
    pair_w = 2 * HEAD_DIM
    zeros = jnp.zeros((nk, HEAD_DIM), k_cat.dtype)
    first_head = lax.broadcasted_iota(I32, (WINDOW, pair_w), 1) < HEAD_DIM

    def block_diag(x):
        return jnp.concatenate([jnp.concatenate([x, zeros], axis=1),
                                jnp.concatenate([zeros, x], axis=1)], axis=0)

    def softmax_terms(s, sink):
        s = jnp.where(mask, s * inv_sqrt, NEG_BIG)
        m = jnp.maximum(jnp.max(s, axis=-1, keepdims=True), sink)
        p = jnp.exp(s - m)
        return p.astype(BF16), jnp.sum(p, axis=-1, keepdims=True) + jnp.exp(sink - m)

    outs = []
    for j in range(N_KV_HEADS):
        k_bd = block_diag(k_cat[:, j * HEAD_DIM:(j + 1) * HEAD_DIM])
        v_bd = block_diag(v_cat[:, j * HEAD_DIM:(j + 1) * HEAD_DIM])
        for pair in range(Q_PER_KV // 2):
            h0 = j * Q_PER_KV + 2 * pair
            q2 = q[:, h0 * HEAD_DIM:(h0 + 2) * HEAD_DIM]
            s2 = lax.dot_general(q2, k_bd, (((1,), (1,)), ((), ())), preferred_element_type=F32)
            p_a, den_a = softmax_terms(s2[:, :nk], sink_ref[h0])
            p_b, den_b = softmax_terms(s2[:, nk:], sink_ref[h0 + 1])
            o2 = jnp.dot(jnp.concatenate([p_a, p_b], axis=1), v_bd, preferred_element_type=F32)
            outs.append(o2 / jnp.where(first_head, den_a, den_b))
    attn = jnp.concatenate(outs, axis=-1).astype(BF16)
    proj = jnp.dot(attn, wo_ref[...], preferred_element_type=F32)
    o_ref[...] = (_sigmoid(g_ref[...].astype(F32)) * proj).astype(o_ref.dtype)


def _attention(proj, sinks, w_o, batch, seq, d, gcol):
    t = proj.shape[0]
    nb = seq // WINDOW
    kcol = ATTN_WIDTH // KV_WIDTH
    vcol = kcol + 1
    cur = lambda b, i: b * nb + i
    prev = lambda b, i: b * nb + jnp.maximum(i - 1, 0)
    return pl.pallas_call(
        _attn_kernel,
        grid=(batch, nb),
        in_specs=[
            pl.BlockSpec(memory_space=pltpu.SMEM),
            pl.BlockSpec((WINDOW, ATTN_WIDTH), lambda b, i: (cur(b, i), 0)),
            pl.BlockSpec((WINDOW, KV_WIDTH), lambda b, i: (cur(b, i), kcol)),
            pl.BlockSpec((WINDOW, KV_WIDTH), lambda b, i: (cur(b, i), vcol)),
            pl.BlockSpec((WINDOW, KV_WIDTH), lambda b, i: (prev(b, i), kcol)),
            pl.BlockSpec((WINDOW, KV_WIDTH), lambda b, i: (prev(b, i), vcol)),
            pl.BlockSpec((WINDOW, d), lambda b, i: (cur(b, i), gcol)),
            _resident(w_o.shape),
        ],
        out_specs=pl.BlockSpec((WINDOW, d), lambda b, i: (cur(b, i), 0)),
        out_shape=jax.ShapeDtypeStruct((t, d), BF16),
        compiler_params=_params(("arbitrary", "arbitrary")),
    )(sinks, proj, proj, proj, proj, proj, proj, w_o.astype(BF16))


def _ssm_kernel(u_ref, ar_ref, ai_ref, br_ref, bi_ref, cr_ref, cin_ref, d_ref, y_ref,
                uperm, bur, bui, yperm):
    seq = u_ref.shape[0]
    nseg = SSM_SEGMENTS
    n = seq // nseg
    ns = bur.shape[1]

    for r in range(nseg):
        uperm[pl.ds(r, n, stride=nseg), :] = u_ref[r * n:(r + 1) * n, :].astype(F32)

    chunk = 512

    def bu_body(c, _):
        r0 = pl.multiple_of(c * chunk, chunk)
        up = uperm[pl.ds(r0, chunk), :].astype(BF16)
        bur[pl.ds(r0, chunk), :] = (jnp.dot(up, br_ref[0, 0], preferred_element_type=F32)
                                    + jnp.dot(up, br_ref[0, 1], preferred_element_type=F32))
        bui[pl.ds(r0, chunk), :] = (jnp.dot(up, bi_ref[0, 0], preferred_element_type=F32)
                                    + jnp.dot(up, bi_ref[0, 1], preferred_element_type=F32))
        return 0

    lax.fori_loop(0, seq // chunk, bu_body, 0)

    a_r1 = ar_ref[0]
    a_i1 = ai_ref[0]
    a_r = jnp.broadcast_to(a_r1, (nseg, ns))
    a_i = jnp.broadcast_to(a_i1, (nseg, ns))

    def scan_body(s, carry):
        xr, xi = carry
        r0 = pl.multiple_of(s * nseg, nseg)
        nxr = a_r * xr - a_i * xi + bur[pl.ds(r0, nseg), :]
        nxi = a_r * xi + a_i * xr + bui[pl.ds(r0, nseg), :]
        bur[pl.ds(r0, nseg), :] = nxr
        bui[pl.ds(r0, nseg), :] = nxi
        return nxr, nxi

    zeros = jnp.zeros((nseg, ns), F32)
    er, ei = lax.fori_loop(0, n, scan_body, (zeros, zeros), unroll=4)

    pr, pi = a_r1, a_i1
    for _ in range(n.bit_length() - 1):
        pr, pi = pr * pr - pi * pi, 2.0 * pr * pi

    cr_rows = [jnp.zeros((1, ns), F32)]
    ci_rows = [jnp.zeros((1, ns), F32)]
    for r in range(1, nseg):
        pcr, pci = cr_rows[-1], ci_rows[-1]
        cr_rows.append(er[r - 1:r, :] + pr * pcr - pi * pci)
        ci_rows.append(ei[r - 1:r, :] + pr * pci + pi * pcr)
    c_r = jnp.concatenate(cr_rows, axis=0)
    c_i = jnp.concatenate(ci_rows, axis=0)

    def fix_body(s, carry):
        qr, qi = carry
        r0 = pl.multiple_of(s * nseg, nseg)
        bur[pl.ds(r0, nseg), :] = bur[pl.ds(r0, nseg), :] + (qr * c_r - qi * c_i)
        bui[pl.ds(r0, nseg), :] = bui[pl.ds(r0, nseg), :] + (qr * c_i + qi * c_r)
        return qr * a_r - qi * a_i, qr * a_i + qi * a_r

    lax.fori_loop(0, n, fix_body, (a_r, a_i), unroll=4)

    def out_body(c, _):
        r0 = pl.multiple_of(c * chunk, chunk)
        y = jnp.dot(bur[pl.ds(r0, chunk), :].astype(BF16), cr_ref[0], preferred_element_type=F32)
        y = y + jnp.dot(bui[pl.ds(r0, chunk), :].astype(BF16), cin_ref[0], preferred_element_type=F32)
        y = y + d_ref[0] * uperm[pl.ds(r0, chunk), :]
        yperm[pl.ds(r0, chunk), :] = jax.nn.gelu(y, approximate=True)
        return 0

    lax.fori_loop(0, seq // chunk, out_body, 0)

    for r in range(nseg):
        y_ref[r * n:(r + 1) * n, :] = yperm[pl.ds(r, n, stride=nseg), :].astype(y_ref.dtype)


def _ssm_params(a_re, a_im, log_dt, b_re, b_im, c_re, c_im, d_skip):
    g, p = a_re.shape
    c = b_re.shape[-1]
    gpb = SSM_GROUPS_PER_BLOCK
    nblk = g // gpb
    dt = jnp.exp(log_dt)[:, None]
    mag = jnp.exp(dt * a_re)
    ab_r, ab_i = mag * jnp.cos(dt * a_im), mag * jnp.sin(dt * a_im)
    den = a_re * a_re + a_im * a_im
    nr, ni = ab_r - 1.0, ab_i
    f_r = (nr * a_re + ni * a_im) / den
    f_i = (ni * a_re - nr * a_im) / den
    bb_r = f_r[..., None] * b_re - f_i[..., None] * b_im
    bb_i = f_r[..., None] * b_im + f_i[..., None] * b_re
    eye = jnp.eye(gpb, dtype=F32)

    def in_mat(bb):
        return jnp.einsum("bgpc,gh->bgchp", bb.reshape(nblk, gpb, p, c), eye).reshape(nblk, gpb * c, gpb * p)

    def out_mat(cc):
        return jnp.einsum("bgcp,gh->bgphc", cc.reshape(nblk, gpb, c, p), eye).reshape(nblk, gpb * p, gpb * c)

    def hi_lo(m):
        hi = m.astype(BF16)
        return jnp.stack([hi, (m - hi.astype(F32)).astype(BF16)], axis=1)

    return (ab_r.reshape(nblk, 1, gpb * p), ab_i.reshape(nblk, 1, gpb * p),
            hi_lo(in_mat(bb_r)), hi_lo(in_mat(bb_i)), out_mat(c_re).astype(BF16), out_mat(-c_im).astype(BF16),
            d_skip.reshape(nblk, 1, gpb * c))


def _ssm(proj, ssm_mats, batch, seq, ucol0):
    ar, ai, br, bi, cr, cin, dsk = ssm_mats
    nblk = ar.shape[0]
    nu, ns = br.shape[2], br.shape[3]
    ublk0 = ucol0 // nu
    blk3 = lambda shape: pl.BlockSpec((1,) + shape, lambda b, g: (g,) + (0,) * len(shape))
    return pl.pallas_call(
        _ssm_kernel,
        grid=(batch, nblk),
        in_specs=[
            pl.BlockSpec((seq, nu), lambda b, g: (b, ublk0 + g)),
            blk3((1, ns)), blk3((1, ns)), blk3((2, nu, ns)), blk3((2, nu, ns)),
            blk3((ns, nu)), blk3((ns, nu)), blk3((1, nu)),
        ],
        out_specs=pl.BlockSpec((seq, nu), lambda b, g: (b, g)),
        out_shape=jax.ShapeDtypeStruct((batch * seq, nblk * nu), BF16),
        scratch_shapes=[
            pltpu.VMEM((seq, nu), F32), pltpu.VMEM((seq, ns), F32),
            pltpu.VMEM((seq, ns), F32), pltpu.VMEM((seq, nu), F32),
        ],
        compiler_params=_params(("arbitrary", "arbitrary")),
    )(proj, ar, ai, br, bi, cr, cin, dsk)


def _mixout_kernel(a_ref, y_ref, gs_ref, x_ref, gt_ref, wglu_ref, wmix_ref, g2_ref, sh_ref, sc_ref,
                   wr_ref, x1_ref, lg_ref):
    d = x_ref.shape[1]
    y = y_ref[...]
    r = None
    nsplit = 2
    w = d // nsplit
    for c in range(nsplit):
        cols = slice(c * w, (c + 1) * w)
        za = jnp.dot(y, wglu_ref[:, c * w:(c + 1) * w], preferred_element_type=F32)
        zb = jnp.dot(y, wglu_ref[:, d + c * w:d + (c + 1) * w], preferred_element_type=F32)
        mixed = a_ref[:, cols].astype(F32) + _sigmoid(gs_ref[:, cols].astype(F32)) * (za * _sigmoid(zb))
        part = jnp.dot(mixed.astype(BF16), wmix_ref[c * w:(c + 1) * w, :], preferred_element_type=F32)
        r = part if r is None else r + part
    x1 = x_ref[...] + gt_ref[0] * r
    x1_ref[...] = x1
    h2 = _norm_mod(x1, g2_ref[...], sh_ref[0], sc_ref[0])
    h_hi = h2.astype(BF16)
    h_lo = (h2 - h_hi.astype(F32)).astype(BF16)
    nt = lambda a, b: lax.dot_general(a, b, (((1,), (1,)), ((), ())), preferred_element_type=F32)
    lg_ref[...] = nt(wr_ref[0], h_hi) + nt(wr_ref[0], h_lo) + nt(wr_ref[1], h_hi)


def _mixout(a_gated, y, proj, x2, gt1, w_glu, w_mix, g2, sh2, sc2, w_router_t, seq, gscol):
    t, d = x2.shape
    tm = 512
    per_b = seq // tm
    ne = w_router_t.shape[0]
    wr_hi = w_router_t.astype(BF16)
    wr_split = jnp.stack([wr_hi, (w_router_t - wr_hi.astype(F32)).astype(BF16)])
    row = lambda i: (i, 0)
    bvec = pl.BlockSpec((1, 1, d), lambda i: (i // per_b, 0, 0))
    return pl.pallas_call(
        _mixout_kernel,
        grid=(t // tm,),
        in_specs=[
            pl.BlockSpec((tm, d), row),
            pl.BlockSpec((tm, y.shape[1]), row),
            pl.BlockSpec((tm, d), lambda i: (i, gscol)),
            pl.BlockSpec((tm, d), row),
            bvec,
            _resident(w_glu.shape),
            _resident(w_mix.shape),
            pl.BlockSpec((1, d), lambda i: (0, 0)),
            bvec, bvec,
            _resident(wr_split.shape),
        ],
        out_specs=[
            pl.BlockSpec((tm, d), row),
            pl.BlockSpec((ne, tm), lambda i: (0, i)),
        ],
        out_shape=[
            jax.ShapeDtypeStruct((t, d), F32),
            jax.ShapeDtypeStruct((ne, t), F32),
        ],
        compiler_params=_params(("arbitrary",)),
    )(a_gated, y, proj, x2, gt1, w_glu.astype(BF16), w_mix.astype(BF16), g2.reshape(1, d), sh2, sc2,
      wr_split)


def _router_kernel(lg_ref, b_ref, idx_ref, w_ref, rank_ref, cnt_ref, run_scr):
    ng, gs, tn = lg_ref.shape
    ne = ng * gs

    @pl.when(pl.program_id(0) == 0)
    def _():
        run_scr[...] = jnp.zeros(run_scr.shape, run_scr.dtype)

    scores = _sigmoid(lg_ref[...])
    sel = scores + b_ref[...]
    mem = lax.broadcasted_iota(I32, (ng, gs, tn), 1)
    grp = lax.broadcasted_iota(I32, (ng, gs, tn), 0)
    eid = grp * gs + mem
    ninf = -jnp.inf

    m1 = jnp.max(sel, axis=1, keepdims=True)
    first = jnp.min(jnp.where(sel == m1, mem, gs), axis=1, keepdims=True)
    m2 = jnp.max(jnp.where(mem == first, ninf, sel), axis=1, keepdims=True)
    gscore = m1 + m2

    gid = lax.broadcasted_iota(I32, (ng, 1, tn), 0)
    gmask = jnp.zeros((ng, 1, tn), jnp.bool_)
    for _ in range(TOPK_GROUPS):
        m = jnp.max(gscore, axis=0, keepdims=True)
        pick = gid == jnp.min(jnp.where(gscore == m, gid, ng), axis=0, keepdims=True)
        gmask = gmask | pick
        gscore = jnp.where(pick, ninf, gscore)

    def reduce_experts(v):
        return jnp.sum(jnp.sum(v, axis=1, keepdims=True), axis=0, keepdims=True)[0]

    cand = jnp.where(gmask, sel, ninf)
    idxs, wts, picks = [], [], []
    for _ in range(TOP_K):
        m = jnp.max(jnp.max(cand, axis=1, keepdims=True), axis=0, keepdims=True)
        e = jnp.where(cand == m, eid, ne)
        e = jnp.min(jnp.min(e, axis=1, keepdims=True), axis=0, keepdims=True)
        pick = eid == e
        idxs.append(e[0])
        wts.append(reduce_experts(jnp.where(pick, scores, 0.0)))
        picks.append(pick)
        cand = jnp.where(pick, ninf, cand)
    idx_ref[...] = jnp.concatenate(idxs, axis=0)
    w = jnp.concatenate(wts, axis=0)
    w_ref[...] = w / jnp.sum(w, axis=0, keepdims=True) * ROUTED_SCALE

    chosen = picks[0]
    for pick in picks[1:]:
        chosen = chosen | pick
    onehot = chosen.astype(F32).reshape(ne, tn)
    before = (lax.broadcasted_iota(I32, (tn, tn), 0) < lax.broadcasted_iota(I32, (tn, tn), 1)).astype(BF16)
    prefix = jnp.dot(onehot.astype(BF16), before, preferred_element_type=F32)
    rank = (run_scr[...] + prefix).reshape(ng, gs, tn)
    ranks = [reduce_experts(jnp.where(pick, rank, 0.0)) for pick in picks]
    rank_ref[...] = jnp.concatenate(ranks, axis=0).astype(I32)
    run_scr[...] = run_scr[...] + jnp.sum(onehot, axis=1, keepdims=True)
    cnt_ref[...] = run_scr[...].astype(I32)


def _router(logits_t, b_router):
    ne, t = logits_t.shape
    ng = N_EXPERT_GROUPS
    gs = ne // ng
    tn = 512
    tok = pl.BlockSpec((TOP_K, tn), lambda i: (0, i))
    return pl.pallas_call(
        _router_kernel,
        grid=(t // tn,),
        in_specs=[
            pl.BlockSpec((ng, gs, tn), lambda i: (0, 0, i)),
            pl.BlockSpec((ng, gs, 1), lambda i: (0, 0, 0)),
        ],
        out_specs=[tok, tok, tok, pl.BlockSpec((ne, 1), lambda i: (0, 0))],
        out_shape=[jax.ShapeDtypeStruct((TOP_K, t), I32), jax.ShapeDtypeStruct((TOP_K, t), F32),
                   jax.ShapeDtypeStruct((TOP_K, t), I32), jax.ShapeDtypeStruct((ne, 1), I32)],
        scratch_shapes=[pltpu.VMEM((ne, 1), F32)],
        compiler_params=_params(("arbitrary",)),
    )(logits_t.reshape(ng, gs, t), b_router.reshape(ng, gs, 1))


def _dest_kernel(pstart_ref, idx_ref, rank_ref, dest_ref):
    idx = idx_ref[...]
    base = jnp.zeros(idx.shape, I32)
    for e in range(pstart_ref.shape[0]):
        base = jnp.where(idx == e, pstart_ref[e], base)
    dest_ref[...] = base + rank_ref[...]


def _dest(pstart, eidx, rank):
    k, t = eidx.shape
    tn = 2048
    tok = pl.BlockSpec((k, tn), lambda i, ps: (0, i))
    return pl.pallas_call(
        _dest_kernel,
        grid_spec=pltpu.PrefetchScalarGridSpec(
            num_scalar_prefetch=1, grid=(t // tn,), in_specs=[tok, tok], out_specs=tok),
        out_shape=jax.ShapeDtypeStruct((k, t), I32),
        compiler_params=_params(("arbitrary",)),
    )(pstart, eidx, rank)


def _slot_plan(counts):
    padded = (counts + MOE_SUB - 1) // MOE_SUB * MOE_SUB
    pend = jnp.cumsum(padded)
    pstart = pend - padded
    npad = (padded - counts)[:, None]
    ntail = MOE_SUB - npad
    tail0 = pend[-1] + jnp.cumsum(ntail, axis=0) - ntail
    j = jnp.arange(MOE_SUB, dtype=I32)[None, :]
    free_rows = jnp.where(j < npad, (pstart + counts)[:, None] + j, tail0 + j - npad)
    return jnp.stack([pstart, padded]).astype(I32), pstart.astype(I32), free_rows.astype(I32)


def _dispatch_kernel(dest_ref, pad_ref, x_ref, g_ref, sh_ref, sc_ref, xs_hbm, h_scr, zero_scr, sem):
    _, ntile, sub, dw = h_scr.shape
    tm = ntile * sub
    k = dest_ref.shape[2] // tm
    npad = pad_ref.shape[2]
    assert npad % tm == 0
    i = pl.program_id(0)
    b = i & 1

    @pl.when(i == 0)
    def _():
        zero_scr[...] = _pack_halves(jnp.zeros((zero_scr.shape[0], 2 * dw), F32))

    h = _norm_mod(x_ref[...], g_ref[...], sh_ref[0], sc_ref[0])
    h_scr[b] = _pack_halves(h).reshape(ntile, sub, dw)

    def issue(r8, _):
        for j in range(sub):
            for kk in range(k):
                slot = dest_ref[0, 0, (r8 * sub + j) * k + kk]
                pltpu.make_async_copy(h_scr.at[b, r8, pl.ds(j, 1)], xs_hbm.at[pl.ds(slot, 1)],
                                      sem.at[b]).start(priority=kk % 2)
        return 0

    lax.fori_loop(0, ntile, issue, 0)

    def issue_pad(j, _):
        pltpu.make_async_copy(zero_scr.at[pl.ds(0, 1)], xs_hbm.at[pl.ds(pad_ref[0, 0, j], 1)], sem.at[b]).start()
        return 0

    lax.fori_loop(0, npad, issue_pad, 0, unroll=8)

    def drain(bb):
        for _ in range(k + npad // tm):
            pltpu.make_async_copy(xs_hbm.at[pl.ds(0, tm)], xs_hbm.at[pl.ds(0, tm)], sem.at[bb]).wait()

    @pl.when(i > 0)
    def _():
        drain(1 - b)

    @pl.when(i + 1 == pl.num_programs(0))
    def _():
        drain(b)


def _dispatch(dest, free_rows, x1, g2, sh2, sc2, n_rows, seq):
    t, d = x1.shape
    k = dest.shape[0]
    tm = 256
    per_b = seq // tm
    nblk = t // tm
    npad = free_rows.size // nblk
    dest_tm = dest.T.reshape(nblk, 1, tm * k)
    bvec = pl.BlockSpec((1, 1, d), lambda i: (i // per_b, 0, 0))
    return pl.pallas_call(
        _dispatch_kernel,
        grid=(nblk,),
        in_specs=[
            pl.BlockSpec((1, 1, tm * k), lambda i: (i, 0, 0), memory_space=pltpu.SMEM),
            pl.BlockSpec((1, 1, npad), lambda i: (i, 0, 0), memory_space=pltpu.SMEM),
            pl.BlockSpec((tm, d), lambda i: (i, 0)),
            pl.BlockSpec((1, d), lambda i: (0, 0)),
            bvec, bvec,
        ],
        out_specs=pl.BlockSpec(memory_space=pl.ANY),
        out_shape=jax.ShapeDtypeStruct((n_rows, d // 2), U32),
        scratch_shapes=[pltpu.VMEM((2, tm // SUBLANES, SUBLANES, d // 2), U32),
                        pltpu.VMEM((SUBLANES, d // 2), U32), pltpu.SemaphoreType.DMA((2,))],
        compiler_params=_params(("arbitrary",)),
    )(dest_tm, free_rows.reshape(nblk, 1, npad), x1, g2.reshape(1, d), sh2, sc2)


def _expert_kernel(meta_ref, x_hbm, wg_ref, wu_ref, wd_ref, y_hbm, xbuf, ybuf,
                   xsem, ysem, ypend):
    e = pl.program_id(0)
    ne = pl.num_programs(0)
    chunk = xbuf.shape[1]
    nsub = chunk // MOE_SUB
    start = meta_ref[0, e]
    nrows = meta_ref[1, e]

    def sub(r):
        return pl.ds(pl.multiple_of(r, MOE_SUB), MOE_SUB)

    def x_piece(r0, slot, j):
        return pltpu.make_async_copy(x_hbm.at[sub(r0 + j * MOE_SUB)], xbuf.at[slot, sub(j * MOE_SUB)],
                                     xsem.at[slot])

    def y_piece(r0, slot, j):
        return pltpu.make_async_copy(ybuf.at[slot, sub(j * MOE_SUB)], y_hbm.at[sub(r0 + j * MOE_SUB)],
                                     ysem.at[slot])

    def pieces(rows_left):
        return jnp.clip(rows_left, 0, chunk) // MOE_SUB

    def x_start(r0, slot, n):
        def body(j, carry):
            x_piece(r0, slot, j).start()
            return carry

        lax.fori_loop(0, n, body, 0)

    def x_wait(slot, n):
        def body(j, carry):
            x_piece(0, slot, 0).wait()
            return carry

        lax.fori_loop(0, n, body, 0)

    def y_drain(slot):
        def body(j, carry):
            y_piece(0, slot, 0).wait()
            return carry

        lax.fori_loop(0, ypend[slot], body, 0)
        ypend[slot] = 0

    nx = xbuf.shape[0]
    ahead = nx - 1

    def prime(first_row, rows):
        for a in range(ahead):
            x_start(first_row + a * chunk, a, pieces(rows - a * chunk))

    @pl.when(e == 0)
    def _():
        ypend[0] = 0
        ypend[1] = 0
        prime(start, nrows)

    dw = xbuf.shape[2]

    def compute(xslot, yslot, m):
        x_l, x_r = (v.astype(BF16) for v in _unpack_halves(xbuf[xslot, pl.ds(0, m), :]))

        def proj(w_ref):
            return (jnp.dot(x_l, w_ref[0, 0, :dw, :].astype(BF16), preferred_element_type=F32)
                    + jnp.dot(x_r, w_ref[0, 0, dw:, :].astype(BF16), preferred_element_type=F32))

        hid = (_silu(proj(wg_ref)) * proj(wu_ref)).astype(BF16)
        ybuf[yslot, pl.ds(0, m), :] = _pack_halves(
            jnp.dot(hid, wd_ref[0, 0].astype(BF16), preferred_element_type=F32))

    def run_chunk(c, m_static, n_here):
        xslot = lax.rem(c, nx)
        yslot = c & 1
        r0 = start + c * chunk
        x_wait(xslot, n_here)
        x_start(r0 + ahead * chunk, lax.rem(c + ahead, nx), pieces(nrows - (c + ahead) * chunk))
        y_drain(yslot)
        compute(xslot, yslot, m_static)
        for j in range(m_static // MOE_SUB):
            y_piece(r0, yslot, j).start()
        ypend[yslot] = m_static // MOE_SUB

    n_full = nrows // chunk

    def full_body(c, carry):
        run_chunk(c, chunk, nsub)
        return carry

    lax.fori_loop(0, n_full, full_body, 0)

    rem = nrows - n_full * chunk
    for m in range(MOE_SUB, chunk, MOE_SUB):
        @pl.when(rem == m)
        def _(m=m):
            run_chunk(n_full, m, m // MOE_SUB)

    @pl.when(e + 1 < ne)
    def _():
        prime(meta_ref[0, e + 1], meta_ref[1, e + 1])

    @pl.when(e + 1 == ne)
    def _():
        y_drain(0)
        y_drain(1)
        tail0 = start + nrows
        ntail = (y_hbm.shape[0] - tail0) // MOE_SUB
        ybuf[0, pl.ds(0, MOE_SUB), :] = _pack_halves(jnp.zeros((MOE_SUB, 2 * dw), F32))

        def fill(j, carry):
            y_piece(tail0 + j * MOE_SUB, 0, 0).start()
            return carry

        lax.fori_loop(0, ntail, fill, 0)
        ypend[0] = ntail
        y_drain(0)


def _experts(meta, x_sorted, w_gate, w_up, w_down, layer, n_slots):
    dw = x_sorted.shape[1]
    _, ne, d, f = w_gate.shape
    chunk = 512
    wspec = lambda a, b: pl.BlockSpec((1, 1, a, b), lambda e, m: (layer, e, 0, 0))
    grid_spec = pltpu.PrefetchScalarGridSpec(
        num_scalar_prefetch=1,
        grid=(ne,),
        in_specs=[pl.BlockSpec(memory_space=pl.ANY), wspec(d, f), wspec(d, f), wspec(f, d)],
        out_specs=pl.BlockSpec(memory_space=pl.ANY),
        scratch_shapes=[
            pltpu.VMEM((3, chunk, dw), U32), pltpu.VMEM((2, chunk, dw), U32),
            pltpu.SemaphoreType.DMA((3,)), pltpu.SemaphoreType.DMA((2,)),
            pltpu.SMEM((2,), I32),
        ],
    )
    return pl.pallas_call(
        _expert_kernel,
        grid_spec=grid_spec,
        out_shape=jax.ShapeDtypeStruct((n_slots, dw), U32),
        compiler_params=_params(("arbitrary",)),
    )(meta, x_sorted, w_gate, w_up, w_down)


def _combine_kernel(dest_ref, dnext_ref, wt_ref, x_ref, g2_ref, sh_ref, sc_ref, gt_ref, wg_ref, wu_ref, wd_ref,
                    gf_ref, y_hbm, o_ref, buf, sem, *, final_norm):
    _, k, ntile, sub, dw = buf.shape
    tm = ntile * sub
    d = x_ref.shape[1]
    i = pl.program_id(0)
    cur = i & 1
    nxt = 1 - cur
    has_next = i + 1 < pl.num_programs(0)

    def issue_group(idx_ref, b, r8):
        for j in range(sub):
            for kk in range(k):
                slot = idx_ref[0, 0, (r8 * sub + j) * k + kk]
                pltpu.make_async_copy(y_hbm.at[pl.ds(slot, 1)], buf.at[b, kk, r8, pl.ds(j, 1)],
                                      sem.at[b]).start(priority=kk % 2)

    @pl.when(i == 0)
    def _():
        def body(r8, carry):
            issue_group(dest_ref, 0, r8)
            return carry

        lax.fori_loop(0, ntile, body, 0)

    def issue(r8):
        issue_group(dnext_ref, nxt, r8)

    def drain(b):
        for kk in range(k):
            pltpu.make_async_copy(buf.at[b, kk], buf.at[b, kk], sem.at[b]).wait()

    drain(cur)
    groups = iter(range(ntile))
    x1 = x_ref[...]
    issue(next(groups))
    h = _norm_mod(x1, g2_ref[...], sh_ref[0], sc_ref[0]).astype(BF16)
    f = wg_ref.shape[1]
    hid = []
    for c in range(f // LANES):
        issue(next(groups))
        cols = slice(c * LANES, (c + 1) * LANES)
        gate = jnp.dot(h, wg_ref[:, cols], preferred_element_type=F32)
        up = jnp.dot(h, wu_ref[:, cols], preferred_element_type=F32)
        hid.append((_silu(gate) * up).astype(BF16))
    hid = jnp.concatenate(hid, axis=-1)
    acc = []
    wn = 2 * LANES
    for c in range(d // wn):
        issue(next(groups))
        acc.append(jnp.dot(hid, wd_ref[:, c * wn:(c + 1) * wn], preferred_element_type=F32))
    acc = jnp.concatenate(acc, axis=-1)

    wt = wt_ref[...]
    routed_l = routed_r = None
    for kk in range(k):
        for r8 in itertools.islice(groups, 1):
            issue(r8)
        y_l, y_r = _unpack_halves(buf[cur, kk].reshape(tm, dw))
        w_k = wt[:, kk:kk + 1]
        routed_l = w_k * y_l if routed_l is None else routed_l + w_k * y_l
        routed_r = w_k * y_r if routed_r is None else routed_r + w_k * y_r
    acc = acc + jnp.concatenate([routed_l, routed_r], axis=-1)

    @pl.when(jnp.logical_not(has_next))
    def _():
        drain(nxt)
    out = x1 + gt_ref[0] * acc
    if final_norm:
        ms = jnp.mean(out * out, axis=-1, keepdims=True)
        out = out * lax.rsqrt(ms + NORM_EPS) * gf_ref[...]
    o_ref[...] = out


def _combine(dest, wts, x1, g2, sh2, sc2, gt2, w_sg, w_su, w_sd, g_final, y_sorted, seq, final_norm):
    t, d = x1.shape
    k = dest.shape[0]
    tm = 128
    per_b = seq // tm
    nblk = t // tm
    dest_tm = dest.T.reshape(nblk, 1, tm * k)
    row = lambda i: (i, 0)
    vec = pl.BlockSpec((1, d), lambda i: (0, 0))
    bvec = pl.BlockSpec((1, 1, d), lambda i: (i // per_b, 0, 0))
    return pl.pallas_call(
        functools.partial(_combine_kernel, final_norm=final_norm),
        grid=(nblk,),
        in_specs=[
            pl.BlockSpec((1, 1, tm * k), lambda i: (i, 0, 0), memory_space=pltpu.SMEM),
            pl.BlockSpec((1, 1, tm * k), lambda i: (jnp.minimum(i + 1, nblk - 1), 0, 0),
                         memory_space=pltpu.SMEM),
            pl.BlockSpec((tm, k), row),
            pl.BlockSpec((tm, d), row),
            vec, bvec, bvec, bvec,
            _resident(w_sg.shape), _resident(w_su.shape), _resident(w_sd.shape),
            vec,
            pl.BlockSpec(memory_space=pl.ANY),
        ],
        out_specs=pl.BlockSpec((tm, d), row),
        out_shape=jax.ShapeDtypeStruct((t, d), F32),
        scratch_shapes=[pltpu.VMEM((2, k, tm // SUBLANES, SUBLANES, d // 2), U32),
                        pltpu.SemaphoreType.DMA((2,))],
        compiler_params=_params(("arbitrary",)),
    )(dest_tm, dest_tm, wts.T, x1, g2.reshape(1, d), sh2, sc2, gt2, w_sg.astype(BF16), w_su.astype(BF16),
      w_sd.astype(BF16), g_final.reshape(1, d), y_sorted)


def kernel(x, c, w_ada, b_ada, norm_mix, w_in, b_in, attn_sinks, w_attn_o, ssm_a_re, ssm_a_im, ssm_log_dt,
           ssm_b_re, ssm_b_im, ssm_c_re, ssm_c_im, ssm_d, w_glu, w_mix_out, norm_ffn, w_router, b_router,
           w_exp_gate, w_exp_up, w_exp_down, w_sh_gate, w_sh_up, w_sh_down, norm_final):
    batch, seq, d = x.shape
    depth = w_ada.shape[0]
    n_experts = w_router.shape[2]
    ssm_width = ssm_d.shape[1]
    ucol0 = ATTN_WIDTH + 2 * KV_WIDTH
    gacol = (ucol0 + ssm_width) // d
    gscol = gacol + 1
    n_slots = batch * seq * TOP_K + n_experts * MOE_SUB

    c_pad = jnp.zeros((SUBLANES, d), F32).at[:batch].set(c)
    mod = _ada_mod(c_pad, w_ada, b_ada)[:, :batch]
    mod = mod.reshape(depth, batch, 6, 1, d)

    w_in_bf16 = w_in.astype(BF16)
    x2 = x.reshape(batch * seq, d)
    for l in range(depth):
        sh1, sc1, gt1, sh2, sc2, gt2 = (mod[l, :, m] for m in range(6))
        proj = _inproj(x2, norm_mix[l], sh1, sc1, w_in_bf16, b_in, l, seq)
        a_gated = _attention(proj, attn_sinks[l], w_attn_o[l], batch, seq, d, gacol)
        ssm_mats = _ssm_params(ssm_a_re[l], ssm_a_im[l], ssm_log_dt[l], ssm_b_re[l], ssm_b_im[l],
                               ssm_c_re[l], ssm_c_im[l], ssm_d[l])
        y = _ssm(proj, ssm_mats, batch, seq, ucol0)
        x1, logits_t = _mixout(a_gated, y, proj, x2, gt1, w_glu[l], w_mix_out[l], norm_ffn[l],
                               sh2, sc2, w_router[l].T, seq, gscol)
        eidx, wts, rank, counts = _router(logits_t, b_router[l])
        meta, pstart, free_rows = _slot_plan(counts[:, 0])
        dest = _dest(pstart, eidx, rank)
        x_sorted = _dispatch(dest, free_rows, x1, norm_ffn[l], sh2, sc2, n_slots, seq)
        y_sorted = _experts(meta, x_sorted, w_exp_gate, w_exp_up, w_exp_down, l, n_slots)
        x2 = _combine(dest, wts, x1, norm_ffn[l], sh2, sc2, gt2, w_sh_gate[l], w_sh_up[l], w_sh_down[l],
                      norm_final, y_sorted, seq, final_norm=(l == depth - 1))
    return x2.reshape(batch, seq, d)
```

```python
import functools
import math

import jax
import jax.numpy as jnp
from jax import lax
from jax.experimental import pallas as pl
from jax.experimental.pallas import tpu as pltpu

F32 = jnp.float32
BF16 = jnp.bfloat16
I32 = jnp.int32
U32 = jnp.uint32

N_Q_HEADS = 16
N_KV_HEADS = 4
HEAD_DIM = 64
Q_PER_KV = N_Q_HEADS // N_KV_HEADS
WINDOW = 128
ATTN_WIDTH = N_Q_HEADS * HEAD_DIM
KV_WIDTH = N_KV_HEADS * HEAD_DIM
SSM_GROUP_CH = 16
SSM_STATE = 64
N_EXPERT_GROUPS = 8
TOPK_GROUPS = 4
TOP_K = 8
ROUTED_SCALE = 2.5
NORM_EPS = 1e-6

LANES = 128
SUBLANES = 8
VMEM_LIMIT_BYTES = 56 * 1024 * 1024

SSM_GROUPS_PER_BLOCK = LANES // SSM_GROUP_CH
SSM_SEGMENTS = SUBLANES
MOE_SUB = 128
NEG_BIG = -1e30


def _sigmoid(x):
    return 1.0 / (1.0 + jnp.exp(-x))


def _silu(x):
    return x * _sigmoid(x)


def _pack_halves(x):
    n = x.shape[1] // 2
    return pltpu.pack_elementwise([x[:, :n], x[:, n:]], packed_dtype=BF16)


def _unpack_halves(w):
    return tuple(pltpu.unpack_elementwise(w, index=i, packed_dtype=BF16, unpacked_dtype=F32) for i in range(2))


def _params(sem):
    return pltpu.CompilerParams(dimension_semantics=sem, vmem_limit_bytes=VMEM_LIMIT_BYTES)


def _resident(shape):
    nd = len(shape)
    return pl.BlockSpec(shape, lambda *_: (0,) * nd, pipeline_mode=pl.Buffered(1))


def _ada_kernel(c_ref, w_ref, b_ref, o_ref):
    c_act = _silu(c_ref[...]).astype(BF16)
    o_ref[0] = jnp.dot(c_act, w_ref[0].astype(BF16), preferred_element_type=F32) + b_ref[0]


def _ada_mod(c_pad, w_ada, b_ada):
    depth, d, n = w_ada.shape
    tn = 1024
    return pl.pallas_call(
        _ada_kernel,
        grid=(depth, n // tn),
        in_specs=[
            pl.BlockSpec((SUBLANES, d), lambda l, j: (0, 0)),
            pl.BlockSpec((1, d, tn), lambda l, j: (l, 0, j)),
            pl.BlockSpec((1, 1, tn), lambda l, j: (l, 0, j)),
        ],
        out_specs=pl.BlockSpec((1, SUBLANES, tn), lambda l, j: (l, 0, j)),
        out_shape=jax.ShapeDtypeStruct((depth, SUBLANES, n), F32),
        compiler_params=_params(("arbitrary", "arbitrary")),
    )(c_pad, w_ada, b_ada.reshape(depth, 1, n))


def _norm_mod(x, g, shift, scale):
    ms = jnp.mean(x * x, axis=-1, keepdims=True)
    xn = x * lax.rsqrt(ms + NORM_EPS) * g
    return xn * (1.0 + scale) + shift


def _inproj_kernel(x_ref, g_ref, sh_ref, sc_ref, w_ref, b_ref, o_ref, h_scr):
    @pl.when(pl.program_id(1) == 0)
    def _():
        h_scr[...] = _norm_mod(x_ref[...], g_ref[...], sh_ref[0], sc_ref[0]).astype(BF16)

    acc = jnp.dot(h_scr[...], w_ref[0], preferred_element_type=F32)
    o_ref[...] = (acc + b_ref[0]).astype(o_ref.dtype)


def _inproj(x2, g, shift, scale, w_all, b_all, layer, seq):
    t, d = x2.shape
    depth, _, n = w_all.shape
    tm, tn = 1024, 1024
    per_b = seq // tm
    return pl.pallas_call(
        _inproj_kernel,
        grid=(t // tm, n // tn),
        in_specs=[
            pl.BlockSpec((tm, d), lambda i, j: (i, 0)),
            pl.BlockSpec((1, d), lambda i, j: (0, 0)),
            pl.BlockSpec((1, 1, d), lambda i, j: (i // per_b, 0, 0)),
            pl.BlockSpec((1, 1, d), lambda i, j: (i // per_b, 0, 0)),
            pl.BlockSpec((1, d, tn), lambda i, j: (layer, 0, j)),
            pl.BlockSpec((1, 1, tn), lambda i, j: (layer, 0, j)),
        ],
        out_specs=pl.BlockSpec((tm, tn), lambda i, j: (i, j)),
        out_shape=jax.ShapeDtypeStruct((t, n), BF16),
        scratch_shapes=[pltpu.VMEM((tm, d), BF16)],
        compiler_params=_params(("arbitrary", "arbitrary")),
    )(x2, g.reshape(1, d), shift, scale, w_all, b_all.reshape(depth, 1, n))


def _attn_kernel(sink_ref, q_ref, kc_ref, vc_ref, kp_ref, vp_ref, g_ref, wo_ref, o_ref):
    i = pl.program_id(1)
    q = q_ref[...]
    k_cat = jnp.concatenate([kp_ref[...], kc_ref[...]], axis=0)
    v_cat = jnp.concatenate([vp_ref[...], vc_ref[...]], axis=0)
    row = lax.broadcasted_iota(I32, (WINDOW, 2 * WINDOW), 0)
    col = lax.broadcasted_iota(I32, (WINDOW, 2 * WINDOW), 1)
    mask = (col > row) & (col <= row + WINDOW) & ((col >= WINDOW) | (i > 0))
    inv_sqrt = 1.0 / math.sqrt(HEAD_DIM)

    nk = 2 * WINDOW
    pair_w = 2 * HEAD_DIM
    zeros = jnp.zeros((nk, HEAD_DIM), k_cat.dtype)
    first_head = lax.broadcasted_iota(I32, (WINDOW, pair_w), 1) < HEAD_DIM

    def block_diag(x):
        return jnp.concatenate([jnp.concatenate([x, zeros], axis=1),
                                jnp.concatenate([zeros, x], axis=1)], axis=0)

    def softmax_terms(s, sink):
        s = jnp.where(mask, s * inv_sqrt, NEG_BIG)
        m = jnp.maximum(jnp.max(s, axis=-1, keepdims=True), sink)
        p = jnp.exp(s - m)
        return p.astype(BF16), jnp.sum(p, axis=-1, keepdims=True) + jnp.exp(sink - m)

    outs = []
    for j in range(N_KV_HEADS):
        k_bd = block_diag(k_cat[:, j * HEAD_DIM:(j + 1) * HEAD_DIM])
        v_bd = block_diag(v_cat[:, j * HEAD_DIM:(j + 1) * HEAD_DIM])
        for pair in range(Q_PER_KV // 2):
            h0 = j * Q_PER_KV + 2 * pair
            q2 = q[:, h0 * HEAD_DIM:(h0 + 2) * HEAD_DIM]
            s2 = lax.dot_general(q2, k_bd, (((1,), (1,)), ((), ())), preferred_element_type=F32)
            p_a, den_a = softmax_terms(s2[:, :nk], sink_ref[h0])
            p_b, den_b = softmax_terms(s2[:, nk:], sink_ref[h0 + 1])
            o2 = jnp.dot(jnp.concatenate([p_a, p_b], axis=1), v_bd, preferred_element_type=F32)
            outs.append(o2 / jnp.where(first_head, den_a, den_b))
    attn = jnp.concatenate(outs, axis=-1).astype(BF16)
    proj = jnp.dot(attn, wo_ref[...], preferred_element_type=F32)
    o_ref[...] = (_sigmoid(g_ref[...].astype(F32)) * proj).astype(o_ref.dtype)


def _attention(proj, sinks, w_o, batch, seq, d, gcol):
    t = proj.shape[0]
    nb = seq // WINDOW
    kcol = ATTN_WIDTH // KV_WIDTH
    vcol = kcol + 1
    cur = lambda b, i: b * nb + i
    prev = lambda b, i: b * nb + jnp.maximum(i - 1, 0)
    return pl.pallas_call(
        _attn_kernel,
        grid=(batch, nb),
        in_specs=[
            pl.BlockSpec(memory_space=pltpu.SMEM),
            pl.BlockSpec((WINDOW, ATTN_WIDTH), lambda b, i: (cur(b, i), 0)),
            pl.BlockSpec((WINDOW, KV_WIDTH), lambda b, i: (cur(b, i), kcol)),
            pl.BlockSpec((WINDOW, KV_WIDTH), lambda b, i: (cur(b, i), vcol)),
            pl.BlockSpec((WINDOW, KV_WIDTH), lambda b, i: (prev(b, i), kcol)),
            pl.BlockSpec((WINDOW, KV_WIDTH), lambda b, i: (prev(b, i), vcol)),
            pl.BlockSpec((WINDOW, d), lambda b, i: (cur(b, i), gcol)),
            _resident(w_o.shape),
        ],
        out_specs=pl.BlockSpec((WINDOW, d), lambda b, i: (cur(b, i), 0)),
        out_shape=jax.ShapeDtypeStruct((t, d), BF16),
        compiler_params=_params(("arbitrary", "arbitrary")),
    )(sinks, proj, proj, proj, proj, proj, proj, w_o.astype(BF16))


def _ssm_kernel(u_ref, ar_ref, ai_ref, br_ref, bi_ref, cr_ref, cin_ref, d_ref, y_ref,
                uperm, bur, bui, yperm):
    seq = u_ref.shape[0]
    nseg = SSM_SEGMENTS
    n = seq // nseg
    ns = bur.shape[1]

    for r in range(nseg):
        uperm[pl.ds(r, n, stride=nseg), :] = u_ref[r * n:(r + 1) * n, :].astype(F32)

    chunk = 512

    def bu_body(c, _):
        r0 = pl.multiple_of(c * chunk, chunk)
        up = uperm[pl.ds(r0, chunk), :].astype(BF16)
        bur[pl.ds(r0, chunk), :] = (jnp.dot(up, br_ref[0, 0], preferred_element_type=F32)
                                    + jnp.dot(up, br_ref[0, 1], preferred_element_type=F32))
        bui[pl.ds(r0, chunk), :] = (jnp.dot(up, bi_ref[0, 0], preferred_element_type=F32)
                                    + jnp.dot(up, bi_ref[0, 1], preferred_element_type=F32))
        return 0

    lax.fori_loop(0, seq // chunk, bu_body, 0)

    a_r1 = ar_ref[0]
    a_i1 = ai_ref[0]
    a_r = jnp.broadcast_to(a_r1, (nseg, ns))
    a_i = jnp.broadcast_to(a_i1, (nseg, ns))

    def scan_body(s, carry):
        xr, xi = carry
        r0 = pl.multiple_of(s * nseg, nseg)
        nxr = a_r * xr - a_i * xi + bur[pl.ds(r0, nseg), :]
        nxi = a_r * xi + a_i * xr + bui[pl.ds(r0, nseg), :]
        bur[pl.ds(r0, nseg), :] = nxr
        bui[pl.ds(r0, nseg), :] = nxi
        return nxr, nxi

    zeros = jnp.zeros((nseg, ns), F32)
    er, ei = lax.fori_loop(0, n, scan_body, (zeros, zeros), unroll=4)

    pr, pi = a_r1, a_i1
    for _ in range(n.bit_length() - 1):
        pr, pi = pr * pr - pi * pi, 2.0 * pr * pi

    cr_rows = [jnp.zeros((1, ns), F32)]
    ci_rows = [jnp.zeros((1, ns), F32)]
    for r in range(1, nseg):
        pcr, pci = cr_rows[-1], ci_rows[-1]
        cr_rows.append(er[r - 1:r, :] + pr * pcr - pi * pci)
        ci_rows.append(ei[r - 1:r, :] + pr * pci + pi * pcr)
    c_r = jnp.concatenate(cr_rows, axis=0)
    c_i = jnp.concatenate(ci_rows, axis=0)

    def fix_body(s, carry):
        qr, qi = carry
        r0 = pl.multiple_of(s * nseg, nseg)
        bur[pl.ds(r0, nseg), :] = bur[pl.ds(r0, nseg), :] + (qr * c_r - qi * c_i)
        bui[pl.ds(r0, nseg), :] = bui[pl.ds(r0, nseg), :] + (qr * c_i + qi * c_r)
        return qr * a_r - qi * a_i, qr * a_i + qi * a_r

    lax.fori_loop(0, n, fix_body, (a_r, a_i), unroll=4)

    def out_body(c, _):
        r0 = pl.multiple_of(c * chunk, chunk)
        y = jnp.dot(bur[pl.ds(r0, chunk), :].astype(BF16), cr_ref[0], preferred_element_type=F32)
        y = y + jnp.dot(bui[pl.ds(r0, chunk), :].astype(BF16), cin_ref[0], preferred_element_type=F32)
        y = y + d_ref[0] * uperm[pl.ds(r0, chunk), :]
        yperm[pl.ds(r0, chunk), :] = jax.nn.gelu(y, approximate=True)
        return 0

    lax.fori_loop(0, seq // chunk, out_body, 0)

    for r in range(nseg):
        y_ref[r * n:(r + 1) * n, :] = yperm[pl.ds(r, n, stride=nseg), :].astype(y_ref.dtype)


def _ssm_params(a_re, a_im, log_dt, b_re, b_im, c_re, c_im, d_skip):
    g, p = a_re.shape
    c = b_re.shape[-1]
    gpb = SSM_GROUPS_PER_BLOCK
    nblk = g // gpb
    dt = jnp.exp(log_dt)[:, None]
    mag = jnp.exp(dt * a_re)
    ab_r, ab_i = mag * jnp.cos(dt * a_im), mag * jnp.sin(dt * a_im)
    den = a_re * a_re + a_im * a_im
    nr, ni = ab_r - 1.0, ab_i
    f_r = (nr * a_re + ni * a_im) / den
    f_i = (ni * a_re - nr * a_im) / den
    bb_r = f_r[..., None] * b_re - f_i[..., None] * b_im
    bb_i = f_r[..., None] * b_im + f_i[..., None] * b_re
    eye = jnp.eye(gpb, dtype=F32)

    def in_mat(bb):
        return jnp.einsum("bgpc,gh->bgchp", bb.reshape(nblk, gpb, p, c), eye).reshape(nblk, gpb * c, gpb * p)

    def out_mat(cc):
        return jnp.einsum("bgcp,gh->bgphc", cc.reshape(nblk, gpb, c, p), eye).reshape(nblk, gpb * p, gpb * c)

    def hi_lo(m):
        hi = m.astype(BF16)
        return jnp.stack([hi, (m - hi.astype(F32)).astype(BF16)], axis=1)

    return (ab_r.reshape(nblk, 1, gpb * p), ab_i.reshape(nblk, 1, gpb * p),
            hi_lo(in_mat(bb_r)), hi_lo(in_mat(bb_i)), out_mat(c_re).astype(BF16), out_mat(-c_im).astype(BF16),
            d_skip.reshape(nblk, 1, gpb * c))


def _ssm(proj, ssm_mats, batch, seq, ucol0):
    ar, ai, br, bi, cr, cin, dsk = ssm_mats
    nblk = ar.shape[0]
    nu, ns = br.shape[2], br.shape[3]
    ublk0 = ucol0 // nu
    blk3 = lambda shape: pl.BlockSpec((1,) + shape, lambda b, g: (g,) + (0,) * len(shape))
    return pl.pallas_call(
        _ssm_kernel,
        grid=(batch, nblk),
        in_specs=[
            pl.BlockSpec((seq, nu), lambda b, g: (b, ublk0 + g)),
            blk3((1, ns)), blk3((1, ns)), blk3((2, nu, ns)), blk3((2, nu, ns)),
            blk3((ns, nu)), blk3((ns, nu)), blk3((1, nu)),
        ],
        out_specs=pl.BlockSpec((seq, nu), lambda b, g: (b, g)),
        out_shape=jax.ShapeDtypeStruct((batch * seq, nblk * nu), BF16),
        scratch_shapes=[
            pltpu.VMEM((seq, nu), F32), pltpu.VMEM((seq, ns), F32),
            pltpu.VMEM((seq, ns), F32), pltpu.VMEM((seq, nu), F32),
        ],
        compiler_params=_params(("arbitrary", "arbitrary")),
    )(proj, ar, ai, br, bi, cr, cin, dsk)


def _mixout_kernel(a_ref, y_ref, gs_ref, x_ref, gt_ref, wglu_ref, wmix_ref, g2_ref, sh_ref, sc_ref,
                   wr_ref, x1_ref, lg_ref):
    d = x_ref.shape[1]
    y = y_ref[...]
    r = None
    nsplit = 2
    w = d // nsplit
    for c in range(nsplit):
        cols = slice(c * w, (c + 1) * w)
        za = jnp.dot(y, wglu_ref[:, c * w:(c + 1) * w], preferred_element_type=F32)
        zb = jnp.dot(y, wglu_ref[:, d + c * w:d + (c + 1) * w], preferred_element_type=F32)
        mixed = a_ref[:, cols].astype(F32) + _sigmoid(gs_ref[:, cols].astype(F32)) * (za * _sigmoid(zb))
        part = jnp.dot(mixed.astype(BF16), wmix_ref[c * w:(c + 1) * w, :], preferred_element_type=F32)
        r = part if r is None else r + part
    x1 = x_ref[...] + gt_ref[0] * r
    x1_ref[...] = x1
    h2 = _norm_mod(x1, g2_ref[...], sh_ref[0], sc_ref[0])
    h_hi = h2.astype(BF16)
    h_lo = (h2 - h_hi.astype(F32)).astype(BF16)
    nt = lambda a, b: lax.dot_general(a, b, (((1,), (1,)), ((), ())), preferred_element_type=F32)
    lg_ref[...] = nt(wr_ref[0], h_hi) + nt(wr_ref[0], h_lo) + nt(wr_ref[1], h_hi)


def _mixout(a_gated, y, proj, x2, gt1, w_glu, w_mix, g2, sh2, sc2, w_router_t, seq, gscol):
    t, d = x2.shape
    tm = 512
    per_b = seq // tm
    ne = w_router_t.shape[0]
    wr_hi = w_router_t.astype(BF16)
    wr_split = jnp.stack([wr_hi, (w_router_t - wr_hi.astype(F32)).astype(BF16)])
    row = lambda i: (i, 0)
    bvec = pl.BlockSpec((1, 1, d), lambda i: (i // per_b, 0, 0))
    return pl.pallas_call(
        _mixout_kernel,
        grid=(t // tm,),
        in_specs=[
            pl.BlockSpec((tm, d), row),
            pl.BlockSpec((tm, y.shape[1]), row),
            pl.BlockSpec((tm, d), lambda i: (i, gscol)),
            pl.BlockSpec((tm, d), row),
            bvec,
            _resident(w_glu.shape),
            _resident(w_mix.shape),
            pl.BlockSpec((1, d), lambda i: (0, 0)),
            bvec, bvec,
            _resident(wr_split.shape),
        ],
        out_specs=[
            pl.BlockSpec((tm, d), row),
            pl.BlockSpec((ne, tm), lambda i: (0, i)),
        ],
        out_shape=[
            jax.ShapeDtypeStruct((t, d), F32),
            jax.ShapeDtypeStruct((ne, t), F32),
        ],
        compiler_params=_params(("arbitrary",)),
    )(a_gated, y, proj, x2, gt1, w_glu.astype(BF16), w_mix.astype(BF16), g2.reshape(1, d), sh2, sc2,
      wr_split)


def _router_kernel(lg_ref, b_ref, idx_ref, w_ref, rank_ref, cnt_ref, run_scr):
    ng, gs, tn = lg_ref.shape
    ne = ng * gs

    @pl.when(pl.program_id(0) == 0)
    def _():
        run_scr[...] = jnp.zeros(run_scr.shape, run_scr.dtype)

    scores = _sigmoid(lg_ref[...])
    sel = scores + b_ref[...]
    mem = lax.broadcasted_iota(I32, (ng, gs, tn), 1)
    grp = lax.broadcasted_iota(I32, (ng, gs, tn), 0)
    eid = grp * gs + mem
    ninf = -jnp.inf

    m1 = jnp.max(sel, axis=1, keepdims=True)
    first = jnp.min(jnp.where(sel == m1, mem, gs), axis=1, keepdims=True)
    m2 = jnp.max(jnp.where(mem == first, ninf, sel), axis=1, keepdims=True)
    gscore = m1 + m2

    gid = lax.broadcasted_iota(I32, (ng, 1, tn), 0)
    gmask = jnp.zeros((ng, 1, tn), jnp.bool_)
    for _ in range(TOPK_GROUPS):
        m = jnp.max(gscore, axis=0, keepdims=True)
        pick = gid == jnp.min(jnp.where(gscore == m, gid, ng), axis=0, keepdims=True)
        gmask = gmask | pick
        gscore = jnp.where(pick, ninf, gscore)

    def reduce_experts(v):
        return jnp.sum(jnp.sum(v, axis=1, keepdims=True), axis=0, keepdims=True)[0]

    cand = jnp.where(gmask, sel, ninf)
    idxs, wts, picks = [], [], []
    for _ in range(TOP_K):
        m = jnp.max(jnp.max(cand, axis=1, keepdims=True), axis=0, keepdims=True)
        e = jnp.where(cand == m, eid, ne)
        e = jnp.min(jnp.min(e, axis=1, keepdims=True), axis=0, keepdims=True)
        pick = eid == e
        idxs.append(e[0])
        wts.append(reduce_experts(jnp.where(pick, scores, 0.0)))
        picks.append(pick)
        cand = jnp.where(pick, ninf, cand)
    idx_ref[...] = jnp.concatenate(idxs, axis=0)
    w = jnp.concatenate(wts, axis=0)
    w_ref[...] = w / jnp.sum(w, axis=0, keepdims=True) * ROUTED_SCALE

    chosen = picks[0]
    for pick in picks[1:]:
        chosen = chosen | pick
    onehot = chosen.astype(F32).reshape(ne, tn)
    before = (lax.broadcasted_iota(I32, (tn, tn), 0) < lax.broadcasted_iota(I32, (tn, tn), 1)).astype(BF16)
    prefix = jnp.dot(onehot.astype(BF16), before, preferred_element_type=F32)
    rank = (run_scr[...] + prefix).reshape(ng, gs, tn)
    ranks = [reduce_experts(jnp.where(pick, rank, 0.0)) for pick in picks]
    rank_ref[...] = jnp.concatenate(ranks, axis=0).astype(I32)
    run_scr[...] = run_scr[...] + jnp.sum(onehot, axis=1, keepdims=True)
    cnt_ref[...] = run_scr[...].astype(I32)


def _router(logits_t, b_router):
    ne, t = logits_t.shape
    ng = N_EXPERT_GROUPS
    gs = ne // ng
    tn = 512
    tok = pl.BlockSpec((TOP_K, tn), lambda i: (0, i))
    return pl.pallas_call(
        _router_kernel,
        grid=(t // tn,),
        in_specs=[
            pl.BlockSpec((ng, gs, tn), lambda i: (0, 0, i)),
            pl.BlockSpec((ng, gs, 1), lambda i: (0, 0, 0)),
        ],
        out_specs=[tok, tok, tok, pl.BlockSpec((ne, 1), lambda i: (0, 0))],
        out_shape=[jax.ShapeDtypeStruct((TOP_K, t), I32), jax.ShapeDtypeStruct((TOP_K, t), F32),
                   jax.ShapeDtypeStruct((TOP_K, t), I32), jax.ShapeDtypeStruct((ne, 1), I32)],
        scratch_shapes=[pltpu.VMEM((ne, 1), F32)],
        compiler_params=_params(("arbitrary",)),
    )(logits_t.reshape(ng, gs, t), b_router.reshape(ng, gs, 1))


def _dest_kernel(pstart_ref, idx_ref, rank_ref, dest_ref):
    idx = idx_ref[...]
    base = jnp.zeros(idx.shape, I32)
    for e in range(pstart_ref.shape[0]):
        base = jnp.where(idx == e, pstart_ref[e], base)
    dest_ref[...] = base + rank_ref[...]


def _dest(pstart, eidx, rank):
    k, t = eidx.shape
    tn = 2048
    tok = pl.BlockSpec((k, tn), lambda i, ps: (0, i))
    return pl.pallas_call(
        _dest_kernel,
        grid_spec=pltpu.PrefetchScalarGridSpec(
            num_scalar_prefetch=1, grid=(t // tn,), in_specs=[tok, tok], out_specs=tok),
        out_shape=jax.ShapeDtypeStruct((k, t), I32),
        compiler_params=_params(("arbitrary",)),
    )(pstart, eidx, rank)


def _slot_plan(counts):
    padded = (counts + MOE_SUB - 1) // MOE_SUB * MOE_SUB
    pend = jnp.cumsum(padded)
    pstart = pend - padded
    npad = (padded - counts)[:, None]
    ntail = MOE_SUB - npad
    tail0 = pend[-1] + jnp.cumsum(ntail, axis=0) - ntail
    j = jnp.arange(MOE_SUB, dtype=I32)[None, :]
    free_rows = jnp.where(j < npad, (pstart + counts)[:, None] + j, tail0 + j - npad)
    return jnp.stack([pstart, padded]).astype(I32), pstart.astype(I32), free_rows.astype(I32)


def _dispatch_kernel(dest_ref, pad_ref, x_ref, g_ref, sh_ref, sc_ref, xs_hbm, h_scr, zero_scr, sem):
    _, ntile, sub, dw = h_scr.shape
    tm = ntile * sub
    k = dest_ref.shape[2] // tm
    npad = pad_ref.shape[2]
    assert npad % tm == 0
    i = pl.program_id(0)
    b = i & 1

    @pl.when(i == 0)
    def _():
        zero_scr[...] = _pack_halves(jnp.zeros((zero_scr.shape[0], 2 * dw), F32))

    h = _norm_mod(x_ref[...], g_ref[...], sh_ref[0], sc_ref[0])
    h_scr[b] = _pack_halves(h).reshape(ntile, sub, dw)

    def issue(r8, _):
        for j in range(sub):
            for kk in range(k):
                slot = dest_ref[0, 0, (r8 * sub + j) * k + kk]
                pltpu.make_async_copy(h_scr.at[b, r8, pl.ds(j, 1)], xs_hbm.at[pl.ds(slot, 1)],
                                      sem.at[b]).start(priority=kk % 2)
        return 0

    lax.fori_loop(0, ntile, issue, 0)

    def issue_pad(j, _):
        pltpu.make_async_copy(zero_scr.at[pl.ds(0, 1)], xs_hbm.at[pl.ds(pad_ref[0, 0, j], 1)], sem.at[b]).start()
        return 0

    lax.fori_loop(0, npad, issue_pad, 0, unroll=8)

    def drain(bb):
        for _ in range(k + npad // tm):
            pltpu.make_async_copy(xs_hbm.at[pl.ds(0, tm)], xs_hbm.at[pl.ds(0, tm)], sem.at[bb]).wait()

    @pl.when(i > 0)
    def _():
        drain(1 - b)

    @pl.when(i + 1 == pl.num_programs(0))
    def _():
        drain(b)


def _dispatch(dest, free_rows, x1, g2, sh2, sc2, n_rows, seq):
    t, d = x1.shape
    k = dest.shape[0]
    tm = 256
    per_b = seq // tm
    nblk = t // tm
    npad = free_rows.size // nblk
    dest_tm = dest.T.reshape(nblk, 1, tm * k)
    bvec = pl.BlockSpec((1, 1, d), lambda i: (i // per_b, 0, 0))
    return pl.pallas_call(
        _dispatch_kernel,
        grid=(nblk,),
        in_specs=[
            pl.BlockSpec((1, 1, tm * k), lambda i: (i, 0, 0), memory_space=pltpu.SMEM),
            pl.BlockSpec((1, 1, npad), lambda i: (i, 0, 0), memory_space=pltpu.SMEM),
            pl.BlockSpec((tm, d), lambda i: (i, 0)),
            pl.BlockSpec((1, d), lambda i: (0, 0)),
            bvec, bvec,
        ],
        out_specs=pl.BlockSpec(memory_space=pl.ANY),
        out_shape=jax.ShapeDtypeStruct((n_rows, d // 2), U32),
        scratch_shapes=[pltpu.VMEM((2, tm // SUBLANES, SUBLANES, d // 2), U32),
                        pltpu.VMEM((SUBLANES, d // 2), U32), pltpu.SemaphoreType.DMA((2,))],
        compiler_params=_params(("arbitrary",)),
    )(dest_tm, free_rows.reshape(nblk, 1, npad), x1, g2.reshape(1, d), sh2, sc2)


def _chunk_plan(meta, chunk, n_chunks):
    pstart, padded = meta[0], meta[1]
    ne = pstart.shape[0]
    nch = (padded + chunk - 1) // chunk
    cend = jnp.cumsum(nch)
    cfirst = cend - nch
    j = jnp.arange(n_chunks, dtype=I32)
    e_of = jnp.sum((cend[None, :] <= j[:, None]).astype(I32), axis=1)
    valid = e_of < ne
    e_c = jnp.minimum(e_of, ne - 1)
    local = j - cfirst[e_c]
    rows = jnp.clip(padded[e_c] - local * chunk, 0, chunk)
    cmeta = jnp.stack([jnp.where(valid, pstart[e_c] + local * chunk, 0), jnp.where(valid, rows // MOE_SUB, 0)])
    return cmeta.astype(I32), jnp.stack([cfirst, nch]).astype(I32)


def _expert_kernel(meta_ref, cmeta_ref, emeta_ref, x_hbm, wg_ref, wu_ref, wd_ref, y_hbm, xbuf, ybuf,
                   xsem, ysem, ypend):
    e = pl.program_id(0)
    ne = pl.num_programs(0)
    chunk = xbuf.shape[1]
    nsub = chunk // MOE_SUB

    def sub(r):
        return pl.ds(pl.multiple_of(r, MOE_SUB), MOE_SUB)

    def x_piece(r0, slot, j):
        return pltpu.make_async_copy(x_hbm.at[sub(r0 + j * MOE_SUB)], xbuf.at[slot, sub(j * MOE_SUB)],
                                     xsem.at[slot])

    def y_piece(r0, slot, j):
        return pltpu.make_async_copy(ybuf.at[slot, sub(j * MOE_SUB)], y_hbm.at[sub(r0 + j * MOE_SUB)],
                                     ysem.at[slot])

    def x_start(r0, slot, n):
        def body(j, carry):
            x_piece(r0, slot, j).start()
            return carry

        lax.fori_loop(0, n, body, 0)

    def x_wait(slot, n):
        def body(j, carry):
            x_piece(0, slot, 0).wait()
            return carry

        lax.fori_loop(0, n, body, 0)

    def y_drain(slot):
        def body(j, carry):
            y_piece(0, slot, 0).wait()
            return carry

        lax.fori_loop(0, ypend[slot], body, 0)
        ypend[slot] = 0

    nx = xbuf.shape[0]
    ahead = nx - 1

    def request(g):
        x_start(cmeta_ref[0, g], lax.rem(g, nx), cmeta_ref[1, g])

    @pl.when(e == 0)
    def _():
        ypend[0] = 0
        ypend[1] = 0
        for a in range(ahead):
            request(a)

    dw = xbuf.shape[2]

    def compute(xslot, yslot, m):
        x_l, x_r = (v.astype(BF16) for v in _unpack_halves(xbuf[xslot, pl.ds(0, m), :]))

        def proj(w_ref):
            return (jnp.dot(x_l, w_ref[0, 0, :dw, :].astype(BF16), preferred_element_type=F32)
                    + jnp.dot(x_r, w_ref[0, 0, dw:, :].astype(BF16), preferred_element_type=F32))

        hid = (_silu(proj(wg_ref)) * proj(wu_ref)).astype(BF16)
        ybuf[yslot, pl.ds(0, m), :] = _pack_halves(
            jnp.dot(hid, wd_ref[0, 0].astype(BF16), preferred_element_type=F32))

    def run_chunk(g, npieces):
        xslot = lax.rem(g, nx)
        yslot = g & 1
        r0 = cmeta_ref[0, g]
        x_wait(xslot, npieces)
        request(g + ahead)
        y_drain(yslot)
        compute(xslot, yslot, npieces * MOE_SUB)
        for j in range(npieces):
            y_piece(r0, yslot, j).start()
        ypend[yslot] = npieces

    def chunk_body(c, carry):
        g = emeta_ref[0, e] + c
        for npieces in range(1, nsub + 1):
            @pl.when(cmeta_ref[1, g] == npieces)
            def _(npieces=npieces):
                run_chunk(g, npieces)
        return carry

    lax.fori_loop(0, emeta_ref[1, e], chunk_body, 0)

    @pl.when(e + 1 == ne)
    def _():
        y_drain(0)
        y_drain(1)
        tail0 = meta_ref[0, e] + meta_ref[1, e]
        ntail = (y_hbm.shape[0] - tail0) // MOE_SUB
        ybuf[0, pl.ds(0, MOE_SUB), :] = _pack_halves(jnp.zeros((MOE_SUB, 2 * dw), F32))

        def fill(j, carry):
            y_piece(tail0 + j * MOE_SUB, 0, 0).start()
            return carry

        lax.fori_loop(0, ntail, fill, 0)
        ypend[0] = ntail
        y_drain(0)


def _experts(meta, x_sorted, w_gate, w_up, w_down, layer, n_slots):
    dw = x_sorted.shape[1]
    _, ne, d, f = w_gate.shape
    chunk = 512
    n_chunks = n_slots // chunk + ne + 3
    cmeta, emeta = _chunk_plan(meta, chunk, n_chunks)
    wspec = lambda a, b: pl.BlockSpec((1, 1, a, b), lambda e, m, cm, em: (layer, e, 0, 0))
    grid_spec = pltpu.PrefetchScalarGridSpec(
        num_scalar_prefetch=3,
        grid=(ne,),
        in_specs=[pl.BlockSpec(memory_space=pl.ANY), wspec(d, f), wspec(d, f), wspec(f, d)],
        out_specs=pl.BlockSpec(memory_space=pl.ANY),
        scratch_shapes=[
            pltpu.VMEM((3, chunk, dw), U32), pltpu.VMEM((2, chunk, dw), U32),
            pltpu.SemaphoreType.DMA((3,)), pltpu.SemaphoreType.DMA((2,)),
            pltpu.SMEM((2,), I32),
        ],
    )
    return pl.pallas_call(
        _expert_kernel,
        grid_spec=grid_spec,
        out_shape=jax.ShapeDtypeStruct((n_slots, dw), U32),
        compiler_params=_params(("arbitrary",)),
    )(meta, cmeta, emeta, x_sorted, w_gate, w_up, w_down)


def _combine_kernel(dest_ref, dnext_ref, wt_ref, x_ref, g2_ref, sh_ref, sc_ref, gt_ref, wg_ref, wu_ref, wd_ref,
                    gf_ref, y_hbm, o_ref, buf, sem, *, final_norm):
    _, k, ntile, sub, dw = buf.shape
    tm = ntile * sub
    d = x_ref.shape[1]
    i = pl.program_id(0)
    cur = i & 1
    nxt = 1 - cur
    has_next = i + 1 < pl.num_programs(0)

    def issue_group(idx_ref, b, r8):
        for j in range(sub):
            for kk in range(k):
                slot = idx_ref[0, 0, (r8 * sub + j) * k + kk]
                pltpu.make_async_copy(y_hbm.at[pl.ds(slot, 1)], buf.at[b, kk, r8, pl.ds(j, 1)],
                                      sem.at[b]).start(priority=kk % 2)

    @pl.when(i == 0)
    def _():
        def body(r8, carry):
            issue_group(dest_ref, 0, r8)
            return carry

        lax.fori_loop(0, ntile, body, 0)

    def issue(r8):
        issue_group(dnext_ref, nxt, r8)

    def drain(b):
        for kk in range(k):
            pltpu.make_async_copy(buf.at[b, kk], buf.at[b, kk], sem.at[b]).wait()

    groups = iter(range(ntile))
    x1 = x_ref[...]
    issue(next(groups))
    h = _norm_mod(x1, g2_ref[...], sh_ref[0], sc_ref[0]).astype(BF16)
    f = wg_ref.shape[1]
    hid = []
    for c in range(f // LANES):
        issue(next(groups))
        cols = slice(c * LANES, (c + 1) * LANES)
        gate = jnp.dot(h, wg_ref[:, cols], preferred_element_type=F32)
        up = jnp.dot(h, wu_ref[:, cols], preferred_element_type=F32)
        hid.append((_silu(gate) * up).astype(BF16))
    hid = jnp.concatenate(hid, axis=-1)
    acc = []
    wn = 2 * LANES
    for c in range(d // wn):
        issue(next(groups))
        acc.append(jnp.dot(hid, wd_ref[:, c * wn:(c + 1) * wn], preferred_element_type=F32))
    for r8 in groups:
        issue(r8)
    acc = jnp.concatenate(acc, axis=-1)

    drain(cur)
    wt = wt_ref[...]
    routed_l = routed_r = None
    for kk in range(k):
        y_l, y_r = _unpack_halves(buf[cur, kk].reshape(tm, dw))
        w_k = wt[:, kk:kk + 1]
        routed_l = w_k * y_l if routed_l is None else routed_l + w_k * y_l
        routed_r = w_k * y_r if routed_r is None else routed_r + w_k * y_r
    acc = acc + jnp.concatenate([routed_l, routed_r], axis=-1)

    @pl.when(jnp.logical_not(has_next))
    def _():
        drain(nxt)
    out = x1 + gt_ref[0] * acc
    if final_norm:
        ms = jnp.mean(out * out, axis=-1, keepdims=True)
        out = out * lax.rsqrt(ms + NORM_EPS) * gf_ref[...]
    o_ref[...] = out


def _combine(dest, wts, x1, g2, sh2, sc2, gt2, w_sg, w_su, w_sd, g_final, y_sorted, seq, final_norm):
    t, d = x1.shape
    k = dest.shape[0]
    tm = 128
    per_b = seq // tm
    nblk = t // tm
    dest_tm = dest.T.reshape(nblk, 1, tm * k)
    row = lambda i: (i, 0)
    vec = pl.BlockSpec((1, d), lambda i: (0, 0))
    bvec = pl.BlockSpec((1, 1, d), lambda i: (i // per_b, 0, 0))
    return pl.pallas_call(
        functools.partial(_combine_kernel, final_norm=final_norm),
        grid=(nblk,),
        in_specs=[
            pl.BlockSpec((1, 1, tm * k), lambda i: (i, 0, 0), memory_space=pltpu.SMEM),
            pl.BlockSpec((1, 1, tm * k), lambda i: (jnp.minimum(i + 1, nblk - 1), 0, 0),
                         memory_space=pltpu.SMEM),
            pl.BlockSpec((tm, k), row),
            pl.BlockSpec((tm, d), row),
            vec, bvec, bvec, bvec,
            _resident(w_sg.shape), _resident(w_su.shape), _resident(w_sd.shape),
            vec,
            pl.BlockSpec(memory_space=pl.ANY),
        ],
        out_specs=pl.BlockSpec((tm, d), row),
        out_shape=jax.ShapeDtypeStruct((t, d), F32),
        scratch_shapes=[pltpu.VMEM((2, k, tm // SUBLANES, SUBLANES, d // 2), U32),
                        pltpu.SemaphoreType.DMA((2,))],
        compiler_params=_params(("arbitrary",)),
    )(dest_tm, dest_tm, wts.T, x1, g2.reshape(1, d), sh2, sc2, gt2, w_sg.astype(BF16), w_su.astype(BF16),
      w_sd.astype(BF16), g_final.reshape(1, d), y_sorted)


def kernel(x, c, w_ada, b_ada, norm_mix, w_in, b_in, attn_sinks, w_attn_o, ssm_a_re, ssm_a_im, ssm_log_dt,
           ssm_b_re, ssm_b_im, ssm_c_re, ssm_c_im, ssm_d, w_glu, w_mix_out, norm_ffn, w_router, b_router,
           w_exp_gate, w_exp_up, w_exp_down, w_sh_gate, w_sh_up, w_sh_down, norm_final):
    batch, seq, d = x.shape
    depth = w_ada.shape[0]
    n_experts = w_router.shape[2]
    ssm_width = ssm_d.shape[1]
    ucol0 = ATTN_WIDTH + 2 * KV_WIDTH
    gacol = (ucol0 + ssm_width) // d
    gscol = gacol + 1
    n_slots = batch * seq * TOP_K + n_experts * MOE_SUB

    c_pad = jnp.zeros((SUBLANES, d), F32).at[:batch].set(c)
    mod = _ada_mod(c_pad, w_ada, b_ada)[:, :batch]
    mod = mod.reshape(depth, batch, 6, 1, d)

    w_in_bf16 = w_in.astype(BF16)
    x2 = x.reshape(batch * seq, d)
    for l in range(depth):
        sh1, sc1, gt1, sh2, sc2, gt2 = (mod[l, :, m] for m in range(6))
        proj = _inproj(x2, norm_mix[l], sh1, sc1, w_in_bf16, b_in, l, seq)
        a_gated = _attention(proj, attn_sinks[l], w_attn_o[l], batch, seq, d, gacol)
        ssm_mats = _ssm_params(ssm_a_re[l], ssm_a_im[l], ssm_log_dt[l], ssm_b_re[l], ssm_b_im[l],
                               ssm_c_re[l], ssm_c_im[l], ssm_d[l])
        y = _ssm(proj, ssm_mats, batch, seq, ucol0)
        x1, logits_t = _mixout(a_gated, y, proj, x2, gt1, w_glu[l], w_mix_out[l], norm_ffn[l],
                               sh2, sc2, w_router[l].T, seq, gscol)
        eidx, wts, rank, counts = _router(logits_t, b_router[l])
        meta, pstart, free_rows = _slot_plan(counts[:, 0])
        dest = _dest(pstart, eidx, rank)
        x_sorted = _dispatch(dest, free_rows, x1, norm_ffn[l], sh2, sc2, n_slots, seq)
        y_sorted = _experts(meta, x_sorted, w_exp_gate, w_exp_up, w_exp_down, l, n_slots)
        x2 = _combine(dest, wts, x1, norm_ffn[l], sh2, sc2, gt2, w_sh_gate[l], w_sh_up[l], w_sh_down[l],
                      norm_final, y_sorted, seq, final_norm=(l == depth - 1))
    return x2.reshape(batch, seq, d)
```

```python
import functools
import math

import jax
import jax.numpy as jnp
from jax import lax
from jax.experimental import pallas as pl
from jax.experimental.pallas import tpu as pltpu

F32 = jnp.float32
BF16 = jnp.bfloat16
I32 = jnp.int32
U32 = jnp.uint32

N_Q_HEADS = 16
N_KV_HEADS = 4
HEAD_DIM = 64
Q_PER_KV = N_Q_HEADS // N_KV_HEADS
WINDOW = 128
ATTN_WIDTH = N_Q_HEADS * HEAD_DIM
KV_WIDTH = N_KV_HEADS * HEAD_DIM
SSM_GROUP_CH = 16
SSM_STATE = 64
N_EXPERT_GROUPS = 8
TOPK_GROUPS = 4
TOP_K = 8
ROUTED_SCALE = 2.5
NORM_EPS = 1e-6

LANES = 128
SUBLANES = 8
VMEM_LIMIT_BYTES = 56 * 1024 * 1024

SSM_GROUPS_PER_BLOCK = LANES // SSM_GROUP_CH
SSM_SEGMENTS = SUBLANES
MOE_SUB = 128
NEG_BIG = -1e30


def _sigmoid(x):
    return 1.0 / (1.0 + jnp.exp(-x))


def _silu(x):
    return x * _sigmoid(x)


def _pack_halves(x):
    n = x.shape[1] // 2
    return pltpu.pack_elementwise([x[:, :n], x[:, n:]], packed_dtype=BF16)


def _unpack_halves(w):
    return tuple(pltpu.unpack_elementwise(w, index=i, packed_dtype=BF16, unpacked_dtype=F32) for i in range(2))


def _params(sem):
    return pltpu.CompilerParams(dimension_semantics=sem, vmem_limit_bytes=VMEM_LIMIT_BYTES)


def _resident(shape):
    nd = len(shape)
    return pl.BlockSpec(shape, lambda *_: (0,) * nd, pipeline_mode=pl.Buffered(1))


def _ada_kernel(c_ref, w_ref, b_ref, o_ref):
    c_act = _silu(c_ref[...]).astype(BF16)
    o_ref[0] = jnp.dot(c_act, w_ref[0].astype(BF16), preferred_element_type=F32) + b_ref[0]


def _ada_mod(c_pad, w_ada, b_ada):
    depth, d, n = w_ada.shape
    tn = 1024
    return pl.pallas_call(
        _ada_kernel,
        grid=(depth, n // tn),
        in_specs=[
            pl.BlockSpec((SUBLANES, d), lambda l, j: (0, 0)),
            pl.BlockSpec((1, d, tn), lambda l, j: (l, 0, j)),
            pl.BlockSpec((1, 1, tn), lambda l, j: (l, 0, j)),
        ],
        out_specs=pl.BlockSpec((1, SUBLANES, tn), lambda l, j: (l, 0, j)),
        out_shape=jax.ShapeDtypeStruct((depth, SUBLANES, n), F32),
        compiler_params=_params(("arbitrary", "arbitrary")),
    )(c_pad, w_ada, b_ada.reshape(depth, 1, n))


def _norm_mod(x, g, shift, scale):
    ms = jnp.mean(x * x, axis=-1, keepdims=True)
    xn = x * lax.rsqrt(ms + NORM_EPS) * g
    return xn * (1.0 + scale) + shift


def _inproj_kernel(x_ref, g_ref, sh_ref, sc_ref, w_ref, b_ref, o_ref, h_scr):
    @pl.when(pl.program_id(1) == 0)
    def _():
        h_scr[...] = _norm_mod(x_ref[...], g_ref[...], sh_ref[0], sc_ref[0]).astype(BF16)

    acc = jnp.dot(h_scr[...], w_ref[0], preferred_element_type=F32)
    o_ref[...] = (acc + b_ref[0]).astype(o_ref.dtype)


def _inproj(x2, g, shift, scale, w_all, b_all, layer, seq):
    t, d = x2.shape
    depth, _, n = w_all.shape
    tm, tn = 1024, 1024
    per_b = seq // tm
    return pl.pallas_call(
        _inproj_kernel,
        grid=(t // tm, n // tn),
        in_specs=[
            pl.BlockSpec((tm, d), lambda i, j: (i, 0)),
            pl.BlockSpec((1, d), lambda i, j: (0, 0)),
            pl.BlockSpec((1, 1, d), lambda i, j: (i // per_b, 0, 0)),
            pl.BlockSpec((1, 1, d), lambda i, j: (i // per_b, 0, 0)),
            pl.BlockSpec((1, d, tn), lambda i, j: (layer, 0, j)),
            pl.BlockSpec((1, 1, tn), lambda i, j: (layer, 0, j)),
        ],
        out_specs=pl.BlockSpec((tm, tn), lambda i, j: (i, j)),
        out_shape=jax.ShapeDtypeStruct((t, n), BF16),
        scratch_shapes=[pltpu.VMEM((tm, d), BF16)],
        compiler_params=_params(("arbitrary", "arbitrary")),
    )(x2, g.reshape(1, d), shift, scale, w_all, b_all.reshape(depth, 1, n))


def _attn_kernel(sink_ref, q_ref, kc_ref, vc_ref, kp_ref, vp_ref, g_ref, wo_ref, o_ref):
    i = pl.program_id(1)
    q = q_ref[...]
    k_cat = jnp.concatenate([kp_ref[...], kc_ref[...]], axis=0)
    v_cat = jnp.concatenate([vp_ref[...], vc_ref[...]], axis=0)
    row = lax.broadcasted_iota(I32, (WINDOW, 2 * WINDOW), 0)
    col = lax.broadcasted_iota(I32, (WINDOW, 2 * WINDOW), 1)
    mask = (col > row) & (col <= row + WINDOW) & ((col >= WINDOW) | (i > 0))
    inv_sqrt = 1.0 / math.sqrt(HEAD_DIM)

    nk = 2 * WINDOW
    pair_w = 2 * HEAD_DIM
    zeros = jnp.zeros((nk, HEAD_DIM), k_cat.dtype)
    first_head = lax.broadcasted_iota(I32, (WINDOW, pair_w), 1) < HEAD_DIM

    def block_diag(x):
        return jnp.concatenate([jnp.concatenate([x, zeros], axis=1),
                                jnp.concatenate([zeros, x], axis=1)], axis=0)

    def softmax_terms(s, sink):
        s = jnp.where(mask, s * inv_sqrt, NEG_BIG)
        m = jnp.maximum(jnp.max(s, axis=-1, keepdims=True), sink)
        p = jnp.exp(s - m)
        return p.astype(BF16), jnp.sum(p, axis=-1, keepdims=True) + jnp.exp(sink - m)

    outs = []
    for j in range(N_KV_HEADS):
        k_bd = block_diag(k_cat[:, j * HEAD_DIM:(j + 1) * HEAD_DIM])
        v_bd = block_diag(v_cat[:, j * HEAD_DIM:(j + 1) * HEAD_DIM])
        for pair in range(Q_PER_KV // 2):
            h0 = j * Q_PER_KV + 2 * pair
            q2 = q[:, h0 * HEAD_DIM:(h0 + 2) * HEAD_DIM]
            s2 = lax.dot_general(q2, k_bd, (((1,), (1,)), ((), ())), preferred_element_type=F32)
            p_a, den_a = softmax_terms(s2[:, :nk], sink_ref[h0])
            p_b, den_b = softmax_terms(s2[:, nk:], sink_ref[h0 + 1])
            o2 = jnp.dot(jnp.concatenate([p_a, p_b], axis=1), v_bd, preferred_element_type=F32)
            outs.append(o2 / jnp.where(first_head, den_a, den_b))
    attn = jnp.concatenate(outs, axis=-1).astype(BF16)
    proj = jnp.dot(attn, wo_ref[...], preferred_element_type=F32)
    o_ref[...] = (_sigmoid(g_ref[...].astype(F32)) * proj).astype(o_ref.dtype)


def _attention(proj, sinks, w_o, batch, seq, d, gcol):
    t = proj.shape[0]
    nb = seq // WINDOW
    kcol = ATTN_WIDTH // KV_WIDTH
    vcol = kcol + 1
    cur = lambda b, i: b * nb + i
    prev = lambda b, i: b * nb + jnp.maximum(i - 1, 0)
    return pl.pallas_call(
        _attn_kernel,
        grid=(batch, nb),
        in_specs=[
            pl.BlockSpec(memory_space=pltpu.SMEM),
            pl.BlockSpec((WINDOW, ATTN_WIDTH), lambda b, i: (cur(b, i), 0)),
            pl.BlockSpec((WINDOW, KV_WIDTH), lambda b, i: (cur(b, i), kcol)),
            pl.BlockSpec((WINDOW, KV_WIDTH), lambda b, i: (cur(b, i), vcol)),
            pl.BlockSpec((WINDOW, KV_WIDTH), lambda b, i: (prev(b, i), kcol)),
            pl.BlockSpec((WINDOW, KV_WIDTH), lambda b, i: (prev(b, i), vcol)),
            pl.BlockSpec((WINDOW, d), lambda b, i: (cur(b, i), gcol)),
            _resident(w_o.shape),
        ],
        out_specs=pl.BlockSpec((WINDOW, d), lambda b, i: (cur(b, i), 0)),
        out_shape=jax.ShapeDtypeStruct((t, d), BF16),
        compiler_params=_params(("arbitrary", "arbitrary")),
    )(sinks, proj, proj, proj, proj, proj, proj, w_o.astype(BF16))


def _ssm_kernel(u_ref, ar_ref, ai_ref, br_ref, bi_ref, cr_ref, cin_ref, d_ref, y_ref,
                uperm, bur, bui, yperm):
    seq = u_ref.shape[0]
    nseg = SSM_SEGMENTS
    n = seq // nseg
    ns = bur.shape[1]

    for r in range(nseg):
        uperm[pl.ds(r, n, stride=nseg), :] = u_ref[r * n:(r + 1) * n, :].astype(F32)

    chunk = 512

    def bu_body(c, _):
        r0 = pl.multiple_of(c * chunk, chunk)
        up = uperm[pl.ds(r0, chunk), :].astype(BF16)
        bur[pl.ds(r0, chunk), :] = (jnp.dot(up, br_ref[0, 0], preferred_element_type=F32)
                                    + jnp.dot(up, br_ref[0, 1], preferred_element_type=F32))
        bui[pl.ds(r0, chunk), :] = (jnp.dot(up, bi_ref[0, 0], preferred_element_type=F32)
                                    + jnp.dot(up, bi_ref[0, 1], preferred_element_type=F32))
        return 0

    lax.fori_loop(0, seq // chunk, bu_body, 0)

    a_r1 = ar_ref[0]
    a_i1 = ai_ref[0]
    a_r = jnp.broadcast_to(a_r1, (nseg, ns))
    a_i = jnp.broadcast_to(a_i1, (nseg, ns))

    def scan_body(s, carry):
        xr, xi = carry
        r0 = pl.multiple_of(s * nseg, nseg)
        nxr = a_r * xr - a_i * xi + bur[pl.ds(r0, nseg), :]
        nxi = a_r * xi + a_i * xr + bui[pl.ds(r0, nseg), :]
        bur[pl.ds(r0, nseg), :] = nxr
        bui[pl.ds(r0, nseg), :] = nxi
        return nxr, nxi

    zeros = jnp.zeros((nseg, ns), F32)
    er, ei = lax.fori_loop(0, n, scan_body, (zeros, zeros), unroll=4)

    pr, pi = a_r1, a_i1
    for _ in range(n.bit_length() - 1):
        pr, pi = pr * pr - pi * pi, 2.0 * pr * pi

    cr_rows = [jnp.zeros((1, ns), F32)]
    ci_rows = [jnp.zeros((1, ns), F32)]
    for r in range(1, nseg):
        pcr, pci = cr_rows[-1], ci_rows[-1]
        cr_rows.append(er[r - 1:r, :] + pr * pcr - pi * pci)
        ci_rows.append(ei[r - 1:r, :] + pr * pci + pi * pcr)
    c_r = jnp.concatenate(cr_rows, axis=0)
    c_i = jnp.concatenate(ci_rows, axis=0)

    def fix_body(s, carry):
        qr, qi = carry
        r0 = pl.multiple_of(s * nseg, nseg)
        bur[pl.ds(r0, nseg), :] = bur[pl.ds(r0, nseg), :] + (qr * c_r - qi * c_i)
        bui[pl.ds(r0, nseg), :] = bui[pl.ds(r0, nseg), :] + (qr * c_i + qi * c_r)
        return qr * a_r - qi * a_i, qr * a_i + qi * a_r

    lax.fori_loop(0, n, fix_body, (a_r, a_i), unroll=4)

    def out_body(c, _):
        r0 = pl.multiple_of(c * chunk, chunk)
        y = jnp.dot(bur[pl.ds(r0, chunk), :].astype(BF16), cr_ref[0], preferred_element_type=F32)
        y = y + jnp.dot(bui[pl.ds(r0, chunk), :].astype(BF16), cin_ref[0], preferred_element_type=F32)
        y = y + d_ref[0] * uperm[pl.ds(r0, chunk), :]
        yperm[pl.ds(r0, chunk), :] = jax.nn.gelu(y, approximate=True)
        return 0

    lax.fori_loop(0, seq // chunk, out_body, 0)

    for r in range(nseg):
        y_ref[r * n:(r + 1) * n, :] = yperm[pl.ds(r, n, stride=nseg), :].astype(y_ref.dtype)


def _ssm_params(a_re, a_im, log_dt, b_re, b_im, c_re, c_im, d_skip):
    g, p = a_re.shape
    c = b_re.shape[-1]
    gpb = SSM_GROUPS_PER_BLOCK
    nblk = g // gpb
    dt = jnp.exp(log_dt)[:, None]
    mag = jnp.exp(dt * a_re)
    ab_r, ab_i = mag * jnp.cos(dt * a_im), mag * jnp.sin(dt * a_im)
    den = a_re * a_re + a_im * a_im
    nr, ni = ab_r - 1.0, ab_i
    f_r = (nr * a_re + ni * a_im) / den
    f_i = (ni * a_re - nr * a_im) / den
    bb_r = f_r[..., None] * b_re - f_i[..., None] * b_im
    bb_i = f_r[..., None] * b_im + f_i[..., None] * b_re
    eye = jnp.eye(gpb, dtype=F32)

    def in_mat(bb):
        return jnp.einsum("bgpc,gh->bgchp", bb.reshape(nblk, gpb, p, c), eye).reshape(nblk, gpb * c, gpb * p)

    def out_mat(cc):
        return jnp.einsum("bgcp,gh->bgphc", cc.reshape(nblk, gpb, c, p), eye).reshape(nblk, gpb * p, gpb * c)

    def hi_lo(m):
        hi = m.astype(BF16)
        return jnp.stack([hi, (m - hi.astype(F32)).astype(BF16)], axis=1)

    return (ab_r.reshape(nblk, 1, gpb * p), ab_i.reshape(nblk, 1, gpb * p),
            hi_lo(in_mat(bb_r)), hi_lo(in_mat(bb_i)), out_mat(c_re).astype(BF16), out_mat(-c_im).astype(BF16),
            d_skip.reshape(nblk, 1, gpb * c))


def _ssm(proj, ssm_mats, batch, seq, ucol0):
    ar, ai, br, bi, cr, cin, dsk = ssm_mats
    nblk = ar.shape[0]
    nu, ns = br.shape[2], br.shape[3]
    ublk0 = ucol0 // nu
    blk3 = lambda shape: pl.BlockSpec((1,) + shape, lambda b, g: (g,) + (0,) * len(shape))
    return pl.pallas_call(
        _ssm_kernel,
        grid=(batch, nblk),
        in_specs=[
            pl.BlockSpec((seq, nu), lambda b, g: (b, ublk0 + g)),
            blk3((1, ns)), blk3((1, ns)), blk3((2, nu, ns)), blk3((2, nu, ns)),
            blk3((ns, nu)), blk3((ns, nu)), blk3((1, nu)),
        ],
        out_specs=pl.BlockSpec((seq, nu), lambda b, g: (b, g)),
        out_shape=jax.ShapeDtypeStruct((batch * seq, nblk * nu), BF16),
        scratch_shapes=[
            pltpu.VMEM((seq, nu), F32), pltpu.VMEM((seq, ns), F32),
            pltpu.VMEM((seq, ns), F32), pltpu.VMEM((seq, nu), F32),
        ],
        compiler_params=_params(("arbitrary", "arbitrary")),
    )(proj, ar, ai, br, bi, cr, cin, dsk)


def _mixout_kernel(a_ref, y_ref, gs_ref, x_ref, gt_ref, wglu_ref, wmix_ref, g2_ref, sh_ref, sc_ref,
                   wr_ref, x1_ref, lg_ref):
    d = x_ref.shape[1]
    y = y_ref[...]
    r = None
    nsplit = 2
    w = d // nsplit
    for c in range(nsplit):
        cols = slice(c * w, (c + 1) * w)
        za = jnp.dot(y, wglu_ref[:, c * w:(c + 1) * w], preferred_element_type=F32)
        zb = jnp.dot(y, wglu_ref[:, d + c * w:d + (c + 1) * w], preferred_element_type=F32)
        mixed = a_ref[:, cols].astype(F32) + _sigmoid(gs_ref[:, cols].astype(F32)) * (za * _sigmoid(zb))
        part = jnp.dot(mixed.astype(BF16), wmix_ref[c * w:(c + 1) * w, :], preferred_element_type=F32)
        r = part if r is None else r + part
    x1 = x_ref[...] + gt_ref[0] * r
    x1_ref[...] = x1
    h2 = _norm_mod(x1, g2_ref[...], sh_ref[0], sc_ref[0])
    h_hi = h2.astype(BF16)
    h_lo = (h2 - h_hi.astype(F32)).astype(BF16)
    nt = lambda a, b: lax.dot_general(a, b, (((1,), (1,)), ((), ())), preferred_element_type=F32)
    lg_ref[...] = nt(wr_ref[0], h_hi) + nt(wr_ref[0], h_lo) + nt(wr_ref[1], h_hi)


def _mixout(a_gated, y, proj, x2, gt1, w_glu, w_mix, g2, sh2, sc2, w_router_t, seq, gscol):
    t, d = x2.shape
    tm = 512
    per_b = seq // tm
    ne = w_router_t.shape[0]
    wr_hi = w_router_t.astype(BF16)
    wr_split = jnp.stack([wr_hi, (w_router_t - wr_hi.astype(F32)).astype(BF16)])
    row = lambda i: (i, 0)
    bvec = pl.BlockSpec((1, 1, d), lambda i: (i // per_b, 0, 0))
    return pl.pallas_call(
        _mixout_kernel,
        grid=(t // tm,),
        in_specs=[
            pl.BlockSpec((tm, d), row),
            pl.BlockSpec((tm, y.shape[1]), row),
            pl.BlockSpec((tm, d), lambda i: (i, gscol)),
            pl.BlockSpec((tm, d), row),
            bvec,
            _resident(w_glu.shape),
            _resident(w_mix.shape),
            pl.BlockSpec((1, d), lambda i: (0, 0)),
            bvec, bvec,
            _resident(wr_split.shape),
        ],
        out_specs=[
            pl.BlockSpec((tm, d), row),
            pl.BlockSpec((ne, tm), lambda i: (0, i)),
        ],
        out_shape=[
            jax.ShapeDtypeStruct((t, d), F32),
            jax.ShapeDtypeStruct((ne, t), F32),
        ],
        compiler_params=_params(("arbitrary",)),
    )(a_gated, y, proj, x2, gt1, w_glu.astype(BF16), w_mix.astype(BF16), g2.reshape(1, d), sh2, sc2,
      wr_split)


def _router_kernel(lg_ref, b_ref, idx_ref, w_ref, rank_ref, cnt_ref, run_scr):
    ng, gs, tn = lg_ref.shape
    ne = ng * gs

    @pl.when(pl.program_id(0) == 0)
    def _():
        run_scr[...] = jnp.zeros(run_scr.shape, run_scr.dtype)

    scores = _sigmoid(lg_ref[...])
    sel = scores + b_ref[...]
    mem = lax.broadcasted_iota(I32, (ng, gs, tn), 1)
    grp = lax.broadcasted_iota(I32, (ng, gs, tn), 0)
    eid = grp * gs + mem
    ninf = -jnp.inf

    m1 = jnp.max(sel, axis=1, keepdims=True)
    first = jnp.min(jnp.where(sel == m1, mem, gs), axis=1, keepdims=True)
    m2 = jnp.max(jnp.where(mem == first, ninf, sel), axis=1, keepdims=True)
    gscore = m1 + m2

    gid = lax.broadcasted_iota(I32, (ng, 1, tn), 0)
    gmask = jnp.zeros((ng, 1, tn), jnp.bool_)
    for _ in range(TOPK_GROUPS):
        m = jnp.max(gscore, axis=0, keepdims=True)
        pick = gid == jnp.min(jnp.where(gscore == m, gid, ng), axis=0, keepdims=True)
        gmask = gmask | pick
        gscore = jnp.where(pick, ninf, gscore)

    def reduce_experts(v):
        return jnp.sum(jnp.sum(v, axis=1, keepdims=True), axis=0, keepdims=True)[0]

    cand = jnp.where(gmask, sel, ninf)
    idxs, wts, picks = [], [], []
    for _ in range(TOP_K):
        m = jnp.max(jnp.max(cand, axis=1, keepdims=True), axis=0, keepdims=True)
        e = jnp.where(cand == m, eid, ne)
        e = jnp.min(jnp.min(e, axis=1, keepdims=True), axis=0, keepdims=True)
        pick = eid == e
        idxs.append(e[0])
        wts.append(reduce_experts(jnp.where(pick, scores, 0.0)))
        picks.append(pick)
        cand = jnp.where(pick, ninf, cand)
    idx_ref[...] = jnp.concatenate(idxs, axis=0)
    w = jnp.concatenate(wts, axis=0)
    w_ref[...] = w / jnp.sum(w, axis=0, keepdims=True) * ROUTED_SCALE

    chosen = picks[0]
    for pick in picks[1:]:
        chosen = chosen | pick
    onehot = chosen.astype(F32).reshape(ne, tn)
    before = (lax.broadcasted_iota(I32, (tn, tn), 0) < lax.broadcasted_iota(I32, (tn, tn), 1)).astype(BF16)
    prefix = jnp.dot(onehot.astype(BF16), before, preferred_element_type=F32)
    rank = (run_scr[...] + prefix).reshape(ng, gs, tn)
    ranks = [reduce_experts(jnp.where(pick, rank, 0.0)) for pick in picks]
    rank_ref[...] = jnp.concatenate(ranks, axis=0).astype(I32)
    run_scr[...] = run_scr[...] + jnp.sum(onehot, axis=1, keepdims=True)
    cnt_ref[...] = run_scr[...].astype(I32)


def _router(logits_t, b_router):
    ne, t = logits_t.shape
    ng = N_EXPERT_GROUPS
    gs = ne // ng
    tn = 512
    tok = pl.BlockSpec((TOP_K, tn), lambda i: (0, i))
    return pl.pallas_call(
        _router_kernel,
        grid=(t // tn,),
        in_specs=[
            pl.BlockSpec((ng, gs, tn), lambda i: (0, 0, i)),
            pl.BlockSpec((ng, gs, 1), lambda i: (0, 0, 0)),
        ],
        out_specs=[tok, tok, tok, pl.BlockSpec((ne, 1), lambda i: (0, 0))],
        out_shape=[jax.ShapeDtypeStruct((TOP_K, t), I32), jax.ShapeDtypeStruct((TOP_K, t), F32),
                   jax.ShapeDtypeStruct((TOP_K, t), I32), jax.ShapeDtypeStruct((ne, 1), I32)],
        scratch_shapes=[pltpu.VMEM((ne, 1), F32)],
        compiler_params=_params(("arbitrary",)),
    )(logits_t.reshape(ng, gs, t), b_router.reshape(ng, gs, 1))


def _dest_kernel(pstart_ref, idx_ref, rank_ref, dest_ref):
    idx = idx_ref[...]
    base = jnp.zeros(idx.shape, I32)
    for e in range(pstart_ref.shape[0]):
        base = jnp.where(idx == e, pstart_ref[e], base)
    dest_ref[...] = base + rank_ref[...]


def _dest(pstart, eidx, rank):
    k, t = eidx.shape
    tn = 2048
    tok = pl.BlockSpec((k, tn), lambda i, ps: (0, i))
    return pl.pallas_call(
        _dest_kernel,
        grid_spec=pltpu.PrefetchScalarGridSpec(
            num_scalar_prefetch=1, grid=(t // tn,), in_specs=[tok, tok], out_specs=tok),
        out_shape=jax.ShapeDtypeStruct((k, t), I32),
        compiler_params=_params(("arbitrary",)),
    )(pstart, eidx, rank)


def _slot_plan(counts):
    padded = (counts + MOE_SUB - 1) // MOE_SUB * MOE_SUB
    pend = jnp.cumsum(padded)
    pstart = pend - padded
    npad = (padded - counts)[:, None]
    ntail = MOE_SUB - npad
    tail0 = pend[-1] + jnp.cumsum(ntail, axis=0) - ntail
    j = jnp.arange(MOE_SUB, dtype=I32)[None, :]
    free_rows = jnp.where(j < npad, (pstart + counts)[:, None] + j, tail0 + j - npad)
    return jnp.stack([pstart, padded]).astype(I32), pstart.astype(I32), free_rows.astype(I32)


def _dispatch_kernel(dest_ref, pad_ref, x_ref, g_ref, sh_ref, sc_ref, wg_ref, wu_ref, wd_ref, xs_hbm, shared_ref,
                     h_scr, zero_scr, sem):
    _, ntile, sub, dw = h_scr.shape
    tm = ntile * sub
    k = dest_ref.shape[2] // tm
    npad = pad_ref.shape[2]
    assert npad % tm == 0
    i = pl.program_id(0)
    b = i & 1

    @pl.when(i == 0)
    def _():
        zero_scr[...] = _pack_halves(jnp.zeros((zero_scr.shape[0], 2 * dw), F32))

    h = _norm_mod(x_ref[...], g_ref[...], sh_ref[0], sc_ref[0])
    h_scr[b] = _pack_halves(h).reshape(ntile, sub, dw)

    def issue_rows(r8):
        for j in range(sub):
            for kk in range(k):
                slot = dest_ref[0, 0, (r8 * sub + j) * k + kk]
                pltpu.make_async_copy(h_scr.at[b, r8, pl.ds(j, 1)], xs_hbm.at[pl.ds(slot, 1)],
                                      sem.at[b]).start(priority=kk % 2)

    per_group = sub * k

    def issue_pad(g):
        for j in range(g * per_group, (g + 1) * per_group):
            pltpu.make_async_copy(zero_scr.at[pl.ds(0, 1)], xs_hbm.at[pl.ds(pad_ref[0, 0, j], 1)],
                                  sem.at[b]).start(priority=j % 2)

    groups = [functools.partial(issue_rows, r8) for r8 in range(ntile)]
    groups += [functools.partial(issue_pad, g) for g in range(npad // per_group)]
    groups = iter(groups)

    def issue_some(n):
        for _ in range(n):
            g = next(groups, None)
            if g is not None:
                g()

    hb = h.astype(BF16)
    f = wg_ref.shape[1]
    d = wd_ref.shape[1]
    gu_pieces = f // LANES
    wn = 2 * LANES
    down_pieces = d // wn
    per_piece = -(-(ntile + npad // per_group) // (gu_pieces + down_pieces))
    hid = []
    for c in range(gu_pieces):
        issue_some(per_piece)
        cols = slice(c * LANES, (c + 1) * LANES)
        gate = jnp.dot(hb, wg_ref[:, cols], preferred_element_type=F32)
        up = jnp.dot(hb, wu_ref[:, cols], preferred_element_type=F32)
        hid.append((_silu(gate) * up).astype(BF16))
    hid = jnp.concatenate(hid, axis=-1)
    for c in range(down_pieces):
        issue_some(per_piece)
        shared_ref[:, c * wn:(c + 1) * wn] = jnp.dot(hid, wd_ref[:, c * wn:(c + 1) * wn],
                                                     preferred_element_type=F32)
    issue_some(ntile + npad)

    def drain(bb):
        for _ in range(k + npad // tm):
            pltpu.make_async_copy(xs_hbm.at[pl.ds(0, tm)], xs_hbm.at[pl.ds(0, tm)], sem.at[bb]).wait()

    @pl.when(i > 0)
    def _():
        drain(1 - b)

    @pl.when(i + 1 == pl.num_programs(0))
    def _():
        drain(b)


def _dispatch(dest, free_rows, x1, g2, sh2, sc2, w_sg, w_su, w_sd, n_rows, seq):
    t, d = x1.shape
    k = dest.shape[0]
    tm = 256
    per_b = seq // tm
    nblk = t // tm
    npad = free_rows.size // nblk
    dest_tm = dest.T.reshape(nblk, 1, tm * k)
    bvec = pl.BlockSpec((1, 1, d), lambda i: (i // per_b, 0, 0))
    return pl.pallas_call(
        _dispatch_kernel,
        grid=(nblk,),
        in_specs=[
            pl.BlockSpec((1, 1, tm * k), lambda i: (i, 0, 0), memory_space=pltpu.SMEM),
            pl.BlockSpec((1, 1, npad), lambda i: (i, 0, 0), memory_space=pltpu.SMEM),
            pl.BlockSpec((tm, d), lambda i: (i, 0)),
            pl.BlockSpec((1, d), lambda i: (0, 0)),
            bvec, bvec,
            _resident(w_sg.shape), _resident(w_su.shape), _resident(w_sd.shape),
        ],
        out_specs=[pl.BlockSpec(memory_space=pl.ANY), pl.BlockSpec((tm, d), lambda i: (i, 0))],
        out_shape=[jax.ShapeDtypeStruct((n_rows, d // 2), U32), jax.ShapeDtypeStruct((t, d), F32)],
        scratch_shapes=[pltpu.VMEM((2, tm // SUBLANES, SUBLANES, d // 2), U32),
                        pltpu.VMEM((SUBLANES, d // 2), U32), pltpu.SemaphoreType.DMA((2,))],
        compiler_params=_params(("arbitrary",)),
    )(dest_tm, free_rows.reshape(nblk, 1, npad), x1, g2.reshape(1, d), sh2, sc2,
      w_sg.astype(BF16), w_su.astype(BF16), w_sd.astype(BF16))


def _chunk_plan(meta, chunk, n_chunks):
    pstart, padded = meta[0], meta[1]
    ne = pstart.shape[0]
    nch = (padded + chunk - 1) // chunk
    cend = jnp.cumsum(nch)
    cfirst = cend - nch
    j = jnp.arange(n_chunks, dtype=I32)
    e_of = jnp.sum((cend[None, :] <= j[:, None]).astype(I32), axis=1)
    valid = e_of < ne
    e_c = jnp.minimum(e_of, ne - 1)
    local = j - cfirst[e_c]
    rows = jnp.clip(padded[e_c] - local * chunk, 0, chunk)
    cmeta = jnp.stack([jnp.where(valid, pstart[e_c] + local * chunk, 0), jnp.where(valid, rows // MOE_SUB, 0)])
    return cmeta.astype(I32), jnp.stack([cfirst, nch]).astype(I32)


def _expert_kernel(meta_ref, cmeta_ref, emeta_ref, x_hbm, wg_ref, wu_ref, wd_ref, y_hbm, xbuf, ybuf,
                   xsem, ysem, ypend):
    e = pl.program_id(0)
    ne = pl.num_programs(0)
    chunk = xbuf.shape[1]
    nsub = chunk // MOE_SUB

    def sub(r):
        return pl.ds(pl.multiple_of(r, MOE_SUB), MOE_SUB)

    def x_piece(r0, slot, j):
        return pltpu.make_async_copy(x_hbm.at[sub(r0 + j * MOE_SUB)], xbuf.at[slot, sub(j * MOE_SUB)],
                                     xsem.at[slot])

    def y_piece(r0, slot, j):
        return pltpu.make_async_copy(ybuf.at[slot, sub(j * MOE_SUB)], y_hbm.at[sub(r0 + j * MOE_SUB)],
                                     ysem.at[slot])

    def x_start(r0, slot, n):
        def body(j, carry):
            x_piece(r0, slot, j).start()
            return carry

        lax.fori_loop(0, n, body, 0)

    def x_wait(slot, n):
        def body(j, carry):
            x_piece(0, slot, 0).wait()
            return carry

        lax.fori_loop(0, n, body, 0)

    def y_drain(slot):
        def body(j, carry):
            y_piece(0, slot, 0).wait()
            return carry

        lax.fori_loop(0, ypend[slot], body, 0)
        ypend[slot] = 0

    nx = xbuf.shape[0]
    ahead = nx - 1

    def request(g):
        x_start(cmeta_ref[0, g], lax.rem(g, nx), cmeta_ref[1, g])

    @pl.when(e == 0)
    def _():
        ypend[0] = 0
        ypend[1] = 0
        for a in range(ahead):
            request(a)

    dw = xbuf.shape[2]

    def compute(xslot, yslot, m):
        x_l, x_r = (v.astype(BF16) for v in _unpack_halves(xbuf[xslot, pl.ds(0, m), :]))

        def proj(w_ref):
            return (jnp.dot(x_l, w_ref[0, 0, :dw, :].astype(BF16), preferred_element_type=F32)
                    + jnp.dot(x_r, w_ref[0, 0, dw:, :].astype(BF16), preferred_element_type=F32))

        hid = (_silu(proj(wg_ref)) * proj(wu_ref)).astype(BF16)
        ybuf[yslot, pl.ds(0, m), :] = _pack_halves(
            jnp.dot(hid, wd_ref[0, 0].astype(BF16), preferred_element_type=F32))

    def run_chunk(g, npieces):
        xslot = lax.rem(g, nx)
        yslot = g & 1
        r0 = cmeta_ref[0, g]
        x_wait(xslot, npieces)
        request(g + ahead)
        y_drain(yslot)
        compute(xslot, yslot, npieces * MOE_SUB)
        for j in range(npieces):
            y_piece(r0, yslot, j).start()
        ypend[yslot] = npieces

    def chunk_body(c, carry):
        g = emeta_ref[0, e] + c
        for npieces in range(1, nsub + 1):
            @pl.when(cmeta_ref[1, g] == npieces)
            def _(npieces=npieces):
                run_chunk(g, npieces)
        return carry

    lax.fori_loop(0, emeta_ref[1, e], chunk_body, 0)

    @pl.when(e + 1 == ne)
    def _():
        y_drain(0)
        y_drain(1)
        tail0 = meta_ref[0, e] + meta_ref[1, e]
        ntail = (y_hbm.shape[0] - tail0) // MOE_SUB
        ybuf[0, pl.ds(0, MOE_SUB), :] = _pack_halves(jnp.zeros((MOE_SUB, 2 * dw), F32))

        def fill(j, carry):
            y_piece(tail0 + j * MOE_SUB, 0, 0).start()
            return carry

        lax.fori_loop(0, ntail, fill, 0)
        ypend[0] = ntail
        y_drain(0)


def _experts(meta, x_sorted, w_gate, w_up, w_down, layer, n_slots):
    dw = x_sorted.shape[1]
    _, ne, d, f = w_gate.shape
    chunk = 512
    n_chunks = n_slots // chunk + ne + 3
    cmeta, emeta = _chunk_plan(meta, chunk, n_chunks)
    wspec = lambda a, b: pl.BlockSpec((1, 1, a, b), lambda e, m, cm, em: (layer, e, 0, 0))
    grid_spec = pltpu.PrefetchScalarGridSpec(
        num_scalar_prefetch=3,
        grid=(ne,),
        in_specs=[pl.BlockSpec(memory_space=pl.ANY), wspec(d, f), wspec(d, f), wspec(f, d)],
        out_specs=pl.BlockSpec(memory_space=pl.ANY),
        scratch_shapes=[
            pltpu.VMEM((3, chunk, dw), U32), pltpu.VMEM((2, chunk, dw), U32),
            pltpu.SemaphoreType.DMA((3,)), pltpu.SemaphoreType.DMA((2,)),
            pltpu.SMEM((2,), I32),
        ],
    )
    return pl.pallas_call(
        _expert_kernel,
        grid_spec=grid_spec,
        out_shape=jax.ShapeDtypeStruct((n_slots, dw), U32),
        compiler_params=_params(("arbitrary",)),
    )(meta, cmeta, emeta, x_sorted, w_gate, w_up, w_down)


def _combine_kernel(dest_ref, dnext_ref, wt_ref, x_ref, shared_ref, gt_ref, gf_ref, y_hbm, o_ref, buf, sem, *,
                    final_norm):
    _, k, ntile, sub, dw = buf.shape
    tm = ntile * sub
    i = pl.program_id(0)
    cur = i & 1
    nxt = 1 - cur
    has_next = i + 1 < pl.num_programs(0)

    def issue_group(idx_ref, b, r8):
        for j in range(sub):
            for kk in range(k):
                slot = idx_ref[0, 0, (r8 * sub + j) * k + kk]
                pltpu.make_async_copy(y_hbm.at[pl.ds(slot, 1)], buf.at[b, kk, r8, pl.ds(j, 1)],
                                      sem.at[b]).start(priority=kk % 2)

    @pl.when(i == 0)
    def _():
        def body(r8, carry):
            issue_group(dest_ref, 0, r8)
            return carry

        lax.fori_loop(0, ntile, body, 0)

    def issue(r8):
        issue_group(dnext_ref, nxt, r8)

    def drain(b):
        for kk in range(k):
            pltpu.make_async_copy(buf.at[b, kk], buf.at[b, kk], sem.at[b]).wait()

    groups = iter(range(ntile))
    for _ in range(ntile - k):
        issue(next(groups))
    drain(cur)
    wt = wt_ref[...]
    routed_l = routed_r = None
    for kk in range(k):
        r8 = next(groups, None)
        if r8 is not None:
            issue(r8)
        y_l, y_r = _unpack_halves(buf[cur, kk].reshape(tm, dw))
        w_k = wt[:, kk:kk + 1]
        routed_l = w_k * y_l if routed_l is None else routed_l + w_k * y_l
        routed_r = w_k * y_r if routed_r is None else routed_r + w_k * y_r
    for r8 in groups:
        issue(r8)
    acc = shared_ref[...] + jnp.concatenate([routed_l, routed_r], axis=-1)

    @pl.when(jnp.logical_not(has_next))
    def _():
        drain(nxt)
    out = x_ref[...] + gt_ref[0] * acc
    if final_norm:
        ms = jnp.mean(out * out, axis=-1, keepdims=True)
        out = out * lax.rsqrt(ms + NORM_EPS) * gf_ref[...]
    o_ref[...] = out


def _combine(dest, wts, x1, shared, gt2, g_final, y_sorted, seq, final_norm):
    t, d = x1.shape
    k = dest.shape[0]
    tm = 128
    per_b = seq // tm
    nblk = t // tm
    dest_tm = dest.T.reshape(nblk, 1, tm * k)
    row = lambda i: (i, 0)
    vec = pl.BlockSpec((1, d), lambda i: (0, 0))
    bvec = pl.BlockSpec((1, 1, d), lambda i: (i // per_b, 0, 0))
    return pl.pallas_call(
        functools.partial(_combine_kernel, final_norm=final_norm),
        grid=(nblk,),
        in_specs=[
            pl.BlockSpec((1, 1, tm * k), lambda i: (i, 0, 0), memory_space=pltpu.SMEM),
            pl.BlockSpec((1, 1, tm * k), lambda i: (jnp.minimum(i + 1, nblk - 1), 0, 0),
                         memory_space=pltpu.SMEM),
            pl.BlockSpec((tm, k), row),
            pl.BlockSpec((tm, d), row),
            pl.BlockSpec((tm, d), row),
            bvec, vec,
            pl.BlockSpec(memory_space=pl.ANY),
        ],
        out_specs=pl.BlockSpec((tm, d), row),
        out_shape=jax.ShapeDtypeStruct((t, d), F32),
        scratch_shapes=[pltpu.VMEM((2, k, tm // SUBLANES, SUBLANES, d // 2), U32),
                        pltpu.SemaphoreType.DMA((2,))],
        compiler_params=_params(("arbitrary",)),
    )(dest_tm, dest_tm, wts.T, x1, shared, gt2, g_final.reshape(1, d), y_sorted)


def kernel(x, c, w_ada, b_ada, norm_mix, w_in, b_in, attn_sinks, w_attn_o, ssm_a_re, ssm_a_im, ssm_log_dt,
           ssm_b_re, ssm_b_im, ssm_c_re, ssm_c_im, ssm_d, w_glu, w_mix_out, norm_ffn, w_router, b_router,
           w_exp_gate, w_exp_up, w_exp_down, w_sh_gate, w_sh_up, w_sh_down, norm_final):
    batch, seq, d = x.shape
    depth = w_ada.shape[0]
    n_experts = w_router.shape[2]
    ssm_width = ssm_d.shape[1]
    ucol0 = ATTN_WIDTH + 2 * KV_WIDTH
    gacol = (ucol0 + ssm_width) // d
    gscol = gacol + 1
    n_slots = batch * seq * TOP_K + n_experts * MOE_SUB

    c_pad = jnp.zeros((SUBLANES, d), F32).at[:batch].set(c)
    mod = _ada_mod(c_pad, w_ada, b_ada)[:, :batch]
    mod = mod.reshape(depth, batch, 6, 1, d)

    w_in_bf16 = w_in.astype(BF16)
    x2 = x.reshape(batch * seq, d)
    for l in range(depth):
        sh1, sc1, gt1, sh2, sc2, gt2 = (mod[l, :, m] for m in range(6))
        proj = _inproj(x2, norm_mix[l], sh1, sc1, w_in_bf16, b_in, l, seq)
        a_gated = _attention(proj, attn_sinks[l], w_attn_o[l], batch, seq, d, gacol)
        ssm_mats = _ssm_params(ssm_a_re[l], ssm_a_im[l], ssm_log_dt[l], ssm_b_re[l], ssm_b_im[l],
                               ssm_c_re[l], ssm_c_im[l], ssm_d[l])
        y = _ssm(proj, ssm_mats, batch, seq, ucol0)
        x1, logits_t = _mixout(a_gated, y, proj, x2, gt1, w_glu[l], w_mix_out[l], norm_ffn[l],
                               sh2, sc2, w_router[l].T, seq, gscol)
        eidx, wts, rank, counts = _router(logits_t, b_router[l])
        meta, pstart, free_rows = _slot_plan(counts[:, 0])
        dest = _dest(pstart, eidx, rank)
        x_sorted, shared = _dispatch(dest, free_rows, x1, norm_ffn[l], sh2, sc2,
                                     w_sh_gate[l], w_sh_up[l], w_sh_down[l], n_slots, seq)
        y_sorted = _experts(meta, x_sorted, w_exp_gate, w_exp_up, w_exp_down, l, n_slots)
        x2 = _combine(dest, wts, x1, shared, gt2, norm_final, y_sorted, seq, final_norm=(l == depth - 1))
    return x2.reshape(batch, seq, d)
```

```python
import functools
import math

import jax
import jax.numpy as jnp
from jax import lax
from jax.experimental import pallas as pl
from jax.experimental.pallas import tpu as pltpu

F32 = jnp.float32
BF16 = jnp.bfloat16
I32 = jnp.int32
U32 = jnp.uint32

N_Q_HEADS = 16
N_KV_HEADS = 4
HEAD_DIM = 64
Q_PER_KV = N_Q_HEADS // N_KV_HEADS
WINDOW = 128
ATTN_WIDTH = N_Q_HEADS * HEAD_DIM
KV_WIDTH = N_KV_HEADS * HEAD_DIM
SSM_GROUP_CH = 16
SSM_STATE = 64
N_EXPERT_GROUPS = 8
TOPK_GROUPS = 4
TOP_K = 8
ROUTED_SCALE = 2.5
NORM_EPS = 1e-6

LANES = 128
SUBLANES = 8
VMEM_LIMIT_BYTES = 56 * 1024 * 1024

SSM_GROUPS_PER_BLOCK = LANES // SSM_GROUP_CH
SSM_SEGMENTS = SUBLANES
MOE_SUB = 128
NEG_BIG = -1e30


def _sigmoid(x):
    return 1.0 / (1.0 + jnp.exp(-x))


def _silu(x):
    return x * _sigmoid(x)


def _pack_halves(x):
    n = x.shape[1] // 2
    return pltpu.pack_elementwise([x[:, :n], x[:, n:]], packed_dtype=BF16)


def _unpack_halves(w):
    return tuple(pltpu.unpack_elementwise(w, index=i, packed_dtype=BF16, unpacked_dtype=F32) for i in range(2))


def _params(sem):
    return pltpu.CompilerParams(dimension_semantics=sem, vmem_limit_bytes=VMEM_LIMIT_BYTES)


def _resident(shape):
    nd = len(shape)
    return pl.BlockSpec(shape, lambda *_: (0,) * nd, pipeline_mode=pl.Buffered(1))


def _ada_kernel(c_ref, w_ref, b_ref, o_ref):
    c_act = _silu(c_ref[...]).astype(BF16)
    o_ref[0] = jnp.dot(c_act, w_ref[0].astype(BF16), preferred_element_type=F32) + b_ref[0]


def _ada_mod(c_pad, w_ada, b_ada):
    depth, d, n = w_ada.shape
    tn = 1024
    return pl.pallas_call(
        _ada_kernel,
        grid=(depth, n // tn),
        in_specs=[
            pl.BlockSpec((SUBLANES, d), lambda l, j: (0, 0)),
            pl.BlockSpec((1, d, tn), lambda l, j: (l, 0, j)),
            pl.BlockSpec((1, 1, tn), lambda l, j: (l, 0, j)),
        ],
        out_specs=pl.BlockSpec((1, SUBLANES, tn), lambda l, j: (l, 0, j)),
        out_shape=jax.ShapeDtypeStruct((depth, SUBLANES, n), F32),
        compiler_params=_params(("arbitrary", "arbitrary")),
    )(c_pad, w_ada, b_ada.reshape(depth, 1, n))


def _norm_mod(x, g, shift, scale):
    ms = jnp.mean(x * x, axis=-1, keepdims=True)
    xn = x * lax.rsqrt(ms + NORM_EPS) * g
    return xn * (1.0 + scale) + shift


def _inproj_kernel(x_ref, g_ref, sh_ref, sc_ref, w_ref, b_ref, o_ref, h_scr):
    @pl.when(pl.program_id(1) == 0)
    def _():
        h_scr[...] = _norm_mod(x_ref[...], g_ref[...], sh_ref[0], sc_ref[0]).astype(BF16)

    acc = jnp.dot(h_scr[...], w_ref[0].astype(BF16), preferred_element_type=F32)
    o_ref[...] = (acc + b_ref[0]).astype(o_ref.dtype)


def _inproj(x2, g, shift, scale, w_all, b_all, layer, seq):
    t, d = x2.shape
    depth, _, n = w_all.shape
    tm, tn = 1024, 1024
    per_b = seq // tm
    return pl.pallas_call(
        _inproj_kernel,
        grid=(t // tm, n // tn),
        in_specs=[
            pl.BlockSpec((tm, d), lambda i, j: (i, 0)),
            pl.BlockSpec((1, d), lambda i, j: (0, 0)),
            pl.BlockSpec((1, 1, d), lambda i, j: (i // per_b, 0, 0)),
            pl.BlockSpec((1, 1, d), lambda i, j: (i // per_b, 0, 0)),
            pl.BlockSpec((1, d, tn), lambda i, j: (layer, 0, j)),
            pl.BlockSpec((1, 1, tn), lambda i, j: (layer, 0, j)),
        ],
        out_specs=pl.BlockSpec((tm, tn), lambda i, j: (i, j)),
        out_shape=jax.ShapeDtypeStruct((t, n), BF16),
        scratch_shapes=[pltpu.VMEM((tm, d), BF16)],
        compiler_params=_params(("arbitrary", "arbitrary")),
    )(x2, g.reshape(1, d), shift, scale, w_all, b_all.reshape(depth, 1, n))


def _attn_kernel(sink_ref, q_ref, kc_ref, vc_ref, kp_ref, vp_ref, g_ref, wo_ref, o_ref):
    i = pl.program_id(1)
    q = q_ref[...]
    k_cat = jnp.concatenate([kp_ref[...], kc_ref[...]], axis=0)
    v_cat = jnp.concatenate([vp_ref[...], vc_ref[...]], axis=0)
    row = lax.broadcasted_iota(I32, (WINDOW, 2 * WINDOW), 0)
    col = lax.broadcasted_iota(I32, (WINDOW, 2 * WINDOW), 1)
    mask = (col > row) & (col <= row + WINDOW) & ((col >= WINDOW) | (i > 0))
    inv_sqrt = 1.0 / math.sqrt(HEAD_DIM)

    nk = 2 * WINDOW
    pair_w = 2 * HEAD_DIM
    zeros = jnp.zeros((nk, HEAD_DIM), k_cat.dtype)
    first_head = lax.broadcasted_iota(I32, (WINDOW, pair_w), 1) < HEAD_DIM

    def block_diag(x):
        return jnp.concatenate([jnp.concatenate([x, zeros], axis=1),
                                jnp.concatenate([zeros, x], axis=1)], axis=0)

    def softmax_terms(s, sink):
        s = jnp.where(mask, s * inv_sqrt, NEG_BIG)
        m = jnp.maximum(jnp.max(s, axis=-1, keepdims=True), sink)
        p = jnp.exp(s - m)
        return p.astype(BF16), jnp.sum(p, axis=-1, keepdims=True) + jnp.exp(sink - m)

    outs = []
    for j in range(N_KV_HEADS):
        k_bd = block_diag(k_cat[:, j * HEAD_DIM:(j + 1) * HEAD_DIM])
        v_bd = block_diag(v_cat[:, j * HEAD_DIM:(j + 1) * HEAD_DIM])
        for pair in range(Q_PER_KV // 2):
            h0 = j * Q_PER_KV + 2 * pair
            q2 = q[:, h0 * HEAD_DIM:(h0 + 2) * HEAD_DIM]
            s2 = lax.dot_general(q2, k_bd, (((1,), (1,)), ((), ())), preferred_element_type=F32)
            p_a, den_a = softmax_terms(s2[:, :nk], sink_ref[h0])
            p_b, den_b = softmax_terms(s2[:, nk:], sink_ref[h0 + 1])
            o2 = jnp.dot(jnp.concatenate([p_a, p_b], axis=1), v_bd, preferred_element_type=F32)
            outs.append(o2 / jnp.where(first_head, den_a, den_b))
    attn = jnp.concatenate(outs, axis=-1).astype(BF16)
    proj = jnp.dot(attn, wo_ref[...], preferred_element_type=F32)
    o_ref[...] = (_sigmoid(g_ref[...].astype(F32)) * proj).astype(o_ref.dtype)


def _attention(proj, sinks, w_o, batch, seq, d, gcol):
    t = proj.shape[0]
    nb = seq // WINDOW
    kcol = ATTN_WIDTH // KV_WIDTH
    vcol = kcol + 1
    cur = lambda b, i: b * nb + i
    prev = lambda b, i: b * nb + jnp.maximum(i - 1, 0)
    return pl.pallas_call(
        _attn_kernel,
        grid=(batch, nb),
        in_specs=[
            pl.BlockSpec(memory_space=pltpu.SMEM),
            pl.BlockSpec((WINDOW, ATTN_WIDTH), lambda b, i: (cur(b, i), 0)),
            pl.BlockSpec((WINDOW, KV_WIDTH), lambda b, i: (cur(b, i), kcol)),
            pl.BlockSpec((WINDOW, KV_WIDTH), lambda b, i: (cur(b, i), vcol)),
            pl.BlockSpec((WINDOW, KV_WIDTH), lambda b, i: (prev(b, i), kcol)),
            pl.BlockSpec((WINDOW, KV_WIDTH), lambda b, i: (prev(b, i), vcol)),
            pl.BlockSpec((WINDOW, d), lambda b, i: (cur(b, i), gcol)),
            _resident(w_o.shape),
        ],
        out_specs=pl.BlockSpec((WINDOW, d), lambda b, i: (cur(b, i), 0)),
        out_shape=jax.ShapeDtypeStruct((t, d), BF16),
        compiler_params=_params(("arbitrary", "arbitrary")),
    )(sinks, proj, proj, proj, proj, proj, proj, w_o.astype(BF16))


def _ssm_kernel(u_ref, ar_ref, ai_ref, br_ref, bi_ref, cr_ref, cin_ref, d_ref, y_ref,
                uperm, bur, bui, yperm):
    seq = u_ref.shape[0]
    nseg = SSM_SEGMENTS
    n = seq // nseg
    ns = bur.shape[1]

    for r in range(nseg):
        uperm[pl.ds(r, n, stride=nseg), :] = u_ref[r * n:(r + 1) * n, :].astype(F32)

    chunk = 512

    def bu_body(c, _):
        r0 = pl.multiple_of(c * chunk, chunk)
        up = uperm[pl.ds(r0, chunk), :].astype(BF16)
        bur[pl.ds(r0, chunk), :] = (jnp.dot(up, br_ref[0, 0], preferred_element_type=F32)
                                    + jnp.dot(up, br_ref[0, 1], preferred_element_type=F32))
        bui[pl.ds(r0, chunk), :] = (jnp.dot(up, bi_ref[0, 0], preferred_element_type=F32)
                                    + jnp.dot(up, bi_ref[0, 1], preferred_element_type=F32))
        return 0

    lax.fori_loop(0, seq // chunk, bu_body, 0)

    a_r1 = ar_ref[0]
    a_i1 = ai_ref[0]
    a_r = jnp.broadcast_to(a_r1, (nseg, ns))
    a_i = jnp.broadcast_to(a_i1, (nseg, ns))

    def scan_body(s, carry):
        xr, xi = carry
        r0 = pl.multiple_of(s * nseg, nseg)
        nxr = a_r * xr - a_i * xi + bur[pl.ds(r0, nseg), :]
        nxi = a_r * xi + a_i * xr + bui[pl.ds(r0, nseg), :]
        bur[pl.ds(r0, nseg), :] = nxr
        bui[pl.ds(r0, nseg), :] = nxi
        return nxr, nxi

    zeros = jnp.zeros((nseg, ns), F32)
    er, ei = lax.fori_loop(0, n, scan_body, (zeros, zeros), unroll=4)

    pr, pi = a_r1, a_i1
    for _ in range(n.bit_length() - 1):
        pr, pi = pr * pr - pi * pi, 2.0 * pr * pi

    cr_rows = [jnp.zeros((1, ns), F32)]
    ci_rows = [jnp.zeros((1, ns), F32)]
    for r in range(1, nseg):
        pcr, pci = cr_rows[-1], ci_rows[-1]
        cr_rows.append(er[r - 1:r, :] + pr * pcr - pi * pci)
        ci_rows.append(ei[r - 1:r, :] + pr * pci + pi * pcr)
    c_r = jnp.concatenate(cr_rows, axis=0)
    c_i = jnp.concatenate(ci_rows, axis=0)

    def fix_body(s, carry):
        qr, qi = carry
        r0 = pl.multiple_of(s * nseg, nseg)
        bur[pl.ds(r0, nseg), :] = bur[pl.ds(r0, nseg), :] + (qr * c_r - qi * c_i)
        bui[pl.ds(r0, nseg), :] = bui[pl.ds(r0, nseg), :] + (qr * c_i + qi * c_r)
        return qr * a_r - qi * a_i, qr * a_i + qi * a_r

    lax.fori_loop(0, n, fix_body, (a_r, a_i), unroll=4)

    def out_body(c, _):
        r0 = pl.multiple_of(c * chunk, chunk)
        y = jnp.dot(bur[pl.ds(r0, chunk), :].astype(BF16), cr_ref[0], preferred_element_type=F32)
        y = y + jnp.dot(bui[pl.ds(r0, chunk), :].astype(BF16), cin_ref[0], preferred_element_type=F32)
        y = y + d_ref[0] * uperm[pl.ds(r0, chunk), :]
        yperm[pl.ds(r0, chunk), :] = jax.nn.gelu(y, approximate=True)
        return 0

    lax.fori_loop(0, seq // chunk, out_body, 0)

    for r in range(nseg):
        y_ref[r * n:(r + 1) * n, :] = yperm[pl.ds(r, n, stride=nseg), :].astype(y_ref.dtype)


def _ssm_params(a_re, a_im, log_dt, b_re, b_im, c_re, c_im, d_skip):
    g, p = a_re.shape
    c = b_re.shape[-1]
    gpb = SSM_GROUPS_PER_BLOCK
    nblk = g // gpb
    dt = jnp.exp(log_dt)[:, None]
    mag = jnp.exp(dt * a_re)
    ab_r, ab_i = mag * jnp.cos(dt * a_im), mag * jnp.sin(dt * a_im)
    den = a_re * a_re + a_im * a_im
    nr, ni = ab_r - 1.0, ab_i
    f_r = (nr * a_re + ni * a_im) / den
    f_i = (ni * a_re - nr * a_im) / den
    bb_r = f_r[..., None] * b_re - f_i[..., None] * b_im
    bb_i = f_r[..., None] * b_im + f_i[..., None] * b_re
    eye = jnp.eye(gpb, dtype=F32)

    def in_mat(bb):
        return jnp.einsum("bgpc,gh->bgchp", bb.reshape(nblk, gpb, p, c), eye).reshape(nblk, gpb * c, gpb * p)

    def out_mat(cc):
        return jnp.einsum("bgcp,gh->bgphc", cc.reshape(nblk, gpb, c, p), eye).reshape(nblk, gpb * p, gpb * c)

    def hi_lo(m):
        hi = m.astype(BF16)
        return jnp.stack([hi, (m - hi.astype(F32)).astype(BF16)], axis=1)

    return (ab_r.reshape(nblk, 1, gpb * p), ab_i.reshape(nblk, 1, gpb * p),
            hi_lo(in_mat(bb_r)), hi_lo(in_mat(bb_i)), out_mat(c_re).astype(BF16), out_mat(-c_im).astype(BF16),
            d_skip.reshape(nblk, 1, gpb * c))


def _ssm(proj, ssm_mats, batch, seq, ucol0):
    ar, ai, br, bi, cr, cin, dsk = ssm_mats
    nblk = ar.shape[0]
    nu, ns = br.shape[2], br.shape[3]
    ublk0 = ucol0 // nu
    blk3 = lambda shape: pl.BlockSpec((1,) + shape, lambda b, g: (g,) + (0,) * len(shape))
    return pl.pallas_call(
        _ssm_kernel,
        grid=(batch, nblk),
        in_specs=[
            pl.BlockSpec((seq, nu), lambda b, g: (b, ublk0 + g)),
            blk3((1, ns)), blk3((1, ns)), blk3((2, nu, ns)), blk3((2, nu, ns)),
            blk3((ns, nu)), blk3((ns, nu)), blk3((1, nu)),
        ],
        out_specs=pl.BlockSpec((seq, nu), lambda b, g: (b, g)),
        out_shape=jax.ShapeDtypeStruct((batch * seq, nblk * nu), BF16),
        scratch_shapes=[
            pltpu.VMEM((seq, nu), F32), pltpu.VMEM((seq, ns), F32),
            pltpu.VMEM((seq, ns), F32), pltpu.VMEM((seq, nu), F32),
        ],
        compiler_params=_params(("arbitrary", "arbitrary")),
    )(proj, ar, ai, br, bi, cr, cin, dsk)


def _mixout_kernel(a_ref, y_ref, gs_ref, x_ref, gt_ref, wglu_ref, wmix_ref, g2_ref, sh_ref, sc_ref,
                   wr_ref, x1_ref, lg_ref):
    d = x_ref.shape[1]
    y = y_ref[...]
    r = None
    nsplit = 2
    w = d // nsplit
    for c in range(nsplit):
        cols = slice(c * w, (c + 1) * w)
        za = jnp.dot(y, wglu_ref[:, c * w:(c + 1) * w], preferred_element_type=F32)
        zb = jnp.dot(y, wglu_ref[:, d + c * w:d + (c + 1) * w], preferred_element_type=F32)
        mixed = a_ref[:, cols].astype(F32) + _sigmoid(gs_ref[:, cols].astype(F32)) * (za * _sigmoid(zb))
        part = jnp.dot(mixed.astype(BF16), wmix_ref[c * w:(c + 1) * w, :], preferred_element_type=F32)
        r = part if r is None else r + part
    x1 = x_ref[...] + gt_ref[0] * r
    x1_ref[...] = x1
    h2 = _norm_mod(x1, g2_ref[...], sh_ref[0], sc_ref[0])
    h_hi = h2.astype(BF16)
    h_lo = (h2 - h_hi.astype(F32)).astype(BF16)
    nt = lambda a, b: lax.dot_general(a, b, (((1,), (1,)), ((), ())), preferred_element_type=F32)
    lg_ref[...] = nt(wr_ref[0], h_hi) + nt(wr_ref[0], h_lo) + nt(wr_ref[1], h_hi)


def _mixout(a_gated, y, proj, x2, gt1, w_glu, w_mix, g2, sh2, sc2, w_router_t, seq, gscol):
    t, d = x2.shape
    tm = 512
    per_b = seq // tm
    ne = w_router_t.shape[0]
    wr_hi = w_router_t.astype(BF16)
    wr_split = jnp.stack([wr_hi, (w_router_t - wr_hi.astype(F32)).astype(BF16)])
    row = lambda i: (i, 0)
    bvec = pl.BlockSpec((1, 1, d), lambda i: (i // per_b, 0, 0))
    return pl.pallas_call(
        _mixout_kernel,
        grid=(t // tm,),
        in_specs=[
            pl.BlockSpec((tm, d), row),
            pl.BlockSpec((tm, y.shape[1]), row),
            pl.BlockSpec((tm, d), lambda i: (i, gscol)),
            pl.BlockSpec((tm, d), row),
            bvec,
            _resident(w_glu.shape),
            _resident(w_mix.shape),
            pl.BlockSpec((1, d), lambda i: (0, 0)),
            bvec, bvec,
            _resident(wr_split.shape),
        ],
        out_specs=[
            pl.BlockSpec((tm, d), row),
            pl.BlockSpec((ne, tm), lambda i: (0, i)),
        ],
        out_shape=[
            jax.ShapeDtypeStruct((t, d), F32),
            jax.ShapeDtypeStruct((ne, t), F32),
        ],
        compiler_params=_params(("arbitrary",)),
    )(a_gated, y, proj, x2, gt1, w_glu.astype(BF16), w_mix.astype(BF16), g2.reshape(1, d), sh2, sc2,
      wr_split)


def _router_kernel(lg_ref, b_ref, idx_ref, w_ref, rank_ref, cnt_ref, run_scr):
    ng, gs, tn = lg_ref.shape
    ne = ng * gs

    @pl.when(pl.program_id(0) == 0)
    def _():
        run_scr[...] = jnp.zeros(run_scr.shape, run_scr.dtype)

    scores = _sigmoid(lg_ref[...])
    sel = scores + b_ref[...]
    mem = lax.broadcasted_iota(I32, (ng, gs, tn), 1)
    grp = lax.broadcasted_iota(I32, (ng, gs, tn), 0)
    eid = grp * gs + mem
    ninf = -jnp.inf

    m1 = jnp.max(sel, axis=1, keepdims=True)
    first = jnp.min(jnp.where(sel == m1, mem, gs), axis=1, keepdims=True)
    m2 = jnp.max(jnp.where(mem == first, ninf, sel), axis=1, keepdims=True)
    gscore = m1 + m2

    gid = lax.broadcasted_iota(I32, (ng, 1, tn), 0)
    gmask = jnp.zeros((ng, 1, tn), jnp.bool_)
    for _ in range(TOPK_GROUPS):
        m = jnp.max(gscore, axis=0, keepdims=True)
        pick = gid == jnp.min(jnp.where(gscore == m, gid, ng), axis=0, keepdims=True)
        gmask = gmask | pick
        gscore = jnp.where(pick, ninf, gscore)

    def reduce_experts(v):
        return jnp.sum(jnp.sum(v, axis=1, keepdims=True), axis=0, keepdims=True)[0]

    cand = jnp.where(gmask, sel, ninf)
    idxs, wts, picks = [], [], []
    for _ in range(TOP_K):
        m = jnp.max(jnp.max(cand, axis=1, keepdims=True), axis=0, keepdims=True)
        e = jnp.where(cand == m, eid, ne)
        e = jnp.min(jnp.min(e, axis=1, keepdims=True), axis=0, keepdims=True)
        pick = eid == e
        idxs.append(e[0])
        wts.append(reduce_experts(jnp.where(pick, scores, 0.0)))
        picks.append(pick)
        cand = jnp.where(pick, ninf, cand)
    idx_ref[...] = jnp.concatenate(idxs, axis=0)
    w = jnp.concatenate(wts, axis=0)
    w_ref[...] = w / jnp.sum(w, axis=0, keepdims=True) * ROUTED_SCALE

    chosen = picks[0]
    for pick in picks[1:]:
        chosen = chosen | pick
    onehot = chosen.astype(F32).reshape(ne, tn)
    before = (lax.broadcasted_iota(I32, (tn, tn), 0) < lax.broadcasted_iota(I32, (tn, tn), 1)).astype(BF16)
    prefix = jnp.dot(onehot.astype(BF16), before, preferred_element_type=F32)
    rank = (run_scr[...] + prefix).reshape(ng, gs, tn)
    ranks = [reduce_experts(jnp.where(pick, rank, 0.0)) for pick in picks]
    rank_ref[...] = jnp.concatenate(ranks, axis=0).astype(I32)
    run_scr[...] = run_scr[...] + jnp.sum(onehot, axis=1, keepdims=True)
    cnt_ref[...] = run_scr[...].astype(I32)


def _router(logits_t, b_router):
    ne, t = logits_t.shape
    ng = N_EXPERT_GROUPS
    gs = ne // ng
    tn = 512
    tok = pl.BlockSpec((TOP_K, tn), lambda i: (0, i))
    return pl.pallas_call(
        _router_kernel,
        grid=(t // tn,),
        in_specs=[
            pl.BlockSpec((ng, gs, tn), lambda i: (0, 0, i)),
            pl.BlockSpec((ng, gs, 1), lambda i: (0, 0, 0)),
        ],
        out_specs=[tok, tok, tok, pl.BlockSpec((ne, 1), lambda i: (0, 0))],
        out_shape=[jax.ShapeDtypeStruct((TOP_K, t), I32), jax.ShapeDtypeStruct((TOP_K, t), F32),
                   jax.ShapeDtypeStruct((TOP_K, t), I32), jax.ShapeDtypeStruct((ne, 1), I32)],
        scratch_shapes=[pltpu.VMEM((ne, 1), F32)],
        compiler_params=_params(("arbitrary",)),
    )(logits_t.reshape(ng, gs, t), b_router.reshape(ng, gs, 1))


def _dest_kernel(pstart_ref, idx_ref, rank_ref, dest_ref):
    idx = idx_ref[...]
    base = jnp.zeros(idx.shape, I32)
    for e in range(pstart_ref.shape[0]):
        base = jnp.where(idx == e, pstart_ref[e], base)
    dest_ref[...] = base + rank_ref[...]


def _dest(pstart, eidx, rank):
    k, t = eidx.shape
    tn = 2048
    tok = pl.BlockSpec((k, tn), lambda i, ps: (0, i))
    return pl.pallas_call(
        _dest_kernel,
        grid_spec=pltpu.PrefetchScalarGridSpec(
            num_scalar_prefetch=1, grid=(t // tn,), in_specs=[tok, tok], out_specs=tok),
        out_shape=jax.ShapeDtypeStruct((k, t), I32),
        compiler_params=_params(("arbitrary",)),
    )(pstart, eidx, rank)


def _slot_plan(counts):
    padded = (counts + MOE_SUB - 1) // MOE_SUB * MOE_SUB
    pend = jnp.cumsum(padded)
    pstart = pend - padded
    npad = (padded - counts)[:, None]
    ntail = MOE_SUB - npad
    tail0 = pend[-1] + jnp.cumsum(ntail, axis=0) - ntail
    j = jnp.arange(MOE_SUB, dtype=I32)[None, :]
    free_rows = jnp.where(j < npad, (pstart + counts)[:, None] + j, tail0 + j - npad)
    return jnp.stack([pstart, padded]).astype(I32), pstart.astype(I32), free_rows.astype(I32)


def _dispatch_kernel(dest_ref, pad_ref, x_ref, g_ref, sh_ref, sc_ref, wg_ref, wu_ref, wd_ref, xs_hbm, shared_ref,
                     h_scr, zero_scr, sem):
    _, ntile, sub, dw = h_scr.shape
    tm = ntile * sub
    k = dest_ref.shape[2] // tm
    npad = pad_ref.shape[2]
    assert npad % tm == 0
    i = pl.program_id(0)
    b = i & 1

    @pl.when(i == 0)
    def _():
        zero_scr[...] = _pack_halves(jnp.zeros((zero_scr.shape[0], 2 * dw), F32))

    h = _norm_mod(x_ref[...], g_ref[...], sh_ref[0], sc_ref[0])
    h_scr[b] = _pack_halves(h).reshape(ntile, sub, dw)

    def issue_rows(r8):
        for j in range(sub):
            for kk in range(k):
                slot = dest_ref[0, 0, (r8 * sub + j) * k + kk]
                pltpu.make_async_copy(h_scr.at[b, r8, pl.ds(j, 1)], xs_hbm.at[pl.ds(slot, 1)],
                                      sem.at[b]).start(priority=kk % 2)

    per_group = sub * k

    def issue_pad(g):
        for j in range(g * per_group, (g + 1) * per_group):
            pltpu.make_async_copy(zero_scr.at[pl.ds(0, 1)], xs_hbm.at[pl.ds(pad_ref[0, 0, j], 1)],
                                  sem.at[b]).start(priority=j % 2)

    groups = [functools.partial(issue_rows, r8) for r8 in range(ntile)]
    groups += [functools.partial(issue_pad, g) for g in range(npad // per_group)]
    groups = iter(groups)

    def issue_some(n):
        for _ in range(n):
            g = next(groups, None)
            if g is not None:
                g()

    hb = h.astype(BF16)
    f = wg_ref.shape[1]
    d = wd_ref.shape[1]
    gu_pieces = f // LANES
    wn = 2 * LANES
    down_pieces = d // wn
    per_piece = -(-(ntile + npad // per_group) // (gu_pieces + down_pieces))
    hid = []
    for c in range(gu_pieces):
        issue_some(per_piece)
        cols = slice(c * LANES, (c + 1) * LANES)
        gate = jnp.dot(hb, wg_ref[:, cols], preferred_element_type=F32)
        up = jnp.dot(hb, wu_ref[:, cols], preferred_element_type=F32)
        hid.append((_silu(gate) * up).astype(BF16))
    hid = jnp.concatenate(hid, axis=-1)
    for c in range(down_pieces):
        issue_some(per_piece)
        shared_ref[:, c * wn:(c + 1) * wn] = jnp.dot(hid, wd_ref[:, c * wn:(c + 1) * wn],
                                                     preferred_element_type=F32)
    issue_some(ntile + npad)

    def drain(bb):
        for _ in range(k + npad // tm):
            pltpu.make_async_copy(xs_hbm.at[pl.ds(0, tm)], xs_hbm.at[pl.ds(0, tm)], sem.at[bb]).wait()

    @pl.when(i > 0)
    def _():
        drain(1 - b)

    @pl.when(i + 1 == pl.num_programs(0))
    def _():
        drain(b)


def _dispatch(dest, free_rows, x1, g2, sh2, sc2, w_sg, w_su, w_sd, n_rows, seq):
    t, d = x1.shape
    k = dest.shape[0]
    tm = 256
    per_b = seq // tm
    nblk = t // tm
    npad = free_rows.size // nblk
    dest_tm = dest.T.reshape(nblk, 1, tm * k)
    bvec = pl.BlockSpec((1, 1, d), lambda i: (i // per_b, 0, 0))
    return pl.pallas_call(
        _dispatch_kernel,
        grid=(nblk,),
        in_specs=[
            pl.BlockSpec((1, 1, tm * k), lambda i: (i, 0, 0), memory_space=pltpu.SMEM),
            pl.BlockSpec((1, 1, npad), lambda i: (i, 0, 0), memory_space=pltpu.SMEM),
            pl.BlockSpec((tm, d), lambda i: (i, 0)),
            pl.BlockSpec((1, d), lambda i: (0, 0)),
            bvec, bvec,
            _resident(w_sg.shape), _resident(w_su.shape), _resident(w_sd.shape),
        ],
        out_specs=[pl.BlockSpec(memory_space=pl.ANY), pl.BlockSpec((tm, d), lambda i: (i, 0))],
        out_shape=[jax.ShapeDtypeStruct((n_rows, d // 2), U32), jax.ShapeDtypeStruct((t, d), F32)],
        scratch_shapes=[pltpu.VMEM((2, tm // SUBLANES, SUBLANES, d // 2), U32),
                        pltpu.VMEM((SUBLANES, d // 2), U32), pltpu.SemaphoreType.DMA((2,))],
        compiler_params=_params(("arbitrary",)),
    )(dest_tm, free_rows.reshape(nblk, 1, npad), x1, g2.reshape(1, d), sh2, sc2,
      w_sg.astype(BF16), w_su.astype(BF16), w_sd.astype(BF16))


def _chunk_plan(meta, chunk, n_chunks):
    pstart, padded = meta[0], meta[1]
    ne = pstart.shape[0]
    nch = (padded + chunk - 1) // chunk
    cend = jnp.cumsum(nch)
    cfirst = cend - nch
    j = jnp.arange(n_chunks, dtype=I32)
    e_of = jnp.sum((cend[None, :] <= j[:, None]).astype(I32), axis=1)
    valid = e_of < ne
    e_c = jnp.minimum(e_of, ne - 1)
    local = j - cfirst[e_c]
    rows = jnp.clip(padded[e_c] - local * chunk, 0, chunk)
    cmeta = jnp.stack([jnp.where(valid, pstart[e_c] + local * chunk, 0), jnp.where(valid, rows // MOE_SUB, 0)])
    return cmeta.astype(I32), jnp.stack([cfirst, nch]).astype(I32)


def _expert_kernel(meta_ref, cmeta_ref, emeta_ref, x_hbm, wg_ref, wu_ref, wd_ref, y_hbm, xbuf, ybuf,
                   xsem, ysem, ypend):
    e = pl.program_id(0)
    ne = pl.num_programs(0)
    chunk = xbuf.shape[1]
    nsub = chunk // MOE_SUB

    def sub(r):
        return pl.ds(pl.multiple_of(r, MOE_SUB), MOE_SUB)

    def x_piece(r0, slot, j):
        return pltpu.make_async_copy(x_hbm.at[sub(r0 + j * MOE_SUB)], xbuf.at[slot, sub(j * MOE_SUB)],
                                     xsem.at[slot])

    def y_piece(r0, slot, j):
        return pltpu.make_async_copy(ybuf.at[slot, sub(j * MOE_SUB)], y_hbm.at[sub(r0 + j * MOE_SUB)],
                                     ysem.at[slot])

    def x_start(r0, slot, n):
        def body(j, carry):
            x_piece(r0, slot, j).start()
            return carry

        lax.fori_loop(0, n, body, 0)

    def x_wait(slot, n):
        def body(j, carry):
            x_piece(0, slot, 0).wait()
            return carry

        lax.fori_loop(0, n, body, 0)

    def y_drain(slot):
        def body(j, carry):
            y_piece(0, slot, 0).wait()
            return carry

        lax.fori_loop(0, ypend[slot], body, 0)
        ypend[slot] = 0

    nx = xbuf.shape[0]
    ahead = nx - 1

    def request(g):
        x_start(cmeta_ref[0, g], lax.rem(g, nx), cmeta_ref[1, g])

    @pl.when(e == 0)
    def _():
        ypend[0] = 0
        ypend[1] = 0
        for a in range(ahead):
            request(a)

    dw = xbuf.shape[2]

    def compute(xslot, yslot, m):
        x_l, x_r = (v.astype(BF16) for v in _unpack_halves(xbuf[xslot, pl.ds(0, m), :]))

        def proj(w_ref):
            return (jnp.dot(x_l, w_ref[0, 0, :dw, :].astype(BF16), preferred_element_type=F32)
                    + jnp.dot(x_r, w_ref[0, 0, dw:, :].astype(BF16), preferred_element_type=F32))

        hid = (_silu(proj(wg_ref)) * proj(wu_ref)).astype(BF16)
        ybuf[yslot, pl.ds(0, m), :] = _pack_halves(
            jnp.dot(hid, wd_ref[0, 0].astype(BF16), preferred_element_type=F32))

    def run_chunk(g, npieces):
        xslot = lax.rem(g, nx)
        yslot = g & 1
        r0 = cmeta_ref[0, g]
        x_wait(xslot, npieces)
        request(g + ahead)
        y_drain(yslot)
        compute(xslot, yslot, npieces * MOE_SUB)
        for j in range(npieces):
            y_piece(r0, yslot, j).start()
        ypend[yslot] = npieces

    def chunk_body(c, carry):
        g = emeta_ref[0, e] + c
        for npieces in range(1, nsub + 1):
            @pl.when(cmeta_ref[1, g] == npieces)
            def _(npieces=npieces):
                run_chunk(g, npieces)
        return carry

    lax.fori_loop(0, emeta_ref[1, e], chunk_body, 0)

    @pl.when(e + 1 == ne)
    def _():
        y_drain(0)
        y_drain(1)
        tail0 = meta_ref[0, e] + meta_ref[1, e]
        ntail = (y_hbm.shape[0] - tail0) // MOE_SUB
        ybuf[0, pl.ds(0, MOE_SUB), :] = _pack_halves(jnp.zeros((MOE_SUB, 2 * dw), F32))

        def fill(j, carry):
            y_piece(tail0 + j * MOE_SUB, 0, 0).start()
            return carry

        lax.fori_loop(0, ntail, fill, 0)
        ypend[0] = ntail
        y_drain(0)


def _experts(meta, x_sorted, w_gate, w_up, w_down, layer, n_slots):
    dw = x_sorted.shape[1]
    _, ne, d, f = w_gate.shape
    chunk = 512
    n_chunks = n_slots // chunk + ne + 3
    cmeta, emeta = _chunk_plan(meta, chunk, n_chunks)
    wspec = lambda a, b: pl.BlockSpec((1, 1, a, b), lambda e, m, cm, em: (layer, e, 0, 0))
    grid_spec = pltpu.PrefetchScalarGridSpec(
        num_scalar_prefetch=3,
        grid=(ne,),
        in_specs=[pl.BlockSpec(memory_space=pl.ANY), wspec(d, f), wspec(d, f), wspec(f, d)],
        out_specs=pl.BlockSpec(memory_space=pl.ANY),
        scratch_shapes=[
            pltpu.VMEM((3, chunk, dw), U32), pltpu.VMEM((2, chunk, dw), U32),
            pltpu.SemaphoreType.DMA((3,)), pltpu.SemaphoreType.DMA((2,)),
            pltpu.SMEM((2,), I32),
        ],
    )
    return pl.pallas_call(
        _expert_kernel,
        grid_spec=grid_spec,
        out_shape=jax.ShapeDtypeStruct((n_slots, dw), U32),
        compiler_params=_params(("arbitrary",)),
    )(meta, cmeta, emeta, x_sorted, w_gate, w_up, w_down)


def _combine_kernel(dest_ref, dnext_ref, wt_ref, x_ref, shared_ref, gt_ref, gf_ref, y_hbm, o_ref, buf, sem, *,
                    final_norm):
    _, k, ntile, sub, dw = buf.shape
    tm = ntile * sub
    i = pl.program_id(0)
    cur = i & 1
    nxt = 1 - cur
    has_next = i + 1 < pl.num_programs(0)

    def issue_group(idx_ref, b, r8):
        for j in range(sub):
            for kk in range(k):
                slot = idx_ref[0, 0, (r8 * sub + j) * k + kk]
                pltpu.make_async_copy(y_hbm.at[pl.ds(slot, 1)], buf.at[b, kk, r8, pl.ds(j, 1)],
                                      sem.at[b]).start(priority=kk % 2)

    @pl.when(i == 0)
    def _():
        def body(r8, carry):
            issue_group(dest_ref, 0, r8)
            return carry

        lax.fori_loop(0, ntile, body, 0)

    def issue(r8):
        issue_group(dnext_ref, nxt, r8)

    def drain(b):
        for kk in range(k):
            pltpu.make_async_copy(buf.at[b, kk], buf.at[b, kk], sem.at[b]).wait()

    groups = iter(range(ntile))
    for _ in range(ntile - k):
        issue(next(groups))
    drain(cur)
    wt = wt_ref[...]
    routed_l = routed_r = None
    for kk in range(k):
        r8 = next(groups, None)
        if r8 is not None:
            issue(r8)
        y_l, y_r = _unpack_halves(buf[cur, kk].reshape(tm, dw))
        w_k = wt[:, kk:kk + 1]
        routed_l = w_k * y_l if routed_l is None else routed_l + w_k * y_l
        routed_r = w_k * y_r if routed_r is None else routed_r + w_k * y_r
    for r8 in groups:
        issue(r8)
    acc = shared_ref[...] + jnp.concatenate([routed_l, routed_r], axis=-1)

    @pl.when(jnp.logical_not(has_next))
    def _():
        drain(nxt)
    out = x_ref[...] + gt_ref[0] * acc
    if final_norm:
        ms = jnp.mean(out * out, axis=-1, keepdims=True)
        out = out * lax.rsqrt(ms + NORM_EPS) * gf_ref[...]
    o_ref[...] = out


def _combine(dest, wts, x1, shared, gt2, g_final, y_sorted, seq, final_norm):
    t, d = x1.shape
    k = dest.shape[0]
    tm = 128
    per_b = seq // tm
    nblk = t // tm
    dest_tm = dest.T.reshape(nblk, 1, tm * k)
    row = lambda i: (i, 0)
    vec = pl.BlockSpec((1, d), lambda i: (0, 0))
    bvec = pl.BlockSpec((1, 1, d), lambda i: (i // per_b, 0, 0))
    return pl.pallas_call(
        functools.partial(_combine_kernel, final_norm=final_norm),
        grid=(nblk,),
        in_specs=[
            pl.BlockSpec((1, 1, tm * k), lambda i: (i, 0, 0), memory_space=pltpu.SMEM),
            pl.BlockSpec((1, 1, tm * k), lambda i: (jnp.minimum(i + 1, nblk - 1), 0, 0),
                         memory_space=pltpu.SMEM),
            pl.BlockSpec((tm, k), row),
            pl.BlockSpec((tm, d), row),
            pl.BlockSpec((tm, d), row),
            bvec, vec,
            pl.BlockSpec(memory_space=pl.ANY),
        ],
        out_specs=pl.BlockSpec((tm, d), row),
        out_shape=jax.ShapeDtypeStruct((t, d), F32),
        scratch_shapes=[pltpu.VMEM((2, k, tm // SUBLANES, SUBLANES, d // 2), U32),
                        pltpu.SemaphoreType.DMA((2,))],
        compiler_params=_params(("arbitrary",)),
    )(dest_tm, dest_tm, wts.T, x1, shared, gt2, g_final.reshape(1, d), y_sorted)


def kernel(x, c, w_ada, b_ada, norm_mix, w_in, b_in, attn_sinks, w_attn_o, ssm_a_re, ssm_a_im, ssm_log_dt,
           ssm_b_re, ssm_b_im, ssm_c_re, ssm_c_im, ssm_d, w_glu, w_mix_out, norm_ffn, w_router, b_router,
           w_exp_gate, w_exp_up, w_exp_down, w_sh_gate, w_sh_up, w_sh_down, norm_final):
    batch, seq, d = x.shape
    depth = w_ada.shape[0]
    n_experts = w_router.shape[2]
    ssm_width = ssm_d.shape[1]
    ucol0 = ATTN_WIDTH + 2 * KV_WIDTH
    gacol = (ucol0 + ssm_width) // d
    gscol = gacol + 1
    n_slots = batch * seq * TOP_K + n_experts * MOE_SUB

    c_pad = jnp.zeros((SUBLANES, d), F32).at[:batch].set(c)
    mod = _ada_mod(c_pad, w_ada, b_ada)[:, :batch]
    mod = mod.reshape(depth, batch, 6, 1, d)

    x2 = x.reshape(batch * seq, d)
    for l in range(depth):
        sh1, sc1, gt1, sh2, sc2, gt2 = (mod[l, :, m] for m in range(6))
        proj = _inproj(x2, norm_mix[l], sh1, sc1, w_in, b_in, l, seq)
        a_gated = _attention(proj, attn_sinks[l], w_attn_o[l], batch, seq, d, gacol)
        ssm_mats = _ssm_params(ssm_a_re[l], ssm_a_im[l], ssm_log_dt[l], ssm_b_re[l], ssm_b_im[l],
                               ssm_c_re[l], ssm_c_im[l], ssm_d[l])
        y = _ssm(proj, ssm_mats, batch, seq, ucol0)
        x1, logits_t = _mixout(a_gated, y, proj, x2, gt1, w_glu[l], w_mix_out[l], norm_ffn[l],
                               sh2, sc2, w_router[l].T, seq, gscol)
        eidx, wts, rank, counts = _router(logits_t, b_router[l])
        meta, pstart, free_rows = _slot_plan(counts[:, 0])
        dest = _dest(pstart, eidx, rank)
        x_sorted, shared = _dispatch(dest, free_rows, x1, norm_ffn[l], sh2, sc2,
                                     w_sh_gate[l], w_sh_up[l], w_sh_down[l], n_slots, seq)
        y_sorted = _experts(meta, x_sorted, w_exp_gate, w_exp_up, w_exp_down, l, n_slots)
        x2 = _combine(dest, wts, x1, shared, gt2, norm_final, y_sorted, seq, final_norm=(l == depth - 1))
    return x2.reshape(batch, seq, d)
```

```python
import functools
import math

import jax
import jax.numpy as jnp
from jax import lax
from jax.experimental import pallas as pl
from jax.experimental.pallas import tpu as pltpu

F32 = jnp.float32
BF16 = jnp.bfloat16
I32 = jnp.int32
U32 = jnp.uint32

N_Q_HEADS = 16
N_KV_HEADS = 4
HEAD_DIM = 64
Q_PER_KV = N_Q_HEADS // N_KV_HEADS
WINDOW = 128
ATTN_WIDTH = N_Q_HEADS * HEAD_DIM
KV_WIDTH = N_KV_HEADS * HEAD_DIM
SSM_GROUP_CH = 16
SSM_STATE = 64
N_EXPERT_GROUPS = 8
TOPK_GROUPS = 4
TOP_K = 8
ROUTED_SCALE = 2.5
NORM_EPS = 1e-6

LANES = 128
SUBLANES = 8
VMEM_LIMIT_BYTES = 56 * 1024 * 1024

SSM_GROUPS_PER_BLOCK = LANES // SSM_GROUP_CH
SSM_SEGMENTS = SUBLANES
MOE_SUB = 128
NEG_BIG = -1e30


def _sigmoid(x):
    return 1.0 / (1.0 + jnp.exp(-x))


def _silu(x):
    return x * _sigmoid(x)


def _pack_halves(x):
    n = x.shape[1] // 2
    return pltpu.pack_elementwise([x[:, :n], x[:, n:]], packed_dtype=BF16)


def _unpack_halves(w):
    return tuple(pltpu.unpack_elementwise(w, index=i, packed_dtype=BF16, unpacked_dtype=F32) for i in range(2))


def _params(sem):
    return pltpu.CompilerParams(dimension_semantics=sem, vmem_limit_bytes=VMEM_LIMIT_BYTES)


def _resident(shape):
    nd = len(shape)
    return pl.BlockSpec(shape, lambda *_: (0,) * nd, pipeline_mode=pl.Buffered(1))


def _ada_kernel(c_ref, w_ref, b_ref, o_ref):
    c_act = _silu(c_ref[...]).astype(BF16)
    o_ref[0] = jnp.dot(c_act, w_ref[0].astype(BF16), preferred_element_type=F32) + b_ref[0]


def _ada_mod(c_pad, w_ada, b_ada):
    depth, d, n = w_ada.shape
    tn = 1024
    return pl.pallas_call(
        _ada_kernel,
        grid=(depth, n // tn),
        in_specs=[
            pl.BlockSpec((SUBLANES, d), lambda l, j: (0, 0)),
            pl.BlockSpec((1, d, tn), lambda l, j: (l, 0, j)),
            pl.BlockSpec((1, 1, tn), lambda l, j: (l, 0, j)),
        ],
        out_specs=pl.BlockSpec((1, SUBLANES, tn), lambda l, j: (l, 0, j)),
        out_shape=jax.ShapeDtypeStruct((depth, SUBLANES, n), F32),
        compiler_params=_params(("arbitrary", "arbitrary")),
    )(c_pad, w_ada, b_ada.reshape(depth, 1, n))


def _norm_mod(x, g, shift, scale):
    ms = jnp.mean(x * x, axis=-1, keepdims=True)
    xn = x * lax.rsqrt(ms + NORM_EPS) * g
    return xn * (1.0 + scale) + shift


def _inproj_kernel(x_ref, g_ref, sh_ref, sc_ref, w_ref, b_ref, o_ref, h_scr):
    @pl.when(pl.program_id(1) == 0)
    def _():
        h_scr[...] = _norm_mod(x_ref[...], g_ref[...], sh_ref[0], sc_ref[0]).astype(BF16)

    acc = jnp.dot(h_scr[...], w_ref[0].astype(BF16), preferred_element_type=F32)
    o_ref[...] = (acc + b_ref[0]).astype(o_ref.dtype)


def _inproj(x2, g, shift, scale, w_all, b_all, layer, seq):
    t, d = x2.shape
    depth, _, n = w_all.shape
    tm, tn = 1024, 1024
    per_b = seq // tm
    return pl.pallas_call(
        _inproj_kernel,
        grid=(t // tm, n // tn),
        in_specs=[
            pl.BlockSpec((tm, d), lambda i, j: (i, 0)),
            pl.BlockSpec((1, d), lambda i, j: (0, 0)),
            pl.BlockSpec((1, 1, d), lambda i, j: (i // per_b, 0, 0)),
            pl.BlockSpec((1, 1, d), lambda i, j: (i // per_b, 0, 0)),
            pl.BlockSpec((1, d, tn), lambda i, j: (layer, 0, j)),
            pl.BlockSpec((1, 1, tn), lambda i, j: (layer, 0, j)),
        ],
        out_specs=pl.BlockSpec((tm, tn), lambda i, j: (i, j)),
        out_shape=jax.ShapeDtypeStruct((t, n), BF16),
        scratch_shapes=[pltpu.VMEM((tm, d), BF16)],
        compiler_params=_params(("arbitrary", "arbitrary")),
    )(x2, g.reshape(1, d), shift, scale, w_all, b_all.reshape(depth, 1, n))


def _attn_kernel(sink_ref, q_ref, kc_ref, vc_ref, kp_ref, vp_ref, g_ref, wo_ref, o_ref):
    i = pl.program_id(1)
    nblk = q_ref.shape[0] // WINDOW
    kc, vc = kc_ref[...], vc_ref[...]
    k_parts = [kp_ref[...]] + [kc[s * WINDOW:(s + 1) * WINDOW] for s in range(nblk)]
    v_parts = [vp_ref[...]] + [vc[s * WINDOW:(s + 1) * WINDOW] for s in range(nblk)]
    row = lax.broadcasted_iota(I32, (WINDOW, 2 * WINDOW), 0)
    col = lax.broadcasted_iota(I32, (WINDOW, 2 * WINDOW), 1)
    band = (col > row) & (col <= row + WINDOW)
    inv_sqrt = 1.0 / math.sqrt(HEAD_DIM)
    blocks = []
    for s in range(nblk):
        blocks.append(_attn_block(sink_ref, q_ref[s * WINDOW:(s + 1) * WINDOW, :],
                                  jnp.concatenate(k_parts[s:s + 2], axis=0),
                                  jnp.concatenate(v_parts[s:s + 2], axis=0),
                                  band if s > 0 else band & ((col >= WINDOW) | (i > 0)), inv_sqrt))
    attn = jnp.concatenate(blocks, axis=0)
    proj = jnp.dot(attn, wo_ref[...], preferred_element_type=F32)
    o_ref[...] = (_sigmoid(g_ref[...].astype(F32)) * proj).astype(o_ref.dtype)


def _attn_block(sink_ref, q, k_cat, v_cat, mask, inv_sqrt):
    nk = 2 * WINDOW
    pair_w = 2 * HEAD_DIM
    zeros = jnp.zeros((nk, HEAD_DIM), k_cat.dtype)
    first_head = lax.broadcasted_iota(I32, (WINDOW, pair_w), 1) < HEAD_DIM

    def block_diag(x):
        return jnp.concatenate([jnp.concatenate([x, zeros], axis=1),
                                jnp.concatenate([zeros, x], axis=1)], axis=0)

    def softmax_terms(s, sink):
        s = jnp.where(mask, s * inv_sqrt, NEG_BIG)
        m = jnp.maximum(jnp.max(s, axis=-1, keepdims=True), sink)
        p = jnp.exp(s - m)
        return p.astype(BF16), jnp.sum(p, axis=-1, keepdims=True) + jnp.exp(sink - m)

    outs = []
    for j in range(N_KV_HEADS):
        k_bd = block_diag(k_cat[:, j * HEAD_DIM:(j + 1) * HEAD_DIM])
        v_bd = block_diag(v_cat[:, j * HEAD_DIM:(j + 1) * HEAD_DIM])
        for pair in range(Q_PER_KV // 2):
            h0 = j * Q_PER_KV + 2 * pair
            q2 = q[:, h0 * HEAD_DIM:(h0 + 2) * HEAD_DIM]
            s2 = lax.dot_general(q2, k_bd, (((1,), (1,)), ((), ())), preferred_element_type=F32)
            p_a, den_a = softmax_terms(s2[:, :nk], sink_ref[h0])
            p_b, den_b = softmax_terms(s2[:, nk:], sink_ref[h0 + 1])
            o2 = jnp.dot(jnp.concatenate([p_a, p_b], axis=1), v_bd, preferred_element_type=F32)
            outs.append(o2 / jnp.where(first_head, den_a, den_b))
    return jnp.concatenate(outs, axis=-1).astype(BF16)


def _attention(proj, sinks, w_o, batch, seq, d, gcol):
    t = proj.shape[0]
    per_step = 2
    tq = per_step * WINDOW
    nb = seq // tq
    kcol = ATTN_WIDTH // KV_WIDTH
    vcol = kcol + 1
    cur = lambda b, i: b * nb + i
    prev = lambda b, i: (b * nb + i) * per_step - jnp.minimum(i, 1)
    return pl.pallas_call(
        _attn_kernel,
        grid=(batch, nb),
        in_specs=[
            pl.BlockSpec(memory_space=pltpu.SMEM),
            pl.BlockSpec((tq, ATTN_WIDTH), lambda b, i: (cur(b, i), 0)),
            pl.BlockSpec((tq, KV_WIDTH), lambda b, i: (cur(b, i), kcol)),
            pl.BlockSpec((tq, KV_WIDTH), lambda b, i: (cur(b, i), vcol)),
            pl.BlockSpec((WINDOW, KV_WIDTH), lambda b, i: (prev(b, i), kcol)),
            pl.BlockSpec((WINDOW, KV_WIDTH), lambda b, i: (prev(b, i), vcol)),
            pl.BlockSpec((tq, d), lambda b, i: (cur(b, i), gcol)),
            _resident(w_o.shape),
        ],
        out_specs=pl.BlockSpec((tq, d), lambda b, i: (cur(b, i), 0)),
        out_shape=jax.ShapeDtypeStruct((t, d), BF16),
        compiler_params=_params(("arbitrary", "arbitrary")),
    )(sinks, proj, proj, proj, proj, proj, proj, w_o.astype(BF16))


def _ssm_kernel(u_ref, ar_ref, ai_ref, br_ref, bi_ref, cr_ref, cin_ref, d_ref, y_ref,
                uperm, bur, bui, yperm):
    seq = u_ref.shape[0]
    nseg = SSM_SEGMENTS
    n = seq // nseg
    ns = bur.shape[1]

    for r in range(nseg):
        uperm[pl.ds(r, n, stride=nseg), :] = u_ref[r * n:(r + 1) * n, :].astype(F32)

    chunk = 512

    def bu_body(c, _):
        r0 = pl.multiple_of(c * chunk, chunk)
        up = uperm[pl.ds(r0, chunk), :].astype(BF16)
        bur[pl.ds(r0, chunk), :] = (jnp.dot(up, br_ref[0, 0], preferred_element_type=F32)
                                    + jnp.dot(up, br_ref[0, 1], preferred_element_type=F32))
        bui[pl.ds(r0, chunk), :] = (jnp.dot(up, bi_ref[0, 0], preferred_element_type=F32)
                                    + jnp.dot(up, bi_ref[0, 1], preferred_element_type=F32))
        return 0

    lax.fori_loop(0, seq // chunk, bu_body, 0)

    a_r1 = ar_ref[0]
    a_i1 = ai_ref[0]
    a_r = jnp.broadcast_to(a_r1, (nseg, ns))
    a_i = jnp.broadcast_to(a_i1, (nseg, ns))

    def scan_body(s, carry):
        xr, xi = carry
        r0 = pl.multiple_of(s * nseg, nseg)
        nxr = a_r * xr - a_i * xi + bur[pl.ds(r0, nseg), :]
        nxi = a_r * xi + a_i * xr + bui[pl.ds(r0, nseg), :]
        bur[pl.ds(r0, nseg), :] = nxr
        bui[pl.ds(r0, nseg), :] = nxi
        return nxr, nxi

    zeros = jnp.zeros((nseg, ns), F32)
    er, ei = lax.fori_loop(0, n, scan_body, (zeros, zeros), unroll=4)

    pr, pi = a_r1, a_i1
    for _ in range(n.bit_length() - 1):
        pr, pi = pr * pr - pi * pi, 2.0 * pr * pi

    cr_rows = [jnp.zeros((1, ns), F32)]
    ci_rows = [jnp.zeros((1, ns), F32)]
    for r in range(1, nseg):
        pcr, pci = cr_rows[-1], ci_rows[-1]
        cr_rows.append(er[r - 1:r, :] + pr * pcr - pi * pci)
        ci_rows.append(ei[r - 1:r, :] + pr * pci + pi * pcr)
    c_r = jnp.concatenate(cr_rows, axis=0)
    c_i = jnp.concatenate(ci_rows, axis=0)

    def fix_body(s, carry):
        qr, qi = carry
        r0 = pl.multiple_of(s * nseg, nseg)
        bur[pl.ds(r0, nseg), :] = bur[pl.ds(r0, nseg), :] + (qr * c_r - qi * c_i)
        bui[pl.ds(r0, nseg), :] = bui[pl.ds(r0, nseg), :] + (qr * c_i + qi * c_r)
        return qr * a_r - qi * a_i, qr * a_i + qi * a_r

    lax.fori_loop(0, n, fix_body, (a_r, a_i), unroll=4)

    def out_body(c, _):
        r0 = pl.multiple_of(c * chunk, chunk)
        y = jnp.dot(bur[pl.ds(r0, chunk), :].astype(BF16), cr_ref[0], preferred_element_type=F32)
        y = y + jnp.dot(bui[pl.ds(r0, chunk), :].astype(BF16), cin_ref[0], preferred_element_type=F32)
        y = y + d_ref[0] * uperm[pl.ds(r0, chunk), :]
        yperm[pl.ds(r0, chunk), :] = jax.nn.gelu(y, approximate=True)
        return 0

    lax.fori_loop(0, seq // chunk, out_body, 0)

    for r in range(nseg):
        y_ref[r * n:(r + 1) * n, :] = yperm[pl.ds(r, n, stride=nseg), :].astype(y_ref.dtype)


def _ssm_params(a_re, a_im, log_dt, b_re, b_im, c_re, c_im, d_skip):
    g, p = a_re.shape
    c = b_re.shape[-1]
    gpb = SSM_GROUPS_PER_BLOCK
    nblk = g // gpb
    dt = jnp.exp(log_dt)[:, None]
    mag = jnp.exp(dt * a_re)
    ab_r, ab_i = mag * jnp.cos(dt * a_im), mag * jnp.sin(dt * a_im)
    den = a_re * a_re + a_im * a_im
    nr, ni = ab_r - 1.0, ab_i
    f_r = (nr * a_re + ni * a_im) / den
    f_i = (ni * a_re - nr * a_im) / den
    bb_r = f_r[..., None] * b_re - f_i[..., None] * b_im
    bb_i = f_r[..., None] * b_im + f_i[..., None] * b_re
    eye = jnp.eye(gpb, dtype=F32)

    def in_mat(bb):
        return jnp.einsum("bgpc,gh->bgchp", bb.reshape(nblk, gpb, p, c), eye).reshape(nblk, gpb * c, gpb * p)

    def out_mat(cc):
        return jnp.einsum("bgcp,gh->bgphc", cc.reshape(nblk, gpb, c, p), eye).reshape(nblk, gpb * p, gpb * c)

    def hi_lo(m):
        hi = m.astype(BF16)
        return jnp.stack([hi, (m - hi.astype(F32)).astype(BF16)], axis=1)

    return (ab_r.reshape(nblk, 1, gpb * p), ab_i.reshape(nblk, 1, gpb * p),
            hi_lo(in_mat(bb_r)), hi_lo(in_mat(bb_i)), out_mat(c_re).astype(BF16), out_mat(-c_im).astype(BF16),
            d_skip.reshape(nblk, 1, gpb * c))


def _ssm(proj, ssm_mats, batch, seq, ucol0):
    ar, ai, br, bi, cr, cin, dsk = ssm_mats
    nblk = ar.shape[0]
    nu, ns = br.shape[2], br.shape[3]
    ublk0 = ucol0 // nu
    blk3 = lambda shape: pl.BlockSpec((1,) + shape, lambda b, g: (g,) + (0,) * len(shape))
    return pl.pallas_call(
        _ssm_kernel,
        grid=(batch, nblk),
        in_specs=[
            pl.BlockSpec((seq, nu), lambda b, g: (b, ublk0 + g)),
            blk3((1, ns)), blk3((1, ns)), blk3((2, nu, ns)), blk3((2, nu, ns)),
            blk3((ns, nu)), blk3((ns, nu)), blk3((1, nu)),
        ],
        out_specs=pl.BlockSpec((seq, nu), lambda b, g: (b, g)),
        out_shape=jax.ShapeDtypeStruct((batch * seq, nblk * nu), BF16),
        scratch_shapes=[
            pltpu.VMEM((seq, nu), F32), pltpu.VMEM((seq, ns), F32),
            pltpu.VMEM((seq, ns), F32), pltpu.VMEM((seq, nu), F32),
        ],
        compiler_params=_params(("arbitrary", "arbitrary")),
    )(proj, ar, ai, br, bi, cr, cin, dsk)


def _mixout_kernel(a_ref, y_ref, gs_ref, x_ref, gt_ref, wglu_ref, wmix_ref, g2_ref, sh_ref, sc_ref,
                   wr_ref, x1_ref, lg_ref):
    d = x_ref.shape[1]
    y = y_ref[...]
    r = None
    nsplit = 2
    w = d // nsplit
    for c in range(nsplit):
        cols = slice(c * w, (c + 1) * w)
        za = jnp.dot(y, wglu_ref[:, c * w:(c + 1) * w], preferred_element_type=F32)
        zb = jnp.dot(y, wglu_ref[:, d + c * w:d + (c + 1) * w], preferred_element_type=F32)
        mixed = a_ref[:, cols].astype(F32) + _sigmoid(gs_ref[:, cols].astype(F32)) * (za * _sigmoid(zb))
        part = jnp.dot(mixed.astype(BF16), wmix_ref[c * w:(c + 1) * w, :], preferred_element_type=F32)
        r = part if r is None else r + part
    x1 = x_ref[...] + gt_ref[0] * r
    x1_ref[...] = x1
    h2 = _norm_mod(x1, g2_ref[...], sh_ref[0], sc_ref[0])
    h_hi = h2.astype(BF16)
    h_lo = (h2 - h_hi.astype(F32)).astype(BF16)
    nt = lambda a, b: lax.dot_general(a, b, (((1,), (1,)), ((), ())), preferred_element_type=F32)
    lg_ref[...] = nt(wr_ref[0], h_hi) + nt(wr_ref[0], h_lo) + nt(wr_ref[1], h_hi)


def _mixout(a_gated, y, proj, x2, gt1, w_glu, w_mix, g2, sh2, sc2, w_router_t, seq, gscol):
    t, d = x2.shape
    tm = 512
    per_b = seq // tm
    ne = w_router_t.shape[0]
    wr_hi = w_router_t.astype(BF16)
    wr_split = jnp.stack([wr_hi, (w_router_t - wr_hi.astype(F32)).astype(BF16)])
    row = lambda i: (i, 0)
    bvec = pl.BlockSpec((1, 1, d), lambda i: (i // per_b, 0, 0))
    return pl.pallas_call(
        _mixout_kernel,
        grid=(t // tm,),
        in_specs=[
            pl.BlockSpec((tm, d), row),
            pl.BlockSpec((tm, y.shape[1]), row),
            pl.BlockSpec((tm, d), lambda i: (i, gscol)),
            pl.BlockSpec((tm, d), row),
            bvec,
            _resident(w_glu.shape),
            _resident(w_mix.shape),
            pl.BlockSpec((1, d), lambda i: (0, 0)),
            bvec, bvec,
            _resident(wr_split.shape),
        ],
        out_specs=[
            pl.BlockSpec((tm, d), row),
            pl.BlockSpec((ne, tm), lambda i: (0, i)),
        ],
        out_shape=[
            jax.ShapeDtypeStruct((t, d), F32),
            jax.ShapeDtypeStruct((ne, t), F32),
        ],
        compiler_params=_params(("arbitrary",)),
    )(a_gated, y, proj, x2, gt1, w_glu.astype(BF16), w_mix.astype(BF16), g2.reshape(1, d), sh2, sc2,
      wr_split)


def _router_kernel(lg_ref, b_ref, idx_ref, w_ref, rank_ref, cnt_ref, run_scr):
    ng, gs, tn = lg_ref.shape
    ne = ng * gs

    @pl.when(pl.program_id(0) == 0)
    def _():
        run_scr[...] = jnp.zeros(run_scr.shape, run_scr.dtype)

    scores = _sigmoid(lg_ref[...])
    sel = scores + b_ref[...]
    mem = lax.broadcasted_iota(I32, (ng, gs, tn), 1)
    grp = lax.broadcasted_iota(I32, (ng, gs, tn), 0)
    eid = grp * gs + mem
    ninf = -jnp.inf

    m1 = jnp.max(sel, axis=1, keepdims=True)
    first = jnp.min(jnp.where(sel == m1, mem, gs), axis=1, keepdims=True)
    m2 = jnp.max(jnp.where(mem == first, ninf, sel), axis=1, keepdims=True)
    gscore = m1 + m2

    gid = lax.broadcasted_iota(I32, (ng, 1, tn), 0)
    gmask = jnp.zeros((ng, 1, tn), jnp.bool_)
    for _ in range(TOPK_GROUPS):
        m = jnp.max(gscore, axis=0, keepdims=True)
        pick = gid == jnp.min(jnp.where(gscore == m, gid, ng), axis=0, keepdims=True)
        gmask = gmask | pick
        gscore = jnp.where(pick, ninf, gscore)

    def reduce_experts(v):
        return jnp.sum(jnp.sum(v, axis=1, keepdims=True), axis=0, keepdims=True)[0]

    cand = jnp.where(gmask, sel, ninf)
    idxs, wts, picks = [], [], []
    for _ in range(TOP_K):
        m = jnp.max(jnp.max(cand, axis=1, keepdims=True), axis=0, keepdims=True)
        e = jnp.where(cand == m, eid, ne)
        e = jnp.min(jnp.min(e, axis=1, keepdims=True), axis=0, keepdims=True)
        pick = eid == e
        idxs.append(e[0])
        wts.append(reduce_experts(jnp.where(pick, scores, 0.0)))
        picks.append(pick)
        cand = jnp.where(pick, ninf, cand)
    idx_ref[...] = jnp.concatenate(idxs, axis=0)
    w = jnp.concatenate(wts, axis=0)
    w_ref[...] = w / jnp.sum(w, axis=0, keepdims=True) * ROUTED_SCALE

    chosen = picks[0]
    for pick in picks[1:]:
        chosen = chosen | pick
    onehot = chosen.astype(F32).reshape(ne, tn)
    before = (lax.broadcasted_iota(I32, (tn, tn), 0) < lax.broadcasted_iota(I32, (tn, tn), 1)).astype(BF16)
    prefix = jnp.dot(onehot.astype(BF16), before, preferred_element_type=F32)
    rank = (run_scr[...] + prefix).reshape(ng, gs, tn)
    ranks = [reduce_experts(jnp.where(pick, rank, 0.0)) for pick in picks]
    rank_ref[...] = jnp.concatenate(ranks, axis=0).astype(I32)
    run_scr[...] = run_scr[...] + jnp.sum(onehot, axis=1, keepdims=True)
    cnt_ref[...] = run_scr[...].astype(I32)


def _router(logits_t, b_router):
    ne, t = logits_t.shape
    ng = N_EXPERT_GROUPS
    gs = ne // ng
    tn = 512
    tok = pl.BlockSpec((TOP_K, tn), lambda i: (0, i))
    return pl.pallas_call(
        _router_kernel,
        grid=(t // tn,),
        in_specs=[
            pl.BlockSpec((ng, gs, tn), lambda i: (0, 0, i)),
            pl.BlockSpec((ng, gs, 1), lambda i: (0, 0, 0)),
        ],
        out_specs=[tok, tok, tok, pl.BlockSpec((ne, 1), lambda i: (0, 0))],
        out_shape=[jax.ShapeDtypeStruct((TOP_K, t), I32), jax.ShapeDtypeStruct((TOP_K, t), F32),
                   jax.ShapeDtypeStruct((TOP_K, t), I32), jax.ShapeDtypeStruct((ne, 1), I32)],
        scratch_shapes=[pltpu.VMEM((ne, 1), F32)],
        compiler_params=_params(("arbitrary",)),
    )(logits_t.reshape(ng, gs, t), b_router.reshape(ng, gs, 1))


def _dest_kernel(pstart_ref, idx_ref, rank_ref, dest_ref):
    idx = idx_ref[...]
    base = jnp.zeros(idx.shape, I32)
    for e in range(pstart_ref.shape[0]):
        base = jnp.where(idx == e, pstart_ref[e], base)
    dest_ref[...] = base + rank_ref[...]


def _dest(pstart, eidx, rank):
    k, t = eidx.shape
    tn = 2048
    tok = pl.BlockSpec((k, tn), lambda i, ps: (0, i))
    return pl.pallas_call(
        _dest_kernel,
        grid_spec=pltpu.PrefetchScalarGridSpec(
            num_scalar_prefetch=1, grid=(t // tn,), in_specs=[tok, tok], out_specs=tok),
        out_shape=jax.ShapeDtypeStruct((k, t), I32),
        compiler_params=_params(("arbitrary",)),
    )(pstart, eidx, rank)


def _slot_plan(counts):
    padded = (counts + MOE_SUB - 1) // MOE_SUB * MOE_SUB
    pend = jnp.cumsum(padded)
    pstart = pend - padded
    npad = (padded - counts)[:, None]
    ntail = MOE_SUB - npad
    tail0 = pend[-1] + jnp.cumsum(ntail, axis=0) - ntail
    j = jnp.arange(MOE_SUB, dtype=I32)[None, :]
    free_rows = jnp.where(j < npad, (pstart + counts)[:, None] + j, tail0 + j - npad)
    return jnp.stack([pstart, padded]).astype(I32), pstart.astype(I32), free_rows.astype(I32)


def _dispatch_kernel(dest_ref, pad_ref, x_ref, g_ref, sh_ref, sc_ref, wg_ref, wu_ref, wd_ref, xs_hbm, shared_ref,
                     h_scr, zero_scr, sem):
    _, ntile, sub, dw = h_scr.shape
    tm = ntile * sub
    k = dest_ref.shape[2] // tm
    npad = pad_ref.shape[2]
    assert npad % tm == 0
    i = pl.program_id(0)
    b = i & 1

    @pl.when(i == 0)
    def _():
        zero_scr[...] = _pack_halves(jnp.zeros((zero_scr.shape[0], 2 * dw), F32))

    h = _norm_mod(x_ref[...], g_ref[...], sh_ref[0], sc_ref[0])
    h_scr[b] = _pack_halves(h).reshape(ntile, sub, dw)

    def issue_rows(r8):
        for j in range(sub):
            for kk in range(k):
                slot = dest_ref[0, 0, (r8 * sub + j) * k + kk]
                pltpu.make_async_copy(h_scr.at[b, r8, pl.ds(j, 1)], xs_hbm.at[pl.ds(slot, 1)],
                                      sem.at[b]).start(priority=kk % 2)

    per_group = sub * k

    def issue_pad(g):
        for j in range(g * per_group, (g + 1) * per_group):
            pltpu.make_async_copy(zero_scr.at[pl.ds(0, 1)], xs_hbm.at[pl.ds(pad_ref[0, 0, j], 1)],
                                  sem.at[b]).start(priority=j % 2)

    groups = [functools.partial(issue_rows, r8) for r8 in range(ntile)]
    groups += [functools.partial(issue_pad, g) for g in range(npad // per_group)]
    groups = iter(groups)

    def issue_some(n):
        for _ in range(n):
            g = next(groups, None)
            if g is not None:
                g()

    hb = h.astype(BF16)
    f = wg_ref.shape[1]
    d = wd_ref.shape[1]
    gu_pieces = f // LANES
    wn = 2 * LANES
    down_pieces = d // wn
    per_piece = -(-(ntile + npad // per_group) // (gu_pieces + down_pieces))
    hid = []
    for c in range(gu_pieces):
        issue_some(per_piece)
        cols = slice(c * LANES, (c + 1) * LANES)
        gate = jnp.dot(hb, wg_ref[:, cols], preferred_element_type=F32)
        up = jnp.dot(hb, wu_ref[:, cols], preferred_element_type=F32)
        hid.append((_silu(gate) * up).astype(BF16))
    hid = jnp.concatenate(hid, axis=-1)
    for c in range(down_pieces):
        issue_some(per_piece)
        shared_ref[:, c * wn:(c + 1) * wn] = jnp.dot(hid, wd_ref[:, c * wn:(c + 1) * wn],
                                                     preferred_element_type=F32)
    issue_some(ntile + npad)

    def drain(bb):
        for _ in range(k + npad // tm):
            pltpu.make_async_copy(xs_hbm.at[pl.ds(0, tm)], xs_hbm.at[pl.ds(0, tm)], sem.at[bb]).wait()

    @pl.when(i > 0)
    def _():
        drain(1 - b)

    @pl.when(i + 1 == pl.num_programs(0))
    def _():
        drain(b)


def _dispatch(dest, free_rows, x1, g2, sh2, sc2, w_sg, w_su, w_sd, n_rows, seq):
    t, d = x1.shape
    k = dest.shape[0]
    tm = 256
    per_b = seq // tm
    nblk = t // tm
    npad = free_rows.size // nblk
    dest_tm = dest.T.reshape(nblk, 1, tm * k)
    bvec = pl.BlockSpec((1, 1, d), lambda i: (i // per_b, 0, 0))
    return pl.pallas_call(
        _dispatch_kernel,
        grid=(nblk,),
        in_specs=[
            pl.BlockSpec((1, 1, tm * k), lambda i: (i, 0, 0), memory_space=pltpu.SMEM),
            pl.BlockSpec((1, 1, npad), lambda i: (i, 0, 0), memory_space=pltpu.SMEM),
            pl.BlockSpec((tm, d), lambda i: (i, 0)),
            pl.BlockSpec((1, d), lambda i: (0, 0)),
            bvec, bvec,
            _resident(w_sg.shape), _resident(w_su.shape), _resident(w_sd.shape),
        ],
        out_specs=[pl.BlockSpec(memory_space=pl.ANY), pl.BlockSpec((tm, d), lambda i: (i, 0))],
        out_shape=[jax.ShapeDtypeStruct((n_rows, d // 2), U32), jax.ShapeDtypeStruct((t, d), F32)],
        scratch_shapes=[pltpu.VMEM((2, tm // SUBLANES, SUBLANES, d // 2), U32),
                        pltpu.VMEM((SUBLANES, d // 2), U32), pltpu.SemaphoreType.DMA((2,))],
        compiler_params=_params(("arbitrary",)),
    )(dest_tm, free_rows.reshape(nblk, 1, npad), x1, g2.reshape(1, d), sh2, sc2,
      w_sg.astype(BF16), w_su.astype(BF16), w_sd.astype(BF16))


def _chunk_plan(meta, chunk, n_chunks):
    pstart, padded = meta[0], meta[1]
    ne = pstart.shape[0]
    nch = (padded + chunk - 1) // chunk
    cend = jnp.cumsum(nch)
    cfirst = cend - nch
    j = jnp.arange(n_chunks, dtype=I32)
    e_of = jnp.sum((cend[None, :] <= j[:, None]).astype(I32), axis=1)
    valid = e_of < ne
    e_c = jnp.minimum(e_of, ne - 1)
    local = j - cfirst[e_c]
    rows = jnp.clip(padded[e_c] - local * chunk, 0, chunk)
    cmeta = jnp.stack([jnp.where(valid, pstart[e_c] + local * chunk, 0), jnp.where(valid, rows // MOE_SUB, 0)])
    return cmeta.astype(I32), jnp.stack([cfirst, nch]).astype(I32)


def _expert_kernel(meta_ref, cmeta_ref, emeta_ref, x_hbm, wg_ref, wu_ref, wd_ref, y_hbm, xbuf, ybuf,
                   xsem, ysem, ypend):
    e = pl.program_id(0)
    ne = pl.num_programs(0)
    chunk = xbuf.shape[1]
    nsub = chunk // MOE_SUB

    def sub(r):
        return pl.ds(pl.multiple_of(r, MOE_SUB), MOE_SUB)

    def x_piece(r0, slot, j):
        return pltpu.make_async_copy(x_hbm.at[sub(r0 + j * MOE_SUB)], xbuf.at[slot, sub(j * MOE_SUB)],
                                     xsem.at[slot])

    def y_piece(r0, slot, j):
        return pltpu.make_async_copy(ybuf.at[slot, sub(j * MOE_SUB)], y_hbm.at[sub(r0 + j * MOE_SUB)],
                                     ysem.at[slot])

    def x_start(r0, slot, n):
        def body(j, carry):
            x_piece(r0, slot, j).start()
            return carry

        lax.fori_loop(0, n, body, 0)

    def x_wait(slot, n):
        def body(j, carry):
            x_piece(0, slot, 0).wait()
            return carry

        lax.fori_loop(0, n, body, 0)

    def y_drain(slot):
        def body(j, carry):
            y_piece(0, slot, 0).wait()
            return carry

        lax.fori_loop(0, ypend[slot], body, 0)
        ypend[slot] = 0

    nx = xbuf.shape[0]
    ahead = nx - 1

    def request(g):
        x_start(cmeta_ref[0, g], lax.rem(g, nx), cmeta_ref[1, g])

    @pl.when(e == 0)
    def _():
        ypend[0] = 0
        ypend[1] = 0
        for a in range(ahead):
            request(a)

    dw = xbuf.shape[2]

    def compute(xslot, yslot, m):
        x_l, x_r = (v.astype(BF16) for v in _unpack_halves(xbuf[xslot, pl.ds(0, m), :]))

        def proj(w_ref):
            return (jnp.dot(x_l, w_ref[0, 0, :dw, :].astype(BF16), preferred_element_type=F32)
                    + jnp.dot(x_r, w_ref[0, 0, dw:, :].astype(BF16), preferred_element_type=F32))

        hid = (_silu(proj(wg_ref)) * proj(wu_ref)).astype(BF16)
        ybuf[yslot, pl.ds(0, m), :] = _pack_halves(
            jnp.dot(hid, wd_ref[0, 0].astype(BF16), preferred_element_type=F32))

    def run_chunk(g, npieces):
        xslot = lax.rem(g, nx)
        yslot = g & 1
        r0 = cmeta_ref[0, g]
        x_wait(xslot, npieces)
        request(g + ahead)
        y_drain(yslot)
        compute(xslot, yslot, npieces * MOE_SUB)
        for j in range(npieces):
            y_piece(r0, yslot, j).start()
        ypend[yslot] = npieces

    def chunk_body(c, carry):
        g = emeta_ref[0, e] + c
        for npieces in range(1, nsub + 1):
            @pl.when(cmeta_ref[1, g] == npieces)
            def _(npieces=npieces):
                run_chunk(g, npieces)
        return carry

    lax.fori_loop(0, emeta_ref[1, e], chunk_body, 0)

    @pl.when(e + 1 == ne)
    def _():
        y_drain(0)
        y_drain(1)
        tail0 = meta_ref[0, e] + meta_ref[1, e]
        ntail = (y_hbm.shape[0] - tail0) // MOE_SUB
        ybuf[0, pl.ds(0, MOE_SUB), :] = _pack_halves(jnp.zeros((MOE_SUB, 2 * dw), F32))

        def fill(j, carry):
            y_piece(tail0 + j * MOE_SUB, 0, 0).start()
            return carry

        lax.fori_loop(0, ntail, fill, 0)
        ypend[0] = ntail
        y_drain(0)


def _experts(meta, x_sorted, w_gate, w_up, w_down, layer, n_slots):
    dw = x_sorted.shape[1]
    _, ne, d, f = w_gate.shape
    chunk = 512
    n_chunks = n_slots // chunk + ne + 3
    cmeta, emeta = _chunk_plan(meta, chunk, n_chunks)
    wspec = lambda a, b: pl.BlockSpec((1, 1, a, b), lambda e, m, cm, em: (layer, e, 0, 0))
    grid_spec = pltpu.PrefetchScalarGridSpec(
        num_scalar_prefetch=3,
        grid=(ne,),
        in_specs=[pl.BlockSpec(memory_space=pl.ANY), wspec(d, f), wspec(d, f), wspec(f, d)],
        out_specs=pl.BlockSpec(memory_space=pl.ANY),
        scratch_shapes=[
            pltpu.VMEM((3, chunk, dw), U32), pltpu.VMEM((2, chunk, dw), U32),
            pltpu.SemaphoreType.DMA((3,)), pltpu.SemaphoreType.DMA((2,)),
            pltpu.SMEM((2,), I32),
        ],
    )
    return pl.pallas_call(
        _expert_kernel,
        grid_spec=grid_spec,
        out_shape=jax.ShapeDtypeStruct((n_slots, dw), U32),
        compiler_params=_params(("arbitrary",)),
    )(meta, cmeta, emeta, x_sorted, w_gate, w_up, w_down)


def _combine_kernel(dest_ref, dnext_ref, wt_ref, x_ref, shared_ref, gt_ref, gf_ref, y_hbm, o_ref, buf, sem, *,
                    final_norm):
    _, k, ntile, sub, dw = buf.shape
    tm = ntile * sub
    i = pl.program_id(0)
    cur = i & 1
    nxt = 1 - cur
    has_next = i + 1 < pl.num_programs(0)

    def issue_group(idx_ref, b, r8):
        for j in range(sub):
            for kk in range(k):
                slot = idx_ref[0, 0, (r8 * sub + j) * k + kk]
                pltpu.make_async_copy(y_hbm.at[pl.ds(slot, 1)], buf.at[b, kk, r8, pl.ds(j, 1)],
                                      sem.at[b]).start(priority=kk % 2)

    @pl.when(i == 0)
    def _():
        def body(r8, carry):
            issue_group(dest_ref, 0, r8)
            return carry

        lax.fori_loop(0, ntile, body, 0)

    def issue(r8):
        issue_group(dnext_ref, nxt, r8)

    def drain(b):
        for kk in range(k):
            pltpu.make_async_copy(buf.at[b, kk], buf.at[b, kk], sem.at[b]).wait()

    groups = iter(range(ntile))
    for _ in range(ntile - k):
        issue(next(groups))
    drain(cur)
    wt = wt_ref[...]
    routed_l = routed_r = None
    for kk in range(k):
        r8 = next(groups, None)
        if r8 is not None:
            issue(r8)
        y_l, y_r = _unpack_halves(buf[cur, kk].reshape(tm, dw))
        w_k = wt[:, kk:kk + 1]
        routed_l = w_k * y_l if routed_l is None else routed_l + w_k * y_l
        routed_r = w_k * y_r if routed_r is None else routed_r + w_k * y_r
    for r8 in groups:
        issue(r8)
    acc = shared_ref[...] + jnp.concatenate([routed_l, routed_r], axis=-1)

    @pl.when(jnp.logical_not(has_next))
    def _():
        drain(nxt)
    out = x_ref[...] + gt_ref[0] * acc
    if final_norm:
        ms = jnp.mean(out * out, axis=-1, keepdims=True)
        out = out * lax.rsqrt(ms + NORM_EPS) * gf_ref[...]
    o_ref[...] = out


def _combine(dest, wts, x1, shared, gt2, g_final, y_sorted, seq, final_norm):
    t, d = x1.shape
    k = dest.shape[0]
    tm = 128
    per_b = seq // tm
    nblk = t // tm
    dest_tm = dest.T.reshape(nblk, 1, tm * k)
    row = lambda i: (i, 0)
    vec = pl.BlockSpec((1, d), lambda i: (0, 0))
    bvec = pl.BlockSpec((1, 1, d), lambda i: (i // per_b, 0, 0))
    return pl.pallas_call(
        functools.partial(_combine_kernel, final_norm=final_norm),
        grid=(nblk,),
        in_specs=[
            pl.BlockSpec((1, 1, tm * k), lambda i: (i, 0, 0), memory_space=pltpu.SMEM),
            pl.BlockSpec((1, 1, tm * k), lambda i: (jnp.minimum(i + 1, nblk - 1), 0, 0),
                         memory_space=pltpu.SMEM),
            pl.BlockSpec((tm, k), row),
            pl.BlockSpec((tm, d), row),
            pl.BlockSpec((tm, d), row),
            bvec, vec,
            pl.BlockSpec(memory_space=pl.ANY),
        ],
        out_specs=pl.BlockSpec((tm, d), row),
        out_shape=jax.ShapeDtypeStruct((t, d), F32),
        scratch_shapes=[pltpu.VMEM((2, k, tm // SUBLANES, SUBLANES, d // 2), U32),
                        pltpu.SemaphoreType.DMA((2,))],
        compiler_params=_params(("arbitrary",)),
    )(dest_tm, dest_tm, wts.T, x1, shared, gt2, g_final.reshape(1, d), y_sorted)


def kernel(x, c, w_ada, b_ada, norm_mix, w_in, b_in, attn_sinks, w_attn_o, ssm_a_re, ssm_a_im, ssm_log_dt,
           ssm_b_re, ssm_b_im, ssm_c_re, ssm_c_im, ssm_d, w_glu, w_mix_out, norm_ffn, w_router, b_router,
           w_exp_gate, w_exp_up, w_exp_down, w_sh_gate, w_sh_up, w_sh_down, norm_final):
    batch, seq, d = x.shape
    depth = w_ada.shape[0]
    n_experts = w_router.shape[2]
    ssm_width = ssm_d.shape[1]
    ucol0 = ATTN_WIDTH + 2 * KV_WIDTH
    gacol = (ucol0 + ssm_width) // d
    gscol = gacol + 1
    n_slots = batch * seq * TOP_K + n_experts * MOE_SUB

    c_pad = jnp.zeros((SUBLANES, d), F32).at[:batch].set(c)
    mod = _ada_mod(c_pad, w_ada, b_ada)[:, :batch]
    mod = mod.reshape(depth, batch, 6, 1, d)

    x2 = x.reshape(batch * seq, d)
    for l in range(depth):
        sh1, sc1, gt1, sh2, sc2, gt2 = (mod[l, :, m] for m in range(6))
        proj = _inproj(x2, norm_mix[l], sh1, sc1, w_in, b_in, l, seq)
        a_gated = _attention(proj, attn_sinks[l], w_attn_o[l], batch, seq, d, gacol)
        ssm_mats = _ssm_params(ssm_a_re[l], ssm_a_im[l], ssm_log_dt[l], ssm_b_re[l], ssm_b_im[l],
                               ssm_c_re[l], ssm_c_im[l], ssm_d[l])
        y = _ssm(proj, ssm_mats, batch, seq, ucol0)
        x1, logits_t = _mixout(a_gated, y, proj, x2, gt1, w_glu[l], w_mix_out[l], norm_ffn[l],
                               sh2, sc2, w_router[l].T, seq, gscol)
        eidx, wts, rank, counts = _router(logits_t, b_router[l])
        meta, pstart, free_rows = _slot_plan(counts[:, 0])
        dest = _dest(pstart, eidx, rank)
        x_sorted, shared = _dispatch(dest, free_rows, x1, norm_ffn[l], sh2, sc2,
                                     w_sh_gate[l], w_sh_up[l], w_sh_down[l], n_slots, seq)
        y_sorted = _experts(meta, x_sorted, w_exp_gate, w_exp_up, w_exp_down, l, n_slots)
        x2 = _combine(dest, wts, x1, shared, gt2, norm_final, y_sorted, seq, final_norm=(l == depth - 1))
    return x2.reshape(batch, seq, d)
```

```python
import functools
import math

import jax
import jax.numpy as jnp
from jax import lax
from jax.experimental import pallas as pl
from jax.experimental.pallas import tpu as pltpu

F32 = jnp.float32
BF16 = jnp.bfloat16
I32 = jnp.int32
U32 = jnp.uint32

N_Q_HEADS = 16
N_KV_HEADS = 4
HEAD_DIM = 64
Q_PER_KV = N_Q_HEADS // N_KV_HEADS
WINDOW = 128
ATTN_WIDTH = N_Q_HEADS * HEAD_DIM
KV_WIDTH = N_KV_HEADS * HEAD_DIM
SSM_GROUP_CH = 16
SSM_STATE = 64
N_EXPERT_GROUPS = 8
TOPK_GROUPS = 4
TOP_K = 8
ROUTED_SCALE = 2.5
NORM_EPS = 1e-6

LANES = 128
SUBLANES = 8
VMEM_LIMIT_BYTES = 56 * 1024 * 1024

SSM_GROUPS_PER_BLOCK = LANES // SSM_GROUP_CH
SSM_SEGMENTS = SUBLANES
MOE_SUB = 128
NEG_BIG = -1e30


def _sigmoid(x):
    return 1.0 / (1.0 + jnp.exp(-x))


def _silu(x):
    return x * _sigmoid(x)


def _pack_halves(x):
    n = x.shape[1] // 2
    return pltpu.pack_elementwise([x[:, :n], x[:, n:]], packed_dtype=BF16)


def _unpack_halves(w):
    return tuple(pltpu.unpack_elementwise(w, index=i, packed_dtype=BF16, unpacked_dtype=F32) for i in range(2))


def _params(sem):
    return pltpu.CompilerParams(dimension_semantics=sem, vmem_limit_bytes=VMEM_LIMIT_BYTES)


def _resident(shape):
    nd = len(shape)
    return pl.BlockSpec(shape, lambda *_: (0,) * nd, pipeline_mode=pl.Buffered(1))


def _ada_kernel(c_ref, w_ref, b_ref, o_ref):
    c_act = _silu(c_ref[...]).astype(BF16)
    o_ref[0] = jnp.dot(c_act, w_ref[0].astype(BF16), preferred_element_type=F32) + b_ref[0]


def _ada_mod(c_pad, w_ada, b_ada):
    depth, d, n = w_ada.shape
    tn = 1024
    return pl.pallas_call(
        _ada_kernel,
        grid=(depth, n // tn),
        in_specs=[
            pl.BlockSpec((SUBLANES, d), lambda l, j: (0, 0)),
            pl.BlockSpec((1, d, tn), lambda l, j: (l, 0, j)),
            pl.BlockSpec((1, 1, tn), lambda l, j: (l, 0, j)),
        ],
        out_specs=pl.BlockSpec((1, SUBLANES, tn), lambda l, j: (l, 0, j)),
        out_shape=jax.ShapeDtypeStruct((depth, SUBLANES, n), F32),
        compiler_params=_params(("arbitrary", "arbitrary")),
    )(c_pad, w_ada, b_ada.reshape(depth, 1, n))


def _norm_mod(x, g, shift, scale):
    ms = jnp.mean(x * x, axis=-1, keepdims=True)
    xn = x * lax.rsqrt(ms + NORM_EPS) * g
    return xn * (1.0 + scale) + shift


def _inproj_kernel(x_ref, g_ref, sh_ref, sc_ref, w_ref, b_ref, o_ref, h_scr):
    @pl.when(pl.program_id(1) == 0)
    def _():
        h_scr[...] = _norm_mod(x_ref[...], g_ref[...], sh_ref[0], sc_ref[0]).astype(BF16)

    acc = jnp.dot(h_scr[...], w_ref[0].astype(BF16), preferred_element_type=F32)
    o_ref[...] = (acc + b_ref[0]).astype(o_ref.dtype)


def _inproj(x2, g, shift, scale, w_all, b_all, layer, seq):
    t, d = x2.shape
    depth, _, n = w_all.shape
    tm, tn = 1024, 1024
    per_b = seq // tm
    return pl.pallas_call(
        _inproj_kernel,
        grid=(t // tm, n // tn),
        in_specs=[
            pl.BlockSpec((tm, d), lambda i, j: (i, 0)),
            pl.BlockSpec((1, d), lambda i, j: (0, 0)),
            pl.BlockSpec((1, 1, d), lambda i, j: (i // per_b, 0, 0)),
            pl.BlockSpec((1, 1, d), lambda i, j: (i // per_b, 0, 0)),
            pl.BlockSpec((1, d, tn), lambda i, j: (layer, 0, j)),
            pl.BlockSpec((1, 1, tn), lambda i, j: (layer, 0, j)),
        ],
        out_specs=pl.BlockSpec((tm, tn), lambda i, j: (i, j)),
        out_shape=jax.ShapeDtypeStruct((t, n), BF16),
        scratch_shapes=[pltpu.VMEM((tm, d), BF16)],
        compiler_params=_params(("arbitrary", "arbitrary")),
    )(x2, g.reshape(1, d), shift, scale, w_all, b_all.reshape(depth, 1, n))


def _attn_kernel(sink_ref, q_ref, kc_ref, vc_ref, kp_ref, vp_ref, g_ref, wo_ref, o_ref):
    i = pl.program_id(1)
    nblk = q_ref.shape[0] // WINDOW
    kc, vc = kc_ref[...], vc_ref[...]
    k_parts = [kp_ref[...]] + [kc[s * WINDOW:(s + 1) * WINDOW] for s in range(nblk)]
    v_parts = [vp_ref[...]] + [vc[s * WINDOW:(s + 1) * WINDOW] for s in range(nblk)]
    row = lax.broadcasted_iota(I32, (WINDOW, 2 * WINDOW), 0)
    col = lax.broadcasted_iota(I32, (WINDOW, 2 * WINDOW), 1)
    band = (col > row) & (col <= row + WINDOW)
    inv_sqrt = 1.0 / math.sqrt(HEAD_DIM)
    blocks = []
    for s in range(nblk):
        blocks.append(_attn_block(sink_ref, q_ref[s * WINDOW:(s + 1) * WINDOW, :],
                                  jnp.concatenate(k_parts[s:s + 2], axis=0),
                                  jnp.concatenate(v_parts[s:s + 2], axis=0),
                                  band if s > 0 else band & ((col >= WINDOW) | (i > 0)), inv_sqrt))
    attn = jnp.concatenate(blocks, axis=0)
    proj = jnp.dot(attn, wo_ref[...], preferred_element_type=F32)
    o_ref[...] = (_sigmoid(g_ref[...].astype(F32)) * proj).astype(o_ref.dtype)


def _attn_block(sink_ref, q, k_cat, v_cat, mask, inv_sqrt):
    nk = 2 * WINDOW
    pair_w = 2 * HEAD_DIM
    zeros = jnp.zeros((nk, HEAD_DIM), k_cat.dtype)
    first_head = lax.broadcasted_iota(I32, (WINDOW, pair_w), 1) < HEAD_DIM

    def block_diag(x):
        return jnp.concatenate([jnp.concatenate([x, zeros], axis=1),
                                jnp.concatenate([zeros, x], axis=1)], axis=0)

    def softmax_terms(s, sink):
        s = jnp.where(mask, s * inv_sqrt, NEG_BIG)
        m = jnp.maximum(jnp.max(s, axis=-1, keepdims=True), sink)
        p = jnp.exp(s - m)
        return p.astype(BF16), jnp.sum(p, axis=-1, keepdims=True) + jnp.exp(sink - m)

    outs = []
    for j in range(N_KV_HEADS):
        k_bd = block_diag(k_cat[:, j * HEAD_DIM:(j + 1) * HEAD_DIM])
        v_bd = block_diag(v_cat[:, j * HEAD_DIM:(j + 1) * HEAD_DIM])
        for pair in range(Q_PER_KV // 2):
            h0 = j * Q_PER_KV + 2 * pair
            q2 = q[:, h0 * HEAD_DIM:(h0 + 2) * HEAD_DIM]
            s2 = lax.dot_general(q2, k_bd, (((1,), (1,)), ((), ())), preferred_element_type=F32)
            p_a, den_a = softmax_terms(s2[:, :nk], sink_ref[h0])
            p_b, den_b = softmax_terms(s2[:, nk:], sink_ref[h0 + 1])
            o2 = jnp.dot(jnp.concatenate([p_a, p_b], axis=1), v_bd, preferred_element_type=F32)
            outs.append(o2 / jnp.where(first_head, den_a, den_b))
    return jnp.concatenate(outs, axis=-1).astype(BF16)


def _attention(proj, sinks, w_o, batch, seq, d, gcol):
    t = proj.shape[0]
    per_step = 2
    tq = per_step * WINDOW
    nb = seq // tq
    kcol = ATTN_WIDTH // KV_WIDTH
    vcol = kcol + 1
    cur = lambda b, i: b * nb + i
    prev = lambda b, i: (b * nb + i) * per_step - jnp.minimum(i, 1)
    return pl.pallas_call(
        _attn_kernel,
        grid=(batch, nb),
        in_specs=[
            pl.BlockSpec(memory_space=pltpu.SMEM),
            pl.BlockSpec((tq, ATTN_WIDTH), lambda b, i: (cur(b, i), 0)),
            pl.BlockSpec((tq, KV_WIDTH), lambda b, i: (cur(b, i), kcol)),
            pl.BlockSpec((tq, KV_WIDTH), lambda b, i: (cur(b, i), vcol)),
            pl.BlockSpec((WINDOW, KV_WIDTH), lambda b, i: (prev(b, i), kcol)),
            pl.BlockSpec((WINDOW, KV_WIDTH), lambda b, i: (prev(b, i), vcol)),
            pl.BlockSpec((tq, d), lambda b, i: (cur(b, i), gcol)),
            _resident(w_o.shape),
        ],
        out_specs=pl.BlockSpec((tq, d), lambda b, i: (cur(b, i), 0)),
        out_shape=jax.ShapeDtypeStruct((t, d), BF16),
        compiler_params=_params(("arbitrary", "arbitrary")),
    )(sinks, proj, proj, proj, proj, proj, proj, w_o.astype(BF16))


def _ssm_kernel(u_ref, ar_ref, ai_ref, br_ref, bi_ref, cr_ref, cin_ref, d_ref, y_ref,
                uperm, bur, bui, yperm):
    seq = u_ref.shape[0]
    nseg = SSM_SEGMENTS
    n = seq // nseg
    ns = bur.shape[1]

    for r in range(nseg):
        uperm[pl.ds(r, n, stride=nseg), :] = u_ref[r * n:(r + 1) * n, :].astype(F32)

    chunk = 512

    def bu_body(c, _):
        r0 = pl.multiple_of(c * chunk, chunk)
        up = uperm[pl.ds(r0, chunk), :].astype(BF16)
        bur[pl.ds(r0, chunk), :] = (jnp.dot(up, br_ref[0, 0], preferred_element_type=F32)
                                    + jnp.dot(up, br_ref[0, 1], preferred_element_type=F32))
        bui[pl.ds(r0, chunk), :] = (jnp.dot(up, bi_ref[0, 0], preferred_element_type=F32)
                                    + jnp.dot(up, bi_ref[0, 1], preferred_element_type=F32))
        return 0

    lax.fori_loop(0, seq // chunk, bu_body, 0)

    a_r1 = ar_ref[0]
    a_i1 = ai_ref[0]
    a_r = jnp.broadcast_to(a_r1, (nseg, ns))
    a_i = jnp.broadcast_to(a_i1, (nseg, ns))

    def scan_body(s, carry):
        xr, xi = carry
        r0 = pl.multiple_of(s * nseg, nseg)
        nxr = a_r * xr - a_i * xi + bur[pl.ds(r0, nseg), :]
        nxi = a_r * xi + a_i * xr + bui[pl.ds(r0, nseg), :]
        bur[pl.ds(r0, nseg), :] = nxr
        bui[pl.ds(r0, nseg), :] = nxi
        return nxr, nxi

    zeros = jnp.zeros((nseg, ns), F32)
    er, ei = lax.fori_loop(0, n, scan_body, (zeros, zeros), unroll=4)

    pr, pi = a_r1, a_i1
    for _ in range(n.bit_length() - 1):
        pr, pi = pr * pr - pi * pi, 2.0 * pr * pi

    cr_rows = [jnp.zeros((1, ns), F32)]
    ci_rows = [jnp.zeros((1, ns), F32)]
    for r in range(1, nseg):
        pcr, pci = cr_rows[-1], ci_rows[-1]
        cr_rows.append(er[r - 1:r, :] + pr * pcr - pi * pci)
        ci_rows.append(ei[r - 1:r, :] + pr * pci + pi * pcr)
    c_r = jnp.concatenate(cr_rows, axis=0)
    c_i = jnp.concatenate(ci_rows, axis=0)

    def fix_body(s, carry):
        qr, qi = carry
        r0 = pl.multiple_of(s * nseg, nseg)
        bur[pl.ds(r0, nseg), :] = bur[pl.ds(r0, nseg), :] + (qr * c_r - qi * c_i)
        bui[pl.ds(r0, nseg), :] = bui[pl.ds(r0, nseg), :] + (qr * c_i + qi * c_r)
        return qr * a_r - qi * a_i, qr * a_i + qi * a_r

    lax.fori_loop(0, n, fix_body, (a_r, a_i), unroll=4)

    def out_body(c, _):
        r0 = pl.multiple_of(c * chunk, chunk)
        y = jnp.dot(bur[pl.ds(r0, chunk), :].astype(BF16), cr_ref[0], preferred_element_type=F32)
        y = y + jnp.dot(bui[pl.ds(r0, chunk), :].astype(BF16), cin_ref[0], preferred_element_type=F32)
        y = y + d_ref[0] * uperm[pl.ds(r0, chunk), :]
        yperm[pl.ds(r0, chunk), :] = jax.nn.gelu(y, approximate=True)
        return 0

    lax.fori_loop(0, seq // chunk, out_body, 0)

    for r in range(nseg):
        y_ref[r * n:(r + 1) * n, :] = yperm[pl.ds(r, n, stride=nseg), :].astype(y_ref.dtype)


def _ssm_params(a_re, a_im, log_dt, b_re, b_im, c_re, c_im, d_skip):
    g, p = a_re.shape
    c = b_re.shape[-1]
    gpb = SSM_GROUPS_PER_BLOCK
    nblk = g // gpb
    dt = jnp.exp(log_dt)[:, None]
    mag = jnp.exp(dt * a_re)
    ab_r, ab_i = mag * jnp.cos(dt * a_im), mag * jnp.sin(dt * a_im)
    den = a_re * a_re + a_im * a_im
    nr, ni = ab_r - 1.0, ab_i
    f_r = (nr * a_re + ni * a_im) / den
    f_i = (ni * a_re - nr * a_im) / den
    bb_r = f_r[..., None] * b_re - f_i[..., None] * b_im
    bb_i = f_r[..., None] * b_im + f_i[..., None] * b_re
    eye = jnp.eye(gpb, dtype=F32)

    def in_mat(bb):
        return jnp.einsum("bgpc,gh->bgchp", bb.reshape(nblk, gpb, p, c), eye).reshape(nblk, gpb * c, gpb * p)

    def out_mat(cc):
        return jnp.einsum("bgcp,gh->bgphc", cc.reshape(nblk, gpb, c, p), eye).reshape(nblk, gpb * p, gpb * c)

    def hi_lo(m):
        hi = m.astype(BF16)
        return jnp.stack([hi, (m - hi.astype(F32)).astype(BF16)], axis=1)

    return (ab_r.reshape(nblk, 1, gpb * p), ab_i.reshape(nblk, 1, gpb * p),
            hi_lo(in_mat(bb_r)), hi_lo(in_mat(bb_i)), out_mat(c_re).astype(BF16), out_mat(-c_im).astype(BF16),
            d_skip.reshape(nblk, 1, gpb * c))


def _ssm(proj, ssm_mats, batch, seq, ucol0):
    ar, ai, br, bi, cr, cin, dsk = ssm_mats
    nblk = ar.shape[0]
    nu, ns = br.shape[2], br.shape[3]
    ublk0 = ucol0 // nu
    blk3 = lambda shape: pl.BlockSpec((1,) + shape, lambda b, g: (g,) + (0,) * len(shape))
    return pl.pallas_call(
        _ssm_kernel,
        grid=(batch, nblk),
        in_specs=[
            pl.BlockSpec((seq, nu), lambda b, g: (b, ublk0 + g)),
            blk3((1, ns)), blk3((1, ns)), blk3((2, nu, ns)), blk3((2, nu, ns)),
            blk3((ns, nu)), blk3((ns, nu)), blk3((1, nu)),
        ],
        out_specs=pl.BlockSpec((seq, nu), lambda b, g: (b, g)),
        out_shape=jax.ShapeDtypeStruct((batch * seq, nblk * nu), BF16),
        scratch_shapes=[
            pltpu.VMEM((seq, nu), F32), pltpu.VMEM((seq, ns), F32),
            pltpu.VMEM((seq, ns), F32), pltpu.VMEM((seq, nu), F32),
        ],
        compiler_params=_params(("arbitrary", "arbitrary")),
    )(proj, ar, ai, br, bi, cr, cin, dsk)


def _mixout_kernel(a_ref, y_ref, gs_ref, x_ref, gt_ref, wglu_ref, wmix_ref, g2_ref, sh_ref, sc_ref,
                   wr_ref, x1_ref, lg_ref):
    d = x_ref.shape[1]
    y = y_ref[...]
    r = None
    nsplit = 2
    w = d // nsplit
    for c in range(nsplit):
        cols = slice(c * w, (c + 1) * w)
        za = jnp.dot(y, wglu_ref[:, c * w:(c + 1) * w], preferred_element_type=F32)
        zb = jnp.dot(y, wglu_ref[:, d + c * w:d + (c + 1) * w], preferred_element_type=F32)
        mixed = a_ref[:, cols].astype(F32) + _sigmoid(gs_ref[:, cols].astype(F32)) * (za * _sigmoid(zb))
        part = jnp.dot(mixed.astype(BF16), wmix_ref[c * w:(c + 1) * w, :], preferred_element_type=F32)
        r = part if r is None else r + part
    x1 = x_ref[...] + gt_ref[0] * r
    x1_ref[...] = x1
    h2 = _norm_mod(x1, g2_ref[...], sh_ref[0], sc_ref[0])
    h_hi = h2.astype(BF16)
    h_lo = (h2 - h_hi.astype(F32)).astype(BF16)
    nt = lambda a, b: lax.dot_general(a, b, (((1,), (1,)), ((), ())), preferred_element_type=F32)
    lg_ref[...] = nt(wr_ref[0], h_hi) + nt(wr_ref[0], h_lo) + nt(wr_ref[1], h_hi)


def _mixout(a_gated, y, proj, x2, gt1, w_glu, w_mix, g2, sh2, sc2, w_router_t, seq, gscol):
    t, d = x2.shape
    tm = 512
    per_b = seq // tm
    ne = w_router_t.shape[0]
    wr_hi = w_router_t.astype(BF16)
    wr_split = jnp.stack([wr_hi, (w_router_t - wr_hi.astype(F32)).astype(BF16)])
    row = lambda i: (i, 0)
    bvec = pl.BlockSpec((1, 1, d), lambda i: (i // per_b, 0, 0))
    return pl.pallas_call(
        _mixout_kernel,
        grid=(t // tm,),
        in_specs=[
            pl.BlockSpec((tm, d), row),
            pl.BlockSpec((tm, y.shape[1]), row),
            pl.BlockSpec((tm, d), lambda i: (i, gscol)),
            pl.BlockSpec((tm, d), row),
            bvec,
            _resident(w_glu.shape),
            _resident(w_mix.shape),
            pl.BlockSpec((1, d), lambda i: (0, 0)),
            bvec, bvec,
            _resident(wr_split.shape),
        ],
        out_specs=[
            pl.BlockSpec((tm, d), row),
            pl.BlockSpec((ne, tm), lambda i: (0, i)),
        ],
        out_shape=[
            jax.ShapeDtypeStruct((t, d), F32),
            jax.ShapeDtypeStruct((ne, t), F32),
        ],
        compiler_params=_params(("arbitrary",)),
    )(a_gated, y, proj, x2, gt1, w_glu.astype(BF16), w_mix.astype(BF16), g2.reshape(1, d), sh2, sc2,
      wr_split)


def _router_kernel(lg_ref, b_ref, idx_ref, w_ref, rank_ref, cnt_ref, run_scr):
    ng, gs, tn = lg_ref.shape
    ne = ng * gs

    @pl.when(pl.program_id(0) == 0)
    def _():
        run_scr[...] = jnp.zeros(run_scr.shape, run_scr.dtype)

    scores = _sigmoid(lg_ref[...])
    sel = scores + b_ref[...]
    mem = lax.broadcasted_iota(I32, (ng, gs, tn), 1)
    grp = lax.broadcasted_iota(I32, (ng, gs, tn), 0)
    eid = grp * gs + mem
    ninf = -jnp.inf

    m1 = jnp.max(sel, axis=1, keepdims=True)
    first = jnp.min(jnp.where(sel == m1, mem, gs), axis=1, keepdims=True)
    m2 = jnp.max(jnp.where(mem == first, ninf, sel), axis=1, keepdims=True)
    gscore = m1 + m2

    gid = lax.broadcasted_iota(I32, (ng, 1, tn), 0)
    gmask = jnp.zeros((ng, 1, tn), jnp.bool_)
    for _ in range(TOPK_GROUPS):
        m = jnp.max(gscore, axis=0, keepdims=True)
        pick = gid == jnp.min(jnp.where(gscore == m, gid, ng), axis=0, keepdims=True)
        gmask = gmask | pick
        gscore = jnp.where(pick, ninf, gscore)

    def reduce_experts(v):
        return jnp.sum(jnp.sum(v, axis=1, keepdims=True), axis=0, keepdims=True)[0]

    cand = jnp.where(gmask, sel, ninf)
    idxs, wts, picks = [], [], []
    for _ in range(TOP_K):
        m = jnp.max(jnp.max(cand, axis=1, keepdims=True), axis=0, keepdims=True)
        e = jnp.where(cand == m, eid, ne)
        e = jnp.min(jnp.min(e, axis=1, keepdims=True), axis=0, keepdims=True)
        pick = eid == e
        idxs.append(e[0])
        wts.append(reduce_experts(jnp.where(pick, scores, 0.0)))
        picks.append(pick)
        cand = jnp.where(pick, ninf, cand)
    idx_ref[...] = jnp.concatenate(idxs, axis=0)
    w = jnp.concatenate(wts, axis=0)
    w_ref[...] = w / jnp.sum(w, axis=0, keepdims=True) * ROUTED_SCALE

    chosen = picks[0]
    for pick in picks[1:]:
        chosen = chosen | pick
    onehot = chosen.astype(F32).reshape(ne, tn)
    before = (lax.broadcasted_iota(I32, (tn, tn), 0) < lax.broadcasted_iota(I32, (tn, tn), 1)).astype(BF16)
    prefix = jnp.dot(onehot.astype(BF16), before, preferred_element_type=F32)
    rank = (run_scr[...] + prefix).reshape(ng, gs, tn)
    ranks = [reduce_experts(jnp.where(pick, rank, 0.0)) for pick in picks]
    rank_ref[...] = jnp.concatenate(ranks, axis=0).astype(I32)
    run_scr[...] = run_scr[...] + jnp.sum(onehot, axis=1, keepdims=True)
    cnt_ref[...] = run_scr[...].astype(I32)


def _router(logits_t, b_router):
    ne, t = logits_t.shape
    ng = N_EXPERT_GROUPS
    gs = ne // ng
    tn = 512
    tok = pl.BlockSpec((TOP_K, tn), lambda i: (0, i))
    return pl.pallas_call(
        _router_kernel,
        grid=(t // tn,),
        in_specs=[
            pl.BlockSpec((ng, gs, tn), lambda i: (0, 0, i)),
            pl.BlockSpec((ng, gs, 1), lambda i: (0, 0, 0)),
        ],
        out_specs=[tok, tok, tok, pl.BlockSpec((ne, 1), lambda i: (0, 0))],
        out_shape=[jax.ShapeDtypeStruct((TOP_K, t), I32), jax.ShapeDtypeStruct((TOP_K, t), F32),
                   jax.ShapeDtypeStruct((TOP_K, t), I32), jax.ShapeDtypeStruct((ne, 1), I32)],
        scratch_shapes=[pltpu.VMEM((ne, 1), F32)],
        compiler_params=_params(("arbitrary",)),
    )(logits_t.reshape(ng, gs, t), b_router.reshape(ng, gs, 1))


def _dest_kernel(pstart_ref, idx_ref, rank_ref, dest_ref):
    idx = idx_ref[...]
    base = jnp.zeros(idx.shape, I32)
    for e in range(pstart_ref.shape[0]):
        base = jnp.where(idx == e, pstart_ref[e], base)
    dest_ref[...] = base + rank_ref[...]


def _dest(pstart, eidx, rank):
    k, t = eidx.shape
    tn = 2048
    tok = pl.BlockSpec((k, tn), lambda i, ps: (0, i))
    return pl.pallas_call(
        _dest_kernel,
        grid_spec=pltpu.PrefetchScalarGridSpec(
            num_scalar_prefetch=1, grid=(t // tn,), in_specs=[tok, tok], out_specs=tok),
        out_shape=jax.ShapeDtypeStruct((k, t), I32),
        compiler_params=_params(("arbitrary",)),
    )(pstart, eidx, rank)


def _slot_plan(counts):
    padded = (counts + MOE_SUB - 1) // MOE_SUB * MOE_SUB
    pend = jnp.cumsum(padded)
    pstart = pend - padded
    npad = (padded - counts)[:, None]
    ntail = MOE_SUB - npad
    tail0 = pend[-1] + jnp.cumsum(ntail, axis=0) - ntail
    j = jnp.arange(MOE_SUB, dtype=I32)[None, :]
    free_rows = jnp.where(j < npad, (pstart + counts)[:, None] + j, tail0 + j - npad)
    return jnp.stack([pstart, padded]).astype(I32), pstart.astype(I32), free_rows.astype(I32)


def _dispatch_kernel(dest_ref, pad_ref, x_ref, g_ref, sh_ref, sc_ref, wg_ref, wu_ref, wd_ref, xs_hbm, shared_ref,
                     h_scr, zero_scr, sem):
    _, ntile, sub, dw = h_scr.shape
    tm = ntile * sub
    k = dest_ref.shape[2] // tm
    npad = pad_ref.shape[2]
    assert npad % tm == 0
    i = pl.program_id(0)
    b = i & 1

    @pl.when(i == 0)
    def _():
        zero_scr[...] = _pack_halves(jnp.zeros((zero_scr.shape[0], 2 * dw), F32))

    h = _norm_mod(x_ref[...], g_ref[...], sh_ref[0], sc_ref[0])
    h_scr[b] = _pack_halves(h).reshape(ntile, sub, dw)

    def issue_rows(r8):
        for j in range(sub):
            for kk in range(k):
                slot = dest_ref[0, 0, (r8 * sub + j) * k + kk]
                pltpu.make_async_copy(h_scr.at[b, r8, pl.ds(j, 1)], xs_hbm.at[pl.ds(slot, 1)],
                                      sem.at[b]).start(priority=kk % 2)

    per_group = sub * k

    def issue_pad(g):
        for j in range(g * per_group, (g + 1) * per_group):
            pltpu.make_async_copy(zero_scr.at[pl.ds(0, 1)], xs_hbm.at[pl.ds(pad_ref[0, 0, j], 1)],
                                  sem.at[b]).start(priority=j % 2)

    groups = [functools.partial(issue_rows, r8) for r8 in range(ntile)]
    groups += [functools.partial(issue_pad, g) for g in range(npad // per_group)]
    groups = iter(groups)

    def issue_some(n):
        for _ in range(n):
            g = next(groups, None)
            if g is not None:
                g()

    hb = h.astype(BF16)
    f = wg_ref.shape[1]
    d = wd_ref.shape[1]
    gu_pieces = f // LANES
    wn = 2 * LANES
    down_pieces = d // wn
    per_piece = -(-(ntile + npad // per_group) // (gu_pieces + down_pieces))
    hid = []
    for c in range(gu_pieces):
        issue_some(per_piece)
        cols = slice(c * LANES, (c + 1) * LANES)
        gate = jnp.dot(hb, wg_ref[:, cols], preferred_element_type=F32)
        up = jnp.dot(hb, wu_ref[:, cols], preferred_element_type=F32)
        hid.append((_silu(gate) * up).astype(BF16))
    hid = jnp.concatenate(hid, axis=-1)
    for c in range(down_pieces):
        issue_some(per_piece)
        shared_ref[:, c * wn:(c + 1) * wn] = jnp.dot(hid, wd_ref[:, c * wn:(c + 1) * wn],
                                                     preferred_element_type=F32)
    issue_some(ntile + npad)

    def drain(bb):
        for _ in range(k + npad // tm):
            pltpu.make_async_copy(xs_hbm.at[pl.ds(0, tm)], xs_hbm.at[pl.ds(0, tm)], sem.at[bb]).wait()

    @pl.when(i > 0)
    def _():
        drain(1 - b)

    @pl.when(i + 1 == pl.num_programs(0))
    def _():
        drain(b)


def _dispatch(dest, free_rows, x1, g2, sh2, sc2, w_sg, w_su, w_sd, n_rows, seq):
    t, d = x1.shape
    k = dest.shape[0]
    tm = 256
    per_b = seq // tm
    nblk = t // tm
    npad = free_rows.size // nblk
    dest_tm = dest.T.reshape(nblk, 1, tm * k)
    bvec = pl.BlockSpec((1, 1, d), lambda i: (i // per_b, 0, 0))
    return pl.pallas_call(
        _dispatch_kernel,
        grid=(nblk,),
        in_specs=[
            pl.BlockSpec((1, 1, tm * k), lambda i: (i, 0, 0), memory_space=pltpu.SMEM),
            pl.BlockSpec((1, 1, npad), lambda i: (i, 0, 0), memory_space=pltpu.SMEM),
            pl.BlockSpec((tm, d), lambda i: (i, 0)),
            pl.BlockSpec((1, d), lambda i: (0, 0)),
            bvec, bvec,
            _resident(w_sg.shape), _resident(w_su.shape), _resident(w_sd.shape),
        ],
        out_specs=[pl.BlockSpec(memory_space=pl.ANY), pl.BlockSpec((tm, d), lambda i: (i, 0))],
        out_shape=[jax.ShapeDtypeStruct((n_rows, d // 2), U32), jax.ShapeDtypeStruct((t, d), F32)],
        scratch_shapes=[pltpu.VMEM((2, tm // SUBLANES, SUBLANES, d // 2), U32),
                        pltpu.VMEM((SUBLANES, d // 2), U32), pltpu.SemaphoreType.DMA((2,))],
        compiler_params=_params(("arbitrary",)),
    )(dest_tm, free_rows.reshape(nblk, 1, npad), x1, g2.reshape(1, d), sh2, sc2,
      w_sg.astype(BF16), w_su.astype(BF16), w_sd.astype(BF16))


def _chunk_plan(meta, chunk, n_chunks):
    pstart, padded = meta[0], meta[1]
    ne = pstart.shape[0]
    nch = (padded + chunk - 1) // chunk
    cend = jnp.cumsum(nch)
    cfirst = cend - nch
    j = jnp.arange(n_chunks, dtype=I32)
    e_of = jnp.sum((cend[None, :] <= j[:, None]).astype(I32), axis=1)
    valid = e_of < ne
    e_c = jnp.minimum(e_of, ne - 1)
    local = j - cfirst[e_c]
    rows = jnp.clip(padded[e_c] - local * chunk, 0, chunk)
    cmeta = jnp.stack([jnp.where(valid, pstart[e_c] + local * chunk, 0), jnp.where(valid, rows // MOE_SUB, 0)])
    return cmeta.astype(I32), jnp.stack([cfirst, nch]).astype(I32)


def _expert_kernel(meta_ref, cmeta_ref, emeta_ref, x_hbm, wg_ref, wu_ref, wd_ref, y_hbm, xbuf, ybuf,
                   xsem, ysem, ypend):
    e = pl.program_id(0)
    ne = pl.num_programs(0)
    chunk = xbuf.shape[1]
    nsub = chunk // MOE_SUB

    def sub(r):
        return pl.ds(pl.multiple_of(r, MOE_SUB), MOE_SUB)

    def x_piece(r0, slot, j):
        return pltpu.make_async_copy(x_hbm.at[sub(r0 + j * MOE_SUB)], xbuf.at[slot, sub(j * MOE_SUB)],
                                     xsem.at[slot])

    def y_piece(r0, slot, j):
        return pltpu.make_async_copy(ybuf.at[slot, sub(j * MOE_SUB)], y_hbm.at[sub(r0 + j * MOE_SUB)],
                                     ysem.at[slot])

    def x_start(r0, slot, n):
        def body(j, carry):
            x_piece(r0, slot, j).start(priority=1)
            return carry

        lax.fori_loop(0, n, body, 0)

    def x_wait(slot, n):
        def body(j, carry):
            x_piece(0, slot, 0).wait()
            return carry

        lax.fori_loop(0, n, body, 0)

    def y_drain(slot):
        def body(j, carry):
            y_piece(0, slot, 0).wait()
            return carry

        lax.fori_loop(0, ypend[slot], body, 0)
        ypend[slot] = 0

    nx = xbuf.shape[0]
    ahead = nx - 1

    def request(g):
        x_start(cmeta_ref[0, g], lax.rem(g, nx), cmeta_ref[1, g])

    @pl.when(e == 0)
    def _():
        ypend[0] = 0
        ypend[1] = 0
        for a in range(ahead):
            request(a)

    dw = xbuf.shape[2]

    def compute(xslot, yslot, m):
        x_l, x_r = (v.astype(BF16) for v in _unpack_halves(xbuf[xslot, pl.ds(0, m), :]))

        def proj(w_ref):
            return (jnp.dot(x_l, w_ref[0, 0, :dw, :].astype(BF16), preferred_element_type=F32)
                    + jnp.dot(x_r, w_ref[0, 0, dw:, :].astype(BF16), preferred_element_type=F32))

        hid = (_silu(proj(wg_ref)) * proj(wu_ref)).astype(BF16)
        ybuf[yslot, pl.ds(0, m), :] = _pack_halves(
            jnp.dot(hid, wd_ref[0, 0].astype(BF16), preferred_element_type=F32))

    def run_chunk(g, npieces):
        xslot = lax.rem(g, nx)
        yslot = g & 1
        r0 = cmeta_ref[0, g]
        x_wait(xslot, npieces)
        request(g + ahead)
        y_drain(yslot)
        compute(xslot, yslot, npieces * MOE_SUB)
        for j in range(npieces):
            y_piece(r0, yslot, j).start(priority=1)
        ypend[yslot] = npieces

    def chunk_body(c, carry):
        g = emeta_ref[0, e] + c
        for npieces in range(1, nsub + 1):
            @pl.when(cmeta_ref[1, g] == npieces)
            def _(npieces=npieces):
                run_chunk(g, npieces)
        return carry

    lax.fori_loop(0, emeta_ref[1, e], chunk_body, 0)

    @pl.when(e + 1 == ne)
    def _():
        y_drain(0)
        y_drain(1)
        tail0 = meta_ref[0, e] + meta_ref[1, e]
        ntail = (y_hbm.shape[0] - tail0) // MOE_SUB
        ybuf[0, pl.ds(0, MOE_SUB), :] = _pack_halves(jnp.zeros((MOE_SUB, 2 * dw), F32))

        def fill(j, carry):
            y_piece(tail0 + j * MOE_SUB, 0, 0).start()
            return carry

        lax.fori_loop(0, ntail, fill, 0)
        ypend[0] = ntail
        y_drain(0)


def _experts(meta, x_sorted, w_gate, w_up, w_down, layer, n_slots):
    dw = x_sorted.shape[1]
    _, ne, d, f = w_gate.shape
    chunk = 512
    n_chunks = n_slots // chunk + ne + 3
    cmeta, emeta = _chunk_plan(meta, chunk, n_chunks)
    wspec = lambda a, b: pl.BlockSpec((1, 1, a, b), lambda e, m, cm, em: (layer, e, 0, 0))
    grid_spec = pltpu.PrefetchScalarGridSpec(
        num_scalar_prefetch=3,
        grid=(ne,),
        in_specs=[pl.BlockSpec(memory_space=pl.ANY), wspec(d, f), wspec(d, f), wspec(f, d)],
        out_specs=pl.BlockSpec(memory_space=pl.ANY),
        scratch_shapes=[
            pltpu.VMEM((3, chunk, dw), U32), pltpu.VMEM((2, chunk, dw), U32),
            pltpu.SemaphoreType.DMA((3,)), pltpu.SemaphoreType.DMA((2,)),
            pltpu.SMEM((2,), I32),
        ],
    )
    return pl.pallas_call(
        _expert_kernel,
        grid_spec=grid_spec,
        out_shape=jax.ShapeDtypeStruct((n_slots, dw), U32),
        compiler_params=_params(("arbitrary",)),
    )(meta, cmeta, emeta, x_sorted, w_gate, w_up, w_down)


def _combine_kernel(dest_ref, dnext_ref, wt_ref, x_ref, shared_ref, gt_ref, gf_ref, y_hbm, o_ref, buf, sem, *,
                    final_norm):
    _, k, ntile, sub, dw = buf.shape
    tm = ntile * sub
    i = pl.program_id(0)
    cur = i & 1
    nxt = 1 - cur
    has_next = i + 1 < pl.num_programs(0)

    def issue_group(idx_ref, b, r8):
        for j in range(sub):
            for kk in range(k):
                slot = idx_ref[0, 0, (r8 * sub + j) * k + kk]
                pltpu.make_async_copy(y_hbm.at[pl.ds(slot, 1)], buf.at[b, kk, r8, pl.ds(j, 1)],
                                      sem.at[b]).start(priority=kk % 2)

    @pl.when(i == 0)
    def _():
        def body(r8, carry):
            issue_group(dest_ref, 0, r8)
            return carry

        lax.fori_loop(0, ntile, body, 0)

    def issue(r8):
        issue_group(dnext_ref, nxt, r8)

    def drain(b):
        for kk in range(k):
            pltpu.make_async_copy(buf.at[b, kk], buf.at[b, kk], sem.at[b]).wait()

    groups = iter(range(ntile))
    for _ in range(ntile - k):
        issue(next(groups))
    drain(cur)
    wt = wt_ref[...]
    routed_l = routed_r = None
    for kk in range(k):
        r8 = next(groups, None)
        if r8 is not None:
            issue(r8)
        y_l, y_r = _unpack_halves(buf[cur, kk].reshape(tm, dw))
        w_k = wt[:, kk:kk + 1]
        routed_l = w_k * y_l if routed_l is None else routed_l + w_k * y_l
        routed_r = w_k * y_r if routed_r is None else routed_r + w_k * y_r
    for r8 in groups:
        issue(r8)
    acc = shared_ref[...] + jnp.concatenate([routed_l, routed_r], axis=-1)

    @pl.when(jnp.logical_not(has_next))
    def _():
        drain(nxt)
    out = x_ref[...] + gt_ref[0] * acc
    if final_norm:
        ms = jnp.mean(out * out, axis=-1, keepdims=True)
        out = out * lax.rsqrt(ms + NORM_EPS) * gf_ref[...]
    o_ref[...] = out


def _combine(dest, wts, x1, shared, gt2, g_final, y_sorted, seq, final_norm):
    t, d = x1.shape
    k = dest.shape[0]
    tm = 128
    per_b = seq // tm
    nblk = t // tm
    dest_tm = dest.T.reshape(nblk, 1, tm * k)
    row = lambda i: (i, 0)
    vec = pl.BlockSpec((1, d), lambda i: (0, 0))
    bvec = pl.BlockSpec((1, 1, d), lambda i: (i // per_b, 0, 0))
    return pl.pallas_call(
        functools.partial(_combine_kernel, final_norm=final_norm),
        grid=(nblk,),
        in_specs=[
            pl.BlockSpec((1, 1, tm * k), lambda i: (i, 0, 0), memory_space=pltpu.SMEM),
            pl.BlockSpec((1, 1, tm * k), lambda i: (jnp.minimum(i + 1, nblk - 1), 0, 0),
                         memory_space=pltpu.SMEM),
            pl.BlockSpec((tm, k), row),
            pl.BlockSpec((tm, d), row),
            pl.BlockSpec((tm, d), row),
            bvec, vec,
            pl.BlockSpec(memory_space=pl.ANY),
        ],
        out_specs=pl.BlockSpec((tm, d), row),
        out_shape=jax.ShapeDtypeStruct((t, d), F32),
        scratch_shapes=[pltpu.VMEM((2, k, tm // SUBLANES, SUBLANES, d // 2), U32),
                        pltpu.SemaphoreType.DMA((2,))],
        compiler_params=_params(("arbitrary",)),
    )(dest_tm, dest_tm, wts.T, x1, shared, gt2, g_final.reshape(1, d), y_sorted)


def kernel(x, c, w_ada, b_ada, norm_mix, w_in, b_in, attn_sinks, w_attn_o, ssm_a_re, ssm_a_im, ssm_log_dt,
           ssm_b_re, ssm_b_im, ssm_c_re, ssm_c_im, ssm_d, w_glu, w_mix_out, norm_ffn, w_router, b_router,
           w_exp_gate, w_exp_up, w_exp_down, w_sh_gate, w_sh_up, w_sh_down, norm_final):
    batch, seq, d = x.shape
    depth = w_ada.shape[0]
    n_experts = w_router.shape[2]
    ssm_width = ssm_d.shape[1]
    ucol0 = ATTN_WIDTH + 2 * KV_WIDTH
    gacol = (ucol0 + ssm_width) // d
    gscol = gacol + 1
    n_slots = batch * seq * TOP_K + n_experts * MOE_SUB

    c_pad = jnp.zeros((SUBLANES, d), F32).at[:batch].set(c)
    mod = _ada_mod(c_pad, w_ada, b_ada)[:, :batch]
    mod = mod.reshape(depth, batch, 6, 1, d)

    x2 = x.reshape(batch * seq, d)
    for l in range(depth):
        sh1, sc1, gt1, sh2, sc2, gt2 = (mod[l, :, m] for m in range(6))
        proj = _inproj(x2, norm_mix[l], sh1, sc1, w_in, b_in, l, seq)
        a_gated = _attention(proj, attn_sinks[l], w_attn_o[l], batch, seq, d, gacol)
        ssm_mats = _ssm_params(ssm_a_re[l], ssm_a_im[l], ssm_log_dt[l], ssm_b_re[l], ssm_b_im[l],
                               ssm_c_re[l], ssm_c_im[l], ssm_d[l])
        y = _ssm(proj, ssm_mats, batch, seq, ucol0)
        x1, logits_t = _mixout(a_gated, y, proj, x2, gt1, w_glu[l], w_mix_out[l], norm_ffn[l],
                               sh2, sc2, w_router[l].T, seq, gscol)
        eidx, wts, rank, counts = _router(logits_t, b_router[l])
        meta, pstart, free_rows = _slot_plan(counts[:, 0])
        dest = _dest(pstart, eidx, rank)
        x_sorted, shared = _dispatch(dest, free_rows, x1, norm_ffn[l], sh2, sc2,
                                     w_sh_gate[l], w_sh_up[l], w_sh_down[l], n_slots, seq)
        y_sorted = _experts(meta, x_sorted, w_exp_gate, w_exp_up, w_exp_down, l, n_slots)
        x2 = _combine(dest, wts, x1, shared, gt2, norm_final, y_sorted, seq, final_norm=(l == depth - 1))
    return x2.reshape(batch, seq, d)
```
